```python
import jax, jax.numpy as jnp
from jax import lax
import numpy as np

D_MODEL = 2048
BATCH = 1
SEQ = 8192
DEPTH = 4

D_MIX = D_MODEL
HGRN_WIDTH = D_MIX // 2
HGRN_KDIM = 128
HGRN_HEADS = HGRN_WIDTH // HGRN_KDIM
HGRN_VDIM = HGRN_WIDTH // HGRN_HEADS
HGRN_CHUNK = 64
NSA_WIDTH = D_MIX - HGRN_WIDTH
NSA_HEAD_DIM = 64
NSA_HEADS = NSA_WIDTH // NSA_HEAD_DIM
NSA_KV_GROUPS = 4
NSA_HPG = NSA_HEADS // NSA_KV_GROUPS
NSA_KV_WIDTH = NSA_KV_GROUPS * NSA_HEAD_DIM
CMP_BLOCK = 32
CMP_STRIDE = 16
CMP_HIDDEN = 4 * NSA_HEAD_DIM
SLC_BLOCK = 64
SLC_TOPN = 16
WINDOW = 512
Q_BLOCK = 128
D_FF = 256 * ((8 * D_MODEL // 3 + 255) // 256)
CONV_WIDTH = 3
ROPE_THETA = 10000.0
LN_EPS = 1e-5
RMS_EPS = 1e-6
F_MIN = 1e-30
DN_ALPHA = (2 * DEPTH) ** 0.25
DN_BETA = (8 * DEPTH) ** -0.25
NEG_INF = -1e30
FORCE_SCORE = 1e9
IN_SPLITS = (HGRN_WIDTH, HGRN_WIDTH, HGRN_WIDTH, HGRN_WIDTH, NSA_WIDTH,
             NSA_KV_WIDTH, NSA_KV_WIDTH, NSA_KV_WIDTH, NSA_KV_WIDTH, NSA_KV_WIDTH, NSA_KV_WIDTH,
             NSA_HEADS * 3)
N_IN = sum(IN_SPLITS)

kernel_name = 'hgrn2_nsa_parallel_deepnorm_convffn'


def layer_norm(x, g, b):
    xf = x.astype(jnp.float32)
    mu = jnp.mean(xf, -1, keepdims=True)
    var = jnp.mean(jnp.square(xf - mu), -1, keepdims=True)
    return ((xf - mu) * lax.rsqrt(var + LN_EPS) * g + b).astype(x.dtype)


def rope(x, pos):
    half = x.shape[-1] // 2
    inv = ROPE_THETA ** (-jnp.arange(half, dtype=jnp.float32) / half)
    ang = pos.astype(jnp.float32)[:, None] * inv[None, :]
    cos, sin = jnp.cos(ang), jnp.sin(ang)
    xf = x.astype(jnp.float32)
    x1, x2 = xf[..., :half], xf[..., half:]
    return jnp.concatenate([x1 * cos - x2 * sin, x2 * cos + x1 * sin], -1).astype(x.dtype)


def hgrn2_mixer(q, f_pre, i, g, lb, norm_w):
    B, S, _ = q.shape
    H, K, V, C = HGRN_HEADS, HGRN_KDIM, HGRN_VDIM, HGRN_CHUNK
    n = S // C
    z = f_pre.astype(jnp.float32)
    lbf = lb.astype(jnp.float32)
    f = lbf + (1.0 - lbf) * jax.nn.sigmoid(z)
    log_f = jnp.log(jnp.maximum(f, F_MIN))
    k = (1.0 - lbf) * jax.nn.sigmoid(-z)
    qs = jax.nn.silu(q.astype(jnp.float32))

    def heads(t, d):
        return t.astype(jnp.float32).reshape(B, n, C, H, d).transpose(1, 0, 3, 2, 4)

    causal = jnp.tril(jnp.ones((C, C), dtype=bool))

    def step(state, xs):
        qc, kc, vc, lfc = xs
        b = jnp.cumsum(lfc, axis=-2)
        o_inter = jnp.einsum('bhtk,bhkv->bhtv', qc * jnp.exp(b), state)
        diff = b[:, :, :, None, :] - b[:, :, None, :, :]
        decay = jnp.exp(jnp.where(causal[:, :, None], diff, NEG_INF))
        att = jnp.einsum('bhtk,bhtsk,bhsk->bhts', qc, decay, kc)
        o_intra = jnp.einsum('bhts,bhsv->bhtv', att, vc)
        b_last = b[:, :, -1:, :]
        new_state = jnp.exp(b_last[:, :, 0, :])[..., None] * state + jnp.einsum(
            'bhsk,bhsv->bhkv', kc * jnp.exp(b_last - b), vc)
        return new_state, o_inter + o_intra

    s0 = jnp.zeros((B, H, K, V), jnp.float32)
    _, o = lax.scan(step, s0, (heads(qs, K), heads(k, K), heads(i, V), heads(log_f, K)))
    o = o.transpose(1, 0, 3, 2, 4).reshape(B, S, H, V)
    o = o * lax.rsqrt(jnp.mean(jnp.square(o), -1, keepdims=True) + RMS_EPS)
    o = o.reshape(B, S, H * V) * norm_w * jax.nn.silu(g.astype(jnp.float32))
    return o.astype(q.dtype)


def compress(kv, pe, w1, w2):
    S = kv.shape[2]
    n_cmp = (S - CMP_BLOCK) // CMP_STRIDE + 1
    idx = jnp.arange(n_cmp)[:, None] * CMP_STRIDE + jnp.arange(CMP_BLOCK)[None, :]
    blk = kv[:, :, idx] + pe
    flat = blk.reshape(blk.shape[0], blk.shape[1], n_cmp, CMP_BLOCK * NSA_HEAD_DIM)
    return jax.nn.silu(flat @ w1) @ w2


def nsa_mixer(q, k_c, v_c, k_s, v_s, k_w, v_w, gate_pre, pos, pe_k, pe_v, w1_k, w2_k, w1_v, w2_v):
    B, S, _ = q.shape
    G, HP, DK = NSA_KV_GROUPS, NSA_HPG, NSA_HEAD_DIM
    nq = S // Q_BLOCK
    n_cmp = (S - CMP_BLOCK) // CMP_STRIDE + 1
    n_slc = S // SLC_BLOCK
    n_sel = min(SLC_TOPN, n_slc)
    scale = DK ** -0.5
    qh = rope(q.reshape(B, S, G, HP, DK).transpose(0, 2, 3, 1, 4), pos)

    def kv_heads(t):
        return t.reshape(B, S, G, DK).transpose(0, 2, 1, 3)

    k_c, k_s, k_w = (rope(kv_heads(t), pos) for t in (k_c, k_s, k_w))
    v_c, v_s, v_w = (kv_heads(t) for t in (v_c, v_s, v_w))
    k_cmp = compress(k_c, pe_k, w1_k, w2_k)
    v_cmp = compress(v_c, pe_v, w1_v, w2_v).astype(jnp.float32)
    cmp_start = jnp.arange(n_cmp) * CMP_STRIDE
    cmp_end = cmp_start + CMP_BLOCK - 1
    slc_start = jnp.arange(n_slc) * SLC_BLOCK
    overlap = ((cmp_start[:, None] < slc_start[None, :] + SLC_BLOCK)
               & (cmp_start[:, None] + CMP_BLOCK > slc_start[None, :])).astype(jnp.float32)
    ks_blk = k_s.reshape(B, G, n_slc, SLC_BLOCK, DK)
    vs_blk = v_s.reshape(B, G, n_slc, SLC_BLOCK, DK)
    kw_pad = jnp.pad(k_w, ((0, 0), (0, 0), (WINDOW, 0), (0, 0)))
    vw_pad = jnp.pad(v_w, ((0, 0), (0, 0), (WINDOW, 0), (0, 0)))
    gates = jax.nn.sigmoid(gate_pre.astype(jnp.float32)).reshape(B, S, G, HP, 3).transpose(0, 2, 3, 1, 4)

    def chunks(t):
        return t.reshape(B, G, HP, nq, Q_BLOCK, t.shape[-1]).transpose(3, 0, 1, 2, 4, 5)

    b_idx = jnp.arange(B)[:, None, None, None]
    g_idx = jnp.arange(G)[None, :, None, None]
    blk_ids = jnp.arange(n_slc)

    def block(args):
        qb, gb, bi = args
        t = bi * Q_BLOCK + jnp.arange(Q_BLOCK)
        m_c = cmp_end[None, :] <= t[:, None]
        s_c = jnp.einsum('bghqd,bgnd->bghqn', qb, k_cmp).astype(jnp.float32) * scale
        p_c = jax.nn.softmax(jnp.where(m_c, s_c, NEG_INF), -1) * jnp.any(m_c, -1)[:, None]
        o_c = jnp.einsum('bghqn,bgnd->bghqd', p_c, v_cmp)
        imp = jnp.einsum('bghqn,ns->bgqs', p_c, overlap)
        cur = t // SLC_BLOCK
        forced = ((blk_ids[None, :] == 0) | (blk_ids[None, :] == cur[:, None])
                  | (blk_ids[None, :] == cur[:, None] - 1))
        causal_blk = slc_start[None, :] <= t[:, None]
        score = jnp.where(forced, FORCE_SCORE, jnp.where(causal_blk, imp, -1.0))
        top_val, top_idx = lax.top_k(score, n_sel)
        blk_ok = top_val >= 0.0
        kb = ks_blk[b_idx, g_idx, top_idx].reshape(B, G, Q_BLOCK, n_sel * SLC_BLOCK, DK)
        vb = vs_blk[b_idx, g_idx, top_idx].reshape(B, G, Q_BLOCK, n_sel * SLC_BLOCK, DK)
        tok = (top_idx[..., None] * SLC_BLOCK + jnp.arange(SLC_BLOCK)).reshape(B, G, Q_BLOCK, n_sel * SLC_BLOCK)
        m_s = (tok <= t[:, None]) & jnp.repeat(blk_ok, SLC_BLOCK, axis=-1)
        s_s = jnp.einsum('bghqd,bgqkd->bghqk', qb, kb).astype(jnp.float32) * scale
        p_s = jax.nn.softmax(jnp.where(m_s[:, :, None], s_s, NEG_INF), -1)
        o_s = jnp.einsum('bghqk,bgqkd->bghqd', p_s, vb.astype(jnp.float32))
        kw = lax.dynamic_slice_in_dim(kw_pad, bi * Q_BLOCK, Q_BLOCK + WINDOW, axis=2)
        vw = lax.dynamic_slice_in_dim(vw_pad, bi * Q_BLOCK, Q_BLOCK + WINDOW, axis=2)
        j = bi * Q_BLOCK - WINDOW + jnp.arange(Q_BLOCK + WINDOW)
        m_w = (j[None, :] >= 0) & (j[None, :] <= t[:, None]) & (t[:, None] - j[None, :] < WINDOW)
        s_w = jnp.einsum('bghqd,bgkd->bghqk', qb, kw).astype(jnp.float32) * scale
        p_w = jax.nn.softmax(jnp.where(m_w, s_w, NEG_INF), -1)
        o_w = jnp.einsum('bghqk,bgkd->bghqd', p_w, vw.astype(jnp.float32))
        return gb[..., 0:1] * o_c + gb[..., 1:2] * o_s + gb[..., 2:3] * o_w

    out = lax.map(block, (chunks(qh), chunks(gates), jnp.arange(nq)))
    return out.transpose(1, 0, 4, 2, 3, 5).reshape(B, S, NSA_WIDTH).astype(q.dtype)


def conv_ffn(x, w_up, conv_w, conv_b, w_down):
    h = x @ w_up
    h = lax.conv_general_dilated(h, conv_w[:, None, :], window_strides=(1,),
                                 padding=[(CONV_WIDTH - 1, 0)],
                                 dimension_numbers=('NWC', 'WIO', 'NWC'),
                                 feature_group_count=h.shape[-1]) + conv_b
    gate, up = jnp.split(h, 2, axis=-1)
    return (jax.nn.silu(gate) * up) @ w_down


def setup_inputs(seed: int = 0) -> dict:
    key = jax.random.key(seed)
    ks = jax.random.split(key, 19)
    L = DEPTH
    lk = CMP_BLOCK * NSA_HEAD_DIM

    def nrm(k, shape, s):
        return jax.random.normal(k, shape, jnp.float32) * s

    return {
        'x': nrm(ks[0], (BATCH, SEQ, D_MODEL), 1.0),
        'w_in': nrm(ks[1], (L, D_MODEL, N_IN), D_MODEL ** -0.5),
        'w_out': nrm(ks[2], (L, D_MIX, D_MODEL), D_MIX ** -0.5 * DN_BETA),
        'hgrn_lb_logits': nrm(ks[3], (L, HGRN_WIDTH), 0.5),
        'hgrn_norm_w': 1.0 + nrm(ks[4], (L, HGRN_WIDTH), 0.02),
        'cmp_pe_k': nrm(ks[5], (L, CMP_BLOCK, NSA_HEAD_DIM), 0.1),
        'cmp_pe_v': nrm(ks[6], (L, CMP_BLOCK, NSA_HEAD_DIM), 0.1),
        'cmp_w1_k': nrm(ks[7], (L, lk, CMP_HIDDEN), lk ** -0.5),
        'cmp_w2_k': nrm(ks[8], (L, CMP_HIDDEN, NSA_HEAD_DIM), CMP_HIDDEN ** -0.5),
        'cmp_w1_v': nrm(ks[9], (L, lk, CMP_HIDDEN), lk ** -0.5),
        'cmp_w2_v': nrm(ks[10], (L, CMP_HIDDEN, NSA_HEAD_DIM), CMP_HIDDEN ** -0.5),
        'ln1_g': 1.0 + nrm(ks[11], (L, D_MODEL), 0.02),
        'ln1_b': nrm(ks[12], (L, D_MODEL), 0.01),
        'w_up': nrm(ks[13], (L, D_MODEL, 2 * D_FF), D_MODEL ** -0.5),
        'conv_w': nrm(ks[14], (L, CONV_WIDTH, 2 * D_FF), CONV_WIDTH ** -0.5),
        'conv_b': nrm(ks[15], (L, 2 * D_FF), 0.01),
        'w_down': nrm(ks[16], (L, D_FF, D_MODEL), D_FF ** -0.5 * DN_BETA),
        'ln2_g': 1.0 + nrm(ks[17], (L, D_MODEL), 0.02),
        'ln2_b': nrm(ks[18], (L, D_MODEL), 0.01),
    }


def reference(x, w_in, w_out, hgrn_lb_logits, hgrn_norm_w, cmp_pe_k, cmp_pe_v, cmp_w1_k, cmp_w2_k,
              cmp_w1_v, cmp_w2_v, ln1_g, ln1_b, w_up, conv_w, conv_b, w_down, ln2_g, ln2_b):
    S = x.shape[1]
    pos = jnp.arange(S)
    p_lb = jax.nn.softmax(hgrn_lb_logits.astype(jnp.float32), axis=0)
    lower_bounds = jnp.cumsum(p_lb, axis=0) - p_lb[0:1]
    offsets = [int(v) for v in np.cumsum(IN_SPLITS)[:-1]]
    for l in range(DEPTH):
        h = x @ w_in[l]
        hq, hf, hi, hg, nq_, kc, vc, ks_, vs_, kw, vw, gt = jnp.split(h, offsets, axis=-1)
        o_h = hgrn2_mixer(hq, hf, hi, hg, lower_bounds[l], hgrn_norm_w[l])
        o_n = nsa_mixer(nq_, kc, vc, ks_, vs_, kw, vw, gt, pos, cmp_pe_k[l], cmp_pe_v[l],
                        cmp_w1_k[l], cmp_w2_k[l], cmp_w1_v[l], cmp_w2_v[l])
        y = jnp.concatenate([o_h, o_n], axis=-1) @ w_out[l]
        x = layer_norm(DN_ALPHA * x + y, ln1_g[l], ln1_b[l])
        f = conv_ffn(x, w_up[l], conv_w[l], conv_b[l], w_down[l])
        x = layer_norm(DN_ALPHA * x + f, ln2_g[l], ln2_b[l])
    return x
```

```python
import functools

import numpy as np
import jax
import jax.numpy as jnp
from jax import lax
from jax.experimental import pallas as pl
from jax.experimental.pallas import tpu as pltpu

F32 = jnp.float32
BF16 = jnp.bfloat16

D_MODEL = 2048
DEPTH = 4
HGRN_WIDTH = 1024
HGRN_HEADS = 8
HEAD_LANES = 128
NSA_WIDTH = 1024
NSA_HEAD_DIM = 64
NSA_HEADS = 16
NSA_KV_GROUPS = 4
NSA_HPG = 4
NSA_KV_WIDTH = NSA_KV_GROUPS * NSA_HEAD_DIM
CMP_BLOCK = 32
CMP_STRIDE = 16
CMP_HIDDEN = 256
SLC_BLOCK = 64
SLC_TOPN = 16
SLC_LANES = 128
WINDOW = 512
D_FF = 5632
ROPE_THETA = 10000.0
LN_EPS = 1e-5
RMS_EPS = 1e-6
F_MIN = 1e-30
DN_ALPHA = (2 * DEPTH) ** 0.25
NEG_INF = -1e30
FORCE_SCORE = 1e9

HGRN_CHUNK = 64
HGRN_SUB = 16
VMEM_LIMIT = 48 * 1024 * 1024


def _params(sem):
    return pltpu.CompilerParams(dimension_semantics=sem, vmem_limit_bytes=VMEM_LIMIT)


def _dot(a, b):
    return jnp.dot(a, b, preferred_element_type=F32)


def _dot_nt(a, b):
    return lax.dot_general(a, b, (((1,), (1,)), ((), ())), preferred_element_type=F32)


def _dot_tn(a, b):
    return lax.dot_general(a, b, (((0,), (0,)), ((), ())), preferred_element_type=F32)


def _split3(x):
    hi = x.astype(BF16)
    r = x - hi.astype(F32)
    mid = r.astype(BF16)
    lo = (r - mid.astype(F32)).astype(BF16)
    return hi, mid, lo


def _sigmoid_pair(z):
    e = jnp.exp(-jnp.abs(z))
    r = 1.0 / (1.0 + e)
    er = e * r
    pos = z >= 0
    return jnp.where(pos, r, er), jnp.where(pos, er, r)


def _silu(x):
    return x * _sigmoid_pair(x)[0]


def _proj_kernel(x_ref, w_ref, o_ref):
    o_ref[...] = _dot(x_ref[...], w_ref[...]).astype(o_ref.dtype)


def _proj(x, w, out_dtype, tm, tn):
    m, k = x.shape
    n = w.shape[1]
    return pl.pallas_call(
        _proj_kernel,
        grid=(m // tm, n // tn),
        in_specs=[pl.BlockSpec((tm, k), lambda i, j: (i, 0)),
                  pl.BlockSpec((k, tn), lambda i, j: (0, j))],
        out_specs=pl.BlockSpec((tm, tn), lambda i, j: (i, j)),
        out_shape=jax.ShapeDtypeStruct((m, n), out_dtype),
        compiler_params=_params(("parallel", "arbitrary")),
        name="proj",
    )(x, w)


def _proj_rope_kernel(x_ref, w_ref, cos_ref, sin_ref, o_ref, *, tn):
    h = _dot(x_ref[...], w_ref[...])
    cos = cos_ref[0]
    sin = sin_ref[0]
    lane = lax.broadcasted_iota(jnp.int32, cos.shape, 1)
    first_half = (lane % NSA_HEAD_DIM) < (NSA_HEAD_DIM // 2)
    for c in range(tn // 128):
        hc = h[:, c * 128:(c + 1) * 128]
        rot = jnp.where(first_half, pltpu.roll(hc, 96, axis=1), pltpu.roll(hc, 32, axis=1))
        o_ref[:, c * 128:(c + 1) * 128] = (hc * cos + rot * sin).astype(o_ref.dtype)


def _proj_rope(x, w, cos_tabs, sin_tabs, n_q_tiles, tm, tn):
    m, k = x.shape
    n = w.shape[1]
    tab_spec = pl.BlockSpec((1, tm, 128), lambda i, j: (jnp.where(j < n_q_tiles, 0, 1), i, 0))
    return pl.pallas_call(
        functools.partial(_proj_rope_kernel, tn=tn),
        grid=(m // tm, n // tn),
        in_specs=[pl.BlockSpec((tm, k), lambda i, j: (i, 0)),
                  pl.BlockSpec((k, tn), lambda i, j: (0, j)),
                  tab_spec, tab_spec],
        out_specs=pl.BlockSpec((tm, tn), lambda i, j: (i, j)),
        out_shape=jax.ShapeDtypeStruct((m, n), BF16),
        compiler_params=_params(("parallel", "arbitrary")),
        name="proj_rope",
    )(x, w, cos_tabs, sin_tabs)


def _hgrn_kernel(q_ref, z_ref, v_ref, g_ref, lb_ref, nw_ref, o_ref, st_ref, *, tb):
    C, c = HGRN_CHUNK, HGRN_SUB

    @pl.when(pl.program_id(1) == 0)
    def _():
        st_ref[...] = jnp.zeros_like(st_ref)

    lb = lb_ref[...]
    nw = nw_ref[...]
    one_minus_lb = 1.0 - lb
    tri = (lax.broadcasted_iota(jnp.int32, (C, C), 1)
           <= lax.broadcasted_iota(jnp.int32, (C, C), 0)).astype(BF16)
    sub_row = lax.broadcasted_iota(jnp.int32, (c, 1), 0)

    def chunk(ci, carry):
        r0 = pl.multiple_of(ci * C, C)
        q = q_ref[pl.ds(r0, C), :]
        z = z_ref[pl.ds(r0, C), :]
        v = v_ref[pl.ds(r0, C), :]
        g = g_ref[pl.ds(r0, C), :]
        sig, sig_neg = _sigmoid_pair(z)
        f = lb + one_minus_lb * sig
        lf = jnp.log(jnp.maximum(f, F_MIN))
        k = one_minus_lb * sig_neg
        qs = _silu(q)
        lf_hi, lf_mid, lf_lo = _split3(lf)
        b = _dot(tri, lf_hi) + _dot(tri, lf_mid) + _dot(tri, lf_lo)
        st = st_ref[...]
        o_inter = _dot_nt((qs * jnp.exp(b)).astype(BF16), st.astype(BF16))
        v16 = v.astype(BF16)
        outs = []
        for i in range(C // c):
            lo = i * c
            b_i = b[lo:lo + c]
            qs_i = qs[lo:lo + c]
            k_i = k[lo:lo + c]
            v_i = v[lo:lo + c]
            o_i = o_inter[lo:lo + c]
            if i > 0:
                beta = b[lo - 1:lo]
                qt = (qs_i * jnp.exp(b_i - beta)).astype(BF16)
                kt = (k[:lo] * jnp.exp(beta - b[:lo])).astype(BF16)
                att = _dot_nt(qt, kt)
                o_i = o_i + _dot(att.astype(BF16), v16[:lo])
            for s in range(c):
                d = jnp.exp(jnp.minimum(b_i - b_i[s:s + 1], 0.0))
                w = jnp.sum(qs_i * d * k_i[s:s + 1], axis=-1, keepdims=True)
                w = jnp.where(sub_row >= s, w, 0.0)
                o_i = o_i + w * v_i[s:s + 1]
            outs.append(o_i)
        o = jnp.concatenate(outs, axis=0)
        b_last = b[C - 1:C]
        kd = (k * jnp.exp(b_last - b)).astype(BF16)
        st_ref[...] = jnp.exp(b_last) * st + _dot_tn(v16, kd)
        o = o * lax.rsqrt(jnp.mean(o * o, axis=-1, keepdims=True) + RMS_EPS)
        o_ref[pl.ds(r0, C), :] = (o * nw * _silu(g)).astype(o_ref.dtype)
        return carry

    lax.fori_loop(0, tb // C, chunk, 0)


def _hgrn(hh, lb, nw, tb):
    s = hh.shape[0]
    nh = HGRN_HEADS

    def col(off):
        return pl.BlockSpec((tb, HEAD_LANES), lambda h, t: (t, off * nh + h))

    vec = pl.BlockSpec((1, HEAD_LANES), lambda h, t: (0, h))
    return pl.pallas_call(
        functools.partial(_hgrn_kernel, tb=tb),
        grid=(nh, s // tb),
        in_specs=[col(0), col(1), col(2), col(3), vec, vec],
        out_specs=pl.BlockSpec((tb, HEAD_LANES), lambda h, t: (t, h)),
        out_shape=jax.ShapeDtypeStruct((s, HGRN_WIDTH), BF16),
        scratch_shapes=[pltpu.VMEM((HEAD_LANES, HEAD_LANES), F32)],
        compiler_params=_params(("parallel", "arbitrary")),
        name="hgrn2",
    )(hh, hh, hh, hh, lb, nw)


def _compress_kernel(a_ref, pe_ref, w1_ref, w2_ref, o_ref):
    a = a_ref[0, 0]
    w1 = w1_ref[0]
    half = w1.shape[0] // 2
    n_blk = a.shape[0]
    top = _dot(a, w1[:half])
    bot = _dot(a, w1[half:])
    pe_term = _dot(pe_ref[0], w1)[0:1]
    hid = top + pltpu.roll(bot, n_blk - 1, axis=0) + pe_term
    o_ref[0, 0] = _dot(_silu(hid).astype(BF16), w2_ref[0])


def _compress(a, pe8, w1, w2):
    _, g, n_blk, ka = a.shape
    return pl.pallas_call(
        _compress_kernel,
        grid=(2, g),
        in_specs=[pl.BlockSpec((1, 1, n_blk, ka), lambda b, gi: (b, gi, 0, 0)),
                  pl.BlockSpec((1, 8, 2 * ka), lambda b, gi: (b, 0, 0)),
                  pl.BlockSpec((1, 2 * ka, CMP_HIDDEN), lambda b, gi: (b, 0, 0)),
                  pl.BlockSpec((1, CMP_HIDDEN, NSA_HEAD_DIM), lambda b, gi: (b, 0, 0))],
        out_specs=pl.BlockSpec((1, 1, n_blk, NSA_HEAD_DIM), lambda b, gi: (b, gi, 0, 0)),
        out_shape=jax.ShapeDtypeStruct((2, g, n_blk, NSA_HEAD_DIM), F32),
        compiler_params=_params(("parallel", "arbitrary")),
        name="compress",
    )(a, pe8, w1, w2)


def _cmp_select_kernel(q_ref, kc_ref, vc_ref, oc_ref, sel_ref, *, tq):
    t0 = pl.program_id(1) * tq
    q = q_ref[...]
    kc = kc_ref[0, 0].astype(BF16)
    vc = vc_ref[0, 0].astype(BF16)
    n_blk = kc.shape[0]
    t = t0 + lax.broadcasted_iota(jnp.int32, (tq, 1), 0)
    n_idx = lax.broadcasted_iota(jnp.int32, (1, n_blk), 1)
    visible = (n_idx * CMP_STRIDE + (CMP_BLOCK - 1)) <= t
    any_visible = (t >= CMP_BLOCK - 1).astype(F32)
    p_sum = jnp.zeros((tq, n_blk), F32)
    for hp in range(NSA_HPG):
        qh = q[:, hp * NSA_HEAD_DIM:(hp + 1) * NSA_HEAD_DIM]
        s = jnp.where(visible, _dot_nt(qh, kc), NEG_INF)
        e = jnp.exp(s - jnp.max(s, axis=-1, keepdims=True))
        p = e / jnp.sum(e, axis=-1, keepdims=True) * any_visible
        oc_ref[:, hp * NSA_HEAD_DIM:(hp + 1) * NSA_HEAD_DIM] = _dot(p.astype(BF16), vc).astype(oc_ref.dtype)
        p_sum = p_sum + p
    ci = lax.broadcasted_iota(jnp.int32, (n_blk, SLC_LANES), 0) * CMP_STRIDE
    sj = lax.broadcasted_iota(jnp.int32, (n_blk, SLC_LANES), 1) * SLC_BLOCK
    overlap = ((ci < sj + SLC_BLOCK) & (ci + CMP_BLOCK > sj)).astype(BF16)
    p_hi, p_mid, p_lo = _split3(p_sum)
    imp = _dot(p_hi, overlap) + _dot(p_mid, overlap) + _dot(p_lo, overlap)
    blk = lax.broadcasted_iota(jnp.int32, (1, SLC_LANES), 1)
    cur = t // SLC_BLOCK
    forced = (blk == 0) | (blk == cur) | (blk == cur - 1)
    causal = blk * SLC_BLOCK <= t
    work = jnp.where(forced, FORCE_SCORE, jnp.where(causal, imp, -1.0))
    blk_f = blk.astype(F32)
    sel = jnp.zeros((tq, SLC_LANES), F32)
    for _ in range(SLC_TOPN):
        mx = jnp.max(work, axis=-1, keepdims=True)
        first = jnp.min(jnp.where(work == mx, blk_f, float(SLC_LANES)), axis=-1, keepdims=True)
        pick = blk_f == first
        sel = jnp.where(pick & (mx >= 0.0), 1.0, sel)
        work = jnp.where(pick, -3e38, work)
    sel_ref[0] = sel.astype(sel_ref.dtype)


def _cmp_select(q, k_cmp, v_cmp, tq):
    s = q.shape[0]
    g = NSA_KV_GROUPS
    n_blk = k_cmp.shape[2]
    gw = NSA_HPG * NSA_HEAD_DIM
    return pl.pallas_call(
        functools.partial(_cmp_select_kernel, tq=tq),
        grid=(g, s // tq),
        in_specs=[pl.BlockSpec((tq, gw), lambda gi, qi: (qi, gi)),
                  pl.BlockSpec((1, 1, n_blk, NSA_HEAD_DIM), lambda gi, qi: (0, gi, 0, 0)),
                  pl.BlockSpec((1, 1, n_blk, NSA_HEAD_DIM), lambda gi, qi: (1, gi, 0, 0))],
        out_specs=[pl.BlockSpec((tq, gw), lambda gi, qi: (qi, gi)),
                   pl.BlockSpec((1, tq, SLC_LANES), lambda gi, qi: (gi, qi, 0))],
        out_shape=[jax.ShapeDtypeStruct((s, NSA_WIDTH), F32),
                   jax.ShapeDtypeStruct((g, s, SLC_LANES), BF16)],
        compiler_params=_params(("parallel", "arbitrary")),
        name="cmp_select",
    )(q, k_cmp, k_cmp)


def _slc_attn_kernel(q_ref, kv_ref, sel_ref, o_ref, m_ref, l_ref, acc_ref, *, tq, tk):
    qi = pl.program_id(1)
    ki = pl.program_id(2)
    dk = NSA_HEAD_DIM

    @pl.when(ki == 0)
    def _():
        m_ref[...] = jnp.full_like(m_ref, NEG_INF)
        l_ref[...] = jnp.zeros_like(l_ref)
        acc_ref[...] = jnp.zeros_like(acc_ref)

    @pl.when(ki * tk <= qi * tq + (tq - 1))
    def _():
        kv = kv_ref[...]
        k = kv[:, :dk]
        v = kv[:, dk:]
        blk_of_key = ki * (tk // SLC_BLOCK) + lax.broadcasted_iota(jnp.int32, (SLC_LANES, tk), 1) // SLC_BLOCK
        expand = (lax.broadcasted_iota(jnp.int32, (SLC_LANES, tk), 0) == blk_of_key).astype(BF16)
        chosen = _dot(sel_ref[0], expand)
        t = qi * tq + lax.broadcasted_iota(jnp.int32, (tq, 1), 0)
        j = ki * tk + lax.broadcasted_iota(jnp.int32, (1, tk), 1)
        keep = (chosen > 0.5) & (j <= t)
        q = q_ref[...]
        for hp in range(NSA_HPG):
            s = jnp.where(keep, _dot_nt(q[:, hp * dk:(hp + 1) * dk], k), NEG_INF)
            m_old = m_ref[hp]
            m_new = jnp.maximum(m_old, jnp.max(s, axis=-1, keepdims=True))
            alpha = jnp.exp(m_old - m_new)
            p = jnp.exp(s - m_new)
            l_ref[hp] = alpha * l_ref[hp] + jnp.sum(p, axis=-1, keepdims=True)
            acc_ref[hp] = alpha * acc_ref[hp] + _dot(p.astype(BF16), v)
            m_ref[hp] = m_new

    @pl.when(ki == pl.num_programs(2) - 1)
    def _():
        for hp in range(NSA_HPG):
            o_ref[:, hp * dk:(hp + 1) * dk] = (acc_ref[hp] / l_ref[hp]).astype(o_ref.dtype)


def _slc_attn(q, kv, sel, kv_col0, tq, tk):
    s = q.shape[0]
    g = NSA_KV_GROUPS
    gw = NSA_HPG * NSA_HEAD_DIM

    def kv_map(gi, qi, ki):
        return (jnp.minimum(ki, (qi * tq + tq - 1) // tk), kv_col0 + gi)

    return pl.pallas_call(
        functools.partial(_slc_attn_kernel, tq=tq, tk=tk),
        grid=(g, s // tq, s // tk),
        in_specs=[pl.BlockSpec((tq, gw), lambda gi, qi, ki: (qi, gi)),
                  pl.BlockSpec((tk, 2 * NSA_HEAD_DIM), kv_map),
                  pl.BlockSpec((1, tq, SLC_LANES), lambda gi, qi, ki: (gi, qi, 0))],
        out_specs=pl.BlockSpec((tq, gw), lambda gi, qi, ki: (qi, gi)),
        out_shape=jax.ShapeDtypeStruct((s, NSA_WIDTH), F32),
        scratch_shapes=[pltpu.VMEM((NSA_HPG, tq, 1), F32),
                        pltpu.VMEM((NSA_HPG, tq, 1), F32),
                        pltpu.VMEM((NSA_HPG, tq, NSA_HEAD_DIM), F32)],
        compiler_params=_params(("parallel", "parallel", "arbitrary")),
        name="slc_attn",
    )(q, kv, sel)


def _win_attn_kernel(q_ref, kv_ref, o_ref, *, tq):
    dk = NSA_HEAD_DIM
    span = WINDOW + tq
    t0 = pl.program_id(1) * tq
    start = pl.multiple_of(jnp.maximum(t0 - WINDOW, 0), tq)
    kv = kv_ref[pl.ds(start, span), :]
    k = kv[:, :dk]
    v = kv[:, dk:]
    t = t0 + lax.broadcasted_iota(jnp.int32, (tq, 1), 0)
    j = start + lax.broadcasted_iota(jnp.int32, (1, span), 1)
    keep = (j <= t) & (t - j < WINDOW)
    q = q_ref[...]
    for hp in range(NSA_HPG):
        s = jnp.where(keep, _dot_nt(q[:, hp * dk:(hp + 1) * dk], k), NEG_INF)
        e = jnp.exp(s - jnp.max(s, axis=-1, keepdims=True))
        o = _dot(e.astype(BF16), v) / jnp.sum(e, axis=-1, keepdims=True)
        o_ref[:, hp * dk:(hp + 1) * dk] = o.astype(o_ref.dtype)


def _win_attn(q, kv, kv_col0, tq):
    s = q.shape[0]
    g = NSA_KV_GROUPS
    gw = NSA_HPG * NSA_HEAD_DIM
    return pl.pallas_call(
        functools.partial(_win_attn_kernel, tq=tq),
        grid=(g, s // tq),
        in_specs=[pl.BlockSpec((tq, gw), lambda gi, qi: (qi, gi)),
                  pl.BlockSpec((s, 2 * NSA_HEAD_DIM), lambda gi, qi: (0, kv_col0 + gi))],
        out_specs=pl.BlockSpec((tq, gw), lambda gi, qi: (qi, gi)),
        out_shape=jax.ShapeDtypeStruct((s, NSA_WIDTH), F32),
        compiler_params=_params(("parallel", "arbitrary")),
        name="win_attn",
    )(q, kv)


def _combine_kernel(gp_ref, oc_ref, os_ref, ow_ref, o_ref):
    sig = _sigmoid_pair(gp_ref[...])[0]
    pieces = _split3(sig)
    lane = lax.broadcasted_iota(jnp.int32, (SLC_LANES, NSA_WIDTH), 0)
    head = lax.broadcasted_iota(jnp.int32, (SLC_LANES, NSA_WIDTH), 1) // NSA_HEAD_DIM
    out = None
    for br, ref in enumerate((oc_ref, os_ref, ow_ref)):
        expand = (lane == br * NSA_HEADS + head).astype(BF16)
        gate = _dot(pieces[0], expand) + _dot(pieces[1], expand) + _dot(pieces[2], expand)
        term = gate * ref[...]
        out = term if out is None else out + term
    o_ref[...] = out.astype(o_ref.dtype)


def _combine(gate_pre, o_c, o_s, o_w, tm):
    s = o_c.shape[0]
    wide = pl.BlockSpec((tm, NSA_WIDTH), lambda i: (i, 0))
    return pl.pallas_call(
        _combine_kernel,
        grid=(s // tm,),
        in_specs=[pl.BlockSpec((tm, 128), lambda i: (i, 0)), wide, wide, wide],
        out_specs=wide,
        out_shape=jax.ShapeDtypeStruct((s, NSA_WIDTH), BF16),
        compiler_params=_params(("parallel",)),
        name="combine",
    )(gate_pre, o_c, o_s, o_w)


def _res_ln_epilogue(acc, x_ref, g_ref, b_ref, xo_ref, xb_ref):
    v = DN_ALPHA * x_ref[...] + acc
    mu = jnp.mean(v, axis=-1, keepdims=True)
    d = v - mu
    var = jnp.mean(d * d, axis=-1, keepdims=True)
    y = d * lax.rsqrt(var + LN_EPS) * g_ref[...] + b_ref[...]
    xo_ref[...] = y
    xb_ref[...] = y.astype(BF16)


def _mix_out_kernel(a1_ref, a2_ref, w_ref, x_ref, g_ref, b_ref, xo_ref, xb_ref, acc_ref):
    k = pl.program_id(1)

    @pl.when(k == 0)
    def _():
        acc_ref[...] = _dot(a1_ref[...], w_ref[...])

    @pl.when(k == 1)
    def _():
        _res_ln_epilogue(acc_ref[...] + _dot(a2_ref[...], w_ref[...]), x_ref, g_ref, b_ref, xo_ref, xb_ref)


def _mix_out(a1, a2, w, x, gamma, beta, tm):
    s, k1 = a1.shape
    n = w.shape[1]
    row = pl.BlockSpec((tm, n), lambda i, k: (i, 0))
    vec = pl.BlockSpec((1, n), lambda i, k: (0, 0))
    return pl.pallas_call(
        _mix_out_kernel,
        grid=(s // tm, 2),
        in_specs=[pl.BlockSpec((tm, k1), lambda i, k: (i, 0)),
                  pl.BlockSpec((tm, k1), lambda i, k: (i, 0)),
                  pl.BlockSpec((k1, n), lambda i, k: (k, 0)),
                  row, vec, vec],
        out_specs=[row, row],
        out_shape=[jax.ShapeDtypeStruct((s, n), F32), jax.ShapeDtypeStruct((s, n), BF16)],
        scratch_shapes=[pltpu.VMEM((tm, n), F32)],
        compiler_params=_params(("parallel", "arbitrary")),
        name="mix_out",
    )(a1, a2, w, x, gamma, beta)


def _ffn_down_kernel(a_ref, w_ref, x_ref, g_ref, b_ref, xo_ref, xb_ref, acc_ref):
    k = pl.program_id(1)

    @pl.when(k == 0)
    def _():
        acc_ref[...] = jnp.zeros_like(acc_ref)

    acc_ref[...] += _dot(a_ref[...], w_ref[...])

    @pl.when(k == pl.num_programs(1) - 1)
    def _():
        _res_ln_epilogue(acc_ref[...], x_ref, g_ref, b_ref, xo_ref, xb_ref)


def _ffn_down(a, w, x, gamma, beta, tm, tk):
    s, kk = a.shape
    n = w.shape[1]
    row = pl.BlockSpec((tm, n), lambda i, k: (i, 0))
    vec = pl.BlockSpec((1, n), lambda i, k: (0, 0))
    return pl.pallas_call(
        _ffn_down_kernel,
        grid=(s // tm, kk // tk),
        in_specs=[pl.BlockSpec((tm, tk), lambda i, k: (i, k)),
                  pl.BlockSpec((tk, n), lambda i, k: (k, 0)),
                  row, vec, vec],
        out_specs=[row, row],
        out_shape=[jax.ShapeDtypeStruct((s, n), F32), jax.ShapeDtypeStruct((s, n), BF16)],
        scratch_shapes=[pltpu.VMEM((tm, n), F32)],
        compiler_params=_params(("parallel", "arbitrary")),
        name="ffn_down",
    )(a, w, x, gamma, beta)


HALO = 16


def _ffn_up_kernel(a_ref, ap_ref, wg_ref, wu_ref, cwg_ref, cwu_ref, cbg_ref, cbu_ref, o_ref):
    ap = ap_ref[...]
    ap = jnp.where(pl.program_id(0) > 0, ap, jnp.zeros_like(ap))
    a = jnp.concatenate([ap, a_ref[...]], axis=0)

    def branch(w_ref, cw_ref, cb_ref):
        h = _dot(a, w_ref[...])
        cw = cw_ref[...]
        return (cw[2:3] * h[HALO:] + cw[1:2] * h[HALO - 1:-1] + cw[0:1] * h[HALO - 2:-2]) + cb_ref[...]

    gate = branch(wg_ref, cwg_ref, cbg_ref)
    up = branch(wu_ref, cwu_ref, cbu_ref)
    o_ref[...] = (_silu(gate) * up).astype(o_ref.dtype)


def _ffn_up(a, w, conv_w, conv_b, tm, tn):
    s, d = a.shape
    nj = D_FF // tn
    return pl.pallas_call(
        _ffn_up_kernel,
        grid=(s // tm, nj),
        in_specs=[pl.BlockSpec((tm, d), lambda i, j: (i, 0)),
                  pl.BlockSpec((HALO, d), lambda i, j: (jnp.maximum(i * (tm // HALO) - 1, 0), 0)),
                  pl.BlockSpec((d, tn), lambda i, j: (0, j)),
                  pl.BlockSpec((d, tn), lambda i, j: (0, j + nj)),
                  pl.BlockSpec((3, tn), lambda i, j: (0, j)),
                  pl.BlockSpec((3, tn), lambda i, j: (0, j + nj)),
                  pl.BlockSpec((1, tn), lambda i, j: (0, j)),
                  pl.BlockSpec((1, tn), lambda i, j: (0, j + nj))],
        out_specs=pl.BlockSpec((tm, tn), lambda i, j: (i, j)),
        out_shape=jax.ShapeDtypeStruct((s, D_FF), BF16),
        compiler_params=_params(("parallel", "arbitrary")),
        name="ffn_up",
    )(a, a, w, w, conv_w, conv_w, conv_b, conv_b)


def _rope_tables(s):
    half = NSA_HEAD_DIM // 2
    inv = ROPE_THETA ** (-jnp.arange(half, dtype=F32) / half)
    ang = jnp.arange(s).astype(F32)[:, None] * inv[None, :]
    cos, sin = jnp.cos(ang), jnp.sin(ang)
    cos64 = jnp.concatenate([cos, cos], -1)
    sin64 = jnp.concatenate([-sin, sin], -1)
    scale = NSA_HEAD_DIM ** -0.5
    ones, zeros = jnp.ones_like(cos64), jnp.zeros_like(sin64)
    cos_tabs = jnp.stack([jnp.concatenate([cos64, cos64], -1) * scale, jnp.concatenate([cos64, ones], -1)])
    sin_tabs = jnp.stack([jnp.concatenate([sin64, sin64], -1) * scale, jnp.concatenate([sin64, zeros], -1)])
    return cos_tabs, sin_tabs


def _nsa_weight_layout(w_nsa):
    q = w_nsa[:, :NSA_WIDTH]
    kv = w_nsa[:, NSA_WIDTH:NSA_WIDTH + 6 * NSA_KV_WIDTH]
    gt = w_nsa[:, NSA_WIDTH + 6 * NSA_KV_WIDTH:]
    d = w_nsa.shape[0]
    kv = kv.reshape(d, 3, 2, NSA_KV_GROUPS, NSA_HEAD_DIM).transpose(0, 1, 3, 2, 4).reshape(d, 6 * NSA_KV_WIDTH)
    gt = gt.reshape(d, NSA_HEADS, 3).transpose(0, 2, 1).reshape(d, 3 * NSA_HEADS)
    gt = jnp.pad(gt, ((0, 0), (0, 128 - 3 * NSA_HEADS)))
    return jnp.concatenate([q, kv], axis=1), gt


def kernel(x, w_in, w_out, hgrn_lb_logits, hgrn_norm_w, cmp_pe_k, cmp_pe_v, cmp_w1_k, cmp_w2_k,
           cmp_w1_v, cmp_w2_v, ln1_g, ln1_b, w_up, conv_w, conv_b, w_down, ln2_g, ln2_b):
    s = x.shape[1]
    n_blk = s // CMP_STRIDE
    p_lb = jax.nn.softmax(hgrn_lb_logits.astype(F32), axis=0)
    lower_bounds = jnp.cumsum(p_lb, axis=0) - p_lb[0:1]
    cos_tabs, sin_tabs = _rope_tables(s)
    hw = 4 * HGRN_WIDTH
    xf = x[0]
    xb = xf.astype(BF16)
    for l in range(DEPTH):
        w_h = w_in[l][:, :hw].astype(BF16)
        w_qkv, w_gt = _nsa_weight_layout(w_in[l][:, hw:])
        hh = _proj(xb, w_h, F32, 1024, 512)
        qkv = _proj_rope(xb, w_qkv.astype(BF16), cos_tabs, sin_tabs, NSA_WIDTH // 512, 1024, 512)
        gate_pre = _proj(xb, w_gt.astype(BF16), F32, 1024, 128)
        o_h = _hgrn(hh, lower_bounds[l][None], hgrn_norm_w[l][None], 512)

        q = qkv[:, :NSA_WIDTH]
        kv = qkv[:, NSA_WIDTH:]
        kvc = kv[:, :2 * NSA_KV_WIDTH].reshape(n_blk, CMP_STRIDE, NSA_KV_GROUPS, 2, NSA_HEAD_DIM)
        a_cmp = kvc.transpose(3, 2, 0, 1, 4).reshape(2, NSA_KV_GROUPS, n_blk, CMP_STRIDE * NSA_HEAD_DIM)
        pe = jnp.stack([cmp_pe_k[l], cmp_pe_v[l]]).reshape(2, 1, CMP_BLOCK * NSA_HEAD_DIM)
        pe8 = jnp.pad(pe, ((0, 0), (0, 7), (0, 0))).astype(BF16)
        w1 = jnp.stack([cmp_w1_k[l], cmp_w1_v[l]]).astype(BF16)
        w2 = jnp.stack([cmp_w2_k[l], cmp_w2_v[l]]).astype(BF16)
        kv_cmp = _compress(a_cmp, pe8, w1, w2)
        o_c, sel = _cmp_select(q, kv_cmp, kv_cmp, 128)
        o_s = _slc_attn(q, kv, sel, NSA_KV_GROUPS, 256, 512)
        o_w = _win_attn(q, kv, 2 * NSA_KV_GROUPS, 128)
        o_n = _combine(gate_pre, o_c, o_s, o_w, 512)

        xf, xb = _mix_out(o_h, o_n, w_out[l].astype(BF16), xf, ln1_g[l][None], ln1_b[l][None], 512)
        u = _ffn_up(xb, w_up[l].astype(BF16), conv_w[l], conv_b[l][None], 1024, 512)
        xf, xb = _ffn_down(u, w_down[l].astype(BF16), xf, ln2_g[l][None], ln2_b[l][None], 512, 512)
    return xf[None]
```

```python
import functools

import numpy as np
import jax
import jax.numpy as jnp
from jax import lax
from jax.experimental import pallas as pl
from jax.experimental.pallas import tpu as pltpu

F32 = jnp.float32
BF16 = jnp.bfloat16

D_MODEL = 2048
DEPTH = 4
HGRN_WIDTH = 1024
HGRN_HEADS = 8
HEAD_LANES = 128
NSA_WIDTH = 1024
NSA_HEAD_DIM = 64
NSA_HEADS = 16
NSA_KV_GROUPS = 4
NSA_HPG = 4
NSA_KV_WIDTH = NSA_KV_GROUPS * NSA_HEAD_DIM
CMP_BLOCK = 32
CMP_STRIDE = 16
CMP_HIDDEN = 256
SLC_BLOCK = 64
SLC_TOPN = 16
SLC_LANES = 128
WINDOW = 512
D_FF = 5632
ROPE_THETA = 10000.0
LN_EPS = 1e-5
RMS_EPS = 1e-6
F_MIN = 1e-30
DN_ALPHA = (2 * DEPTH) ** 0.25
NEG_INF = -1e30
FORCE_SCORE = 1e9

GATE_ROWS = 16
LOG2E = 1.4426950408889634

NSA_TQ = 512
HGRN_CHUNK = 64
HGRN_SUB = 16
VMEM_LIMIT = 48 * 1024 * 1024


def _params(sem):
    return pltpu.CompilerParams(dimension_semantics=sem, vmem_limit_bytes=VMEM_LIMIT)


def _dot(a, b):
    return jnp.dot(a, b, preferred_element_type=F32)


def _dot_nt(a, b):
    return lax.dot_general(a, b, (((1,), (1,)), ((), ())), preferred_element_type=F32)


def _dot_tn(a, b):
    return lax.dot_general(a, b, (((0,), (0,)), ((), ())), preferred_element_type=F32)


def _split3(x):
    hi = x.astype(BF16)
    r = x - hi.astype(F32)
    mid = r.astype(BF16)
    lo = (r - mid.astype(F32)).astype(BF16)
    return hi, mid, lo


def _sigmoid_pair(z):
    e = jnp.exp(-jnp.abs(z))
    r = 1.0 / (1.0 + e)
    er = e * r
    pos = z >= 0
    return jnp.where(pos, r, er), jnp.where(pos, er, r)


def _silu(x):
    return x * _sigmoid_pair(x)[0]


def _proj_kernel(x_ref, w_ref, o_ref):
    o_ref[...] = _dot(x_ref[...], w_ref[...]).astype(o_ref.dtype)


def _proj(x, w, out_dtype, tm, tn):
    m, k = x.shape
    n = w.shape[1]
    return pl.pallas_call(
        _proj_kernel,
        grid=(m // tm, n // tn),
        in_specs=[pl.BlockSpec((tm, k), lambda i, j: (i, 0)),
                  pl.BlockSpec((k, tn), lambda i, j: (0, j))],
        out_specs=pl.BlockSpec((tm, tn), lambda i, j: (i, j)),
        out_shape=jax.ShapeDtypeStruct((m, n), out_dtype),
        compiler_params=_params(("parallel", "arbitrary")),
        name="proj",
    )(x, w)


def _proj_rope_kernel(x_ref, w_ref, cos_ref, sin_ref, o_ref, *, tn):
    h = _dot(x_ref[...], w_ref[...])
    cos = cos_ref[0]
    sin = sin_ref[0]
    lane = lax.broadcasted_iota(jnp.int32, cos.shape, 1)
    first_half = (lane % NSA_HEAD_DIM) < (NSA_HEAD_DIM // 2)
    for c in range(tn // 128):
        hc = h[:, c * 128:(c + 1) * 128]
        rot = jnp.where(first_half, pltpu.roll(hc, 96, axis=1), pltpu.roll(hc, 32, axis=1))
        o_ref[:, c * 128:(c + 1) * 128] = (hc * cos + rot * sin).astype(o_ref.dtype)


def _proj_rope(x, w, cos_tabs, sin_tabs, n_first, tm, tn):
    m, k = x.shape
    n = w.shape[1]
    tab_spec = pl.BlockSpec((1, tm, 128), lambda i, j: (jnp.where(j < n_first, 0, 1), i, 0))
    return pl.pallas_call(
        functools.partial(_proj_rope_kernel, tn=tn),
        grid=(m // tm, n // tn),
        in_specs=[pl.BlockSpec((tm, k), lambda i, j: (i, 0)),
                  pl.BlockSpec((k, tn), lambda i, j: (0, j)),
                  tab_spec, tab_spec],
        out_specs=pl.BlockSpec((tm, tn), lambda i, j: (i, j)),
        out_shape=jax.ShapeDtypeStruct((m, n), BF16),
        compiler_params=_params(("parallel", "arbitrary")),
        name="proj_rope",
    )(x, w, cos_tabs, sin_tabs)


def _hgrn_chunk(q, z, v, g, lb, nw, st, tri, sub_row):
    C, c = HGRN_CHUNK, HGRN_SUB
    one_minus_lb = 1.0 - lb
    sig, sig_neg = _sigmoid_pair(z)
    f = lb + one_minus_lb * sig
    lf = jnp.log(jnp.maximum(f, F_MIN))
    k = one_minus_lb * sig_neg
    qs = _silu(q)
    lf_hi, lf_mid, lf_lo = _split3(lf)
    b = _dot(tri, lf_hi) + _dot(tri, lf_mid) + _dot(tri, lf_lo)
    o_inter = _dot_nt((qs * jnp.exp(b)).astype(BF16), st.astype(BF16))
    v16 = v.astype(BF16)
    outs = []
    for i in range(C // c):
        lo = i * c
        b_i = b[lo:lo + c]
        qs_i = qs[lo:lo + c]
        k_i = k[lo:lo + c]
        v_i = v[lo:lo + c]
        o_i = o_inter[lo:lo + c]
        if i > 0:
            beta = b[lo - 1:lo]
            qt = (qs_i * jnp.exp(b_i - beta)).astype(BF16)
            kt = (k[:lo] * jnp.exp(beta - b[:lo])).astype(BF16)
            att = _dot_nt(qt, kt)
            o_i = o_i + _dot(att.astype(BF16), v16[:lo])
        for s in range(c):
            d = jnp.exp(jnp.minimum(b_i - b_i[s:s + 1], 0.0))
            w = jnp.sum(qs_i * d * k_i[s:s + 1], axis=-1, keepdims=True)
            w = jnp.where(sub_row >= s, w, 0.0)
            o_i = o_i + w * v_i[s:s + 1]
        outs.append(o_i)
    o = jnp.concatenate(outs, axis=0)
    b_last = b[C - 1:C]
    kd = (k * jnp.exp(b_last - b)).astype(BF16)
    st_new = jnp.exp(b_last) * st + _dot_tn(v16, kd)
    o = o * lax.rsqrt(jnp.mean(o * o, axis=-1, keepdims=True) + RMS_EPS)
    return o * nw * _silu(g), st_new


def _hgrn_kernel(q_ref, z_ref, v_ref, g_ref, lb_ref, nw_ref, o_ref, st_ref, *, tb, hb):
    C, c = HGRN_CHUNK, HGRN_SUB

    @pl.when(pl.program_id(1) == 0)
    def _():
        st_ref[...] = jnp.zeros_like(st_ref)

    tri = (lax.broadcasted_iota(jnp.int32, (C, C), 1)
           <= lax.broadcasted_iota(jnp.int32, (C, C), 0)).astype(BF16)
    sub_row = lax.broadcasted_iota(jnp.int32, (c, 1), 0)

    def chunk(ci, carry):
        r0 = pl.multiple_of(ci * C, C)
        for h in range(hb):
            lanes = slice(h * HEAD_LANES, (h + 1) * HEAD_LANES)
            out, st_new = _hgrn_chunk(q_ref[pl.ds(r0, C), lanes], z_ref[pl.ds(r0, C), lanes],
                                      v_ref[pl.ds(r0, C), lanes], g_ref[pl.ds(r0, C), lanes],
                                      lb_ref[:, lanes], nw_ref[:, lanes], st_ref[h], tri, sub_row)
            st_ref[h] = st_new
            o_ref[pl.ds(r0, C), lanes] = out.astype(o_ref.dtype)
        return carry

    lax.fori_loop(0, tb // C, chunk, 0)


def _hgrn(hh, lb, nw, tb, hb):
    s = hh.shape[0]
    nhb = HGRN_HEADS // hb
    wide = hb * HEAD_LANES

    def col(off):
        return pl.BlockSpec((tb, wide), lambda h, t: (t, off * nhb + h))

    vec = pl.BlockSpec((1, wide), lambda h, t: (0, h))
    return pl.pallas_call(
        functools.partial(_hgrn_kernel, tb=tb, hb=hb),
        grid=(nhb, s // tb),
        in_specs=[col(0), col(1), col(2), col(3), vec, vec],
        out_specs=pl.BlockSpec((tb, wide), lambda h, t: (t, h)),
        out_shape=jax.ShapeDtypeStruct((s, HGRN_WIDTH), BF16),
        scratch_shapes=[pltpu.VMEM((hb, HEAD_LANES, HEAD_LANES), F32)],
        compiler_params=_params(("parallel", "arbitrary")),
        name="hgrn2",
    )(hh, hh, hh, hh, lb, nw)


def _projT_rope_kernel(w_ref, xT_ref, cos_ref, sin_ref, o_ref, *, tq):
    h = _dot(w_ref[...], xT_ref[...])
    cos = cos_ref[...]
    sin = sin_ref[...]
    half = NSA_HEAD_DIM // 2
    for hp in range(NSA_HPG):
        r = hp * NSA_HEAD_DIM
        x1 = h[r:r + half]
        x2 = h[r + half:r + 2 * half]
        o_ref[0, :half, hp * tq:(hp + 1) * tq] = (x1 * cos - x2 * sin).astype(o_ref.dtype)
        o_ref[0, half:, hp * tq:(hp + 1) * tq] = (x2 * cos + x1 * sin).astype(o_ref.dtype)


def _projT_rope(wT, xT, cosT, sinT, tq):
    n, d = wT.shape
    s = xT.shape[1]
    half = NSA_HEAD_DIM // 2
    gw = NSA_HPG * NSA_HEAD_DIM
    return pl.pallas_call(
        functools.partial(_projT_rope_kernel, tq=tq),
        grid=(s // tq, n // gw),
        in_specs=[pl.BlockSpec((gw, d), lambda i, j: (j, 0)),
                  pl.BlockSpec((d, tq), lambda i, j: (0, i)),
                  pl.BlockSpec((half, tq), lambda i, j: (0, i)),
                  pl.BlockSpec((half, tq), lambda i, j: (0, i))],
        out_specs=pl.BlockSpec((1, NSA_HEAD_DIM, NSA_HPG * tq), lambda i, j: (j, 0, i)),
        out_shape=jax.ShapeDtypeStruct((n // gw, NSA_HEAD_DIM, NSA_HPG * s), BF16),
        compiler_params=_params(("parallel", "arbitrary")),
        name="projT_rope",
    )(wT, xT, cosT, sinT)


def _projT_vg_kernel(w_ref, xT_ref, v_ref, g_ref):
    h = _dot(w_ref[...], xT_ref[...])
    nv = v_ref.shape[0]
    v_ref[...] = h[:nv].astype(v_ref.dtype)
    g_ref[...] = h[nv:]


def _projT_vg(wT, xT, nv, ts):
    n, d = wT.shape
    s = xT.shape[1]
    return pl.pallas_call(
        _projT_vg_kernel,
        grid=(s // ts,),
        in_specs=[pl.BlockSpec((n, d), lambda i: (0, 0)),
                  pl.BlockSpec((d, ts), lambda i: (0, i))],
        out_specs=[pl.BlockSpec((nv, ts), lambda i: (0, i)),
                   pl.BlockSpec((n - nv, ts), lambda i: (0, i))],
        out_shape=[jax.ShapeDtypeStruct((nv, s), BF16), jax.ShapeDtypeStruct((n - nv, s), F32)],
        compiler_params=_params(("parallel",)),
        name="projT_vg",
    )(wT, xT)


def _compress_kernel(a_ref, pe_ref, w1_ref, w2_ref, o_ref):
    a = a_ref[0, 0]
    w1 = w1_ref[0]
    half = w1.shape[0] // 2
    n_blk = a.shape[0]
    top = _dot(a, w1[:half])
    bot = _dot(a, w1[half:])
    pe_term = _dot(pe_ref[0], w1)[0:1]
    hid = top + pltpu.roll(bot, n_blk - 1, axis=0) + pe_term
    o_ref[0, 0] = _dot(_silu(hid).astype(BF16), w2_ref[0]).astype(o_ref.dtype)


def _compress(a, pe8, w1, w2):
    _, g, n_blk, ka = a.shape
    return pl.pallas_call(
        _compress_kernel,
        grid=(2, g),
        in_specs=[pl.BlockSpec((1, 1, n_blk, ka), lambda b, gi: (b, gi, 0, 0)),
                  pl.BlockSpec((1, 8, 2 * ka), lambda b, gi: (b, 0, 0)),
                  pl.BlockSpec((1, 2 * ka, CMP_HIDDEN), lambda b, gi: (b, 0, 0)),
                  pl.BlockSpec((1, CMP_HIDDEN, NSA_HEAD_DIM), lambda b, gi: (b, 0, 0))],
        out_specs=pl.BlockSpec((1, 1, n_blk, NSA_HEAD_DIM), lambda b, gi: (b, gi, 0, 0)),
        out_shape=jax.ShapeDtypeStruct((2, g, n_blk, NSA_HEAD_DIM), BF16),
        compiler_params=_params(("parallel", "arbitrary")),
        name="compress",
    )(a, pe8, w1, w2)


def _per_head(x, tq):
    return [x[:, hp * tq:(hp + 1) * tq] for hp in range(NSA_HPG)]


def _cmp_select_kernel(q_ref, kc_ref, vcT_ref, oc_ref, selT_ref, *, tq):
    kc = kc_ref[0]
    vcT = vcT_ref[0]
    n_blk = kc.shape[0]
    t = pl.program_id(1) * tq + lax.broadcasted_iota(jnp.int32, (1, tq), 1)
    n_idx = lax.broadcasted_iota(jnp.int32, (n_blk, 1), 0)
    bias = jnp.where((n_idx * CMP_STRIDE + (CMP_BLOCK - 1)) <= t, 0.0, NEG_INF)
    any_visible = (t >= CMP_BLOCK - 1).astype(F32)
    s = _dot(kc, q_ref[0])
    s = jnp.concatenate([x + bias for x in _per_head(s, tq)], axis=1)
    e = jnp.exp2(s - jnp.max(s, axis=0, keepdims=True))
    scale = jnp.concatenate([any_visible] * NSA_HPG, axis=1) / jnp.sum(e, axis=0, keepdims=True)
    p = e * scale
    oc_ref[0] = _dot(vcT, p.astype(BF16))
    p_sum = functools.reduce(lambda a, b: a + b, _per_head(p, tq))
    sj = lax.broadcasted_iota(jnp.int32, (SLC_LANES, n_blk), 0) * SLC_BLOCK
    ci = lax.broadcasted_iota(jnp.int32, (SLC_LANES, n_blk), 1) * CMP_STRIDE
    overlap = ((ci < sj + SLC_BLOCK) & (ci + CMP_BLOCK > sj)).astype(BF16)
    p_hi, p_mid, p_lo = _split3(p_sum)
    imp = _dot(overlap, p_hi) + _dot(overlap, p_mid) + _dot(overlap, p_lo)
    blk = lax.broadcasted_iota(jnp.int32, (SLC_LANES, 1), 0)
    cur = t // SLC_BLOCK
    forced = (blk == 0) | (blk == cur) | (blk == cur - 1)
    causal = blk * SLC_BLOCK <= t
    work = jnp.where(forced, FORCE_SCORE, jnp.where(causal, imp, -1.0))
    blk_f = blk.astype(F32)
    sel = jnp.zeros((SLC_LANES, tq), F32)
    for _ in range(SLC_TOPN):
        mx = jnp.max(work, axis=0, keepdims=True)
        first = jnp.min(jnp.where(work == mx, blk_f, float(SLC_LANES)), axis=0, keepdims=True)
        pick = blk_f == first
        sel = jnp.where(pick & (mx >= 0.0), 1.0, sel)
        work = jnp.where(pick, -3e38, work)
    selT_ref[0] = sel.astype(selT_ref.dtype)


def _cmp_select(qL, k_cmp, v_cmpT, tq):
    g = NSA_KV_GROUPS
    s = qL.shape[2] // NSA_HPG
    n_blk = k_cmp.shape[1]
    q_spec = pl.BlockSpec((1, NSA_HEAD_DIM, NSA_HPG * tq), lambda gi, qi: (gi, 0, qi))
    return pl.pallas_call(
        functools.partial(_cmp_select_kernel, tq=tq),
        grid=(g, s // tq),
        in_specs=[q_spec,
                  pl.BlockSpec((1, n_blk, NSA_HEAD_DIM), lambda gi, qi: (gi, 0, 0)),
                  pl.BlockSpec((1, NSA_HEAD_DIM, n_blk), lambda gi, qi: (gi, 0, 0))],
        out_specs=[q_spec,
                   pl.BlockSpec((1, SLC_LANES, tq), lambda gi, qi: (gi, 0, qi))],
        out_shape=[jax.ShapeDtypeStruct(qL.shape, F32),
                   jax.ShapeDtypeStruct((g, SLC_LANES, s), BF16)],
        compiler_params=_params(("parallel", "arbitrary")),
        name="cmp_select",
    )(qL, k_cmp, v_cmpT)


def _win_attn_kernel(q_ref, k0_ref, k1_ref, v0_ref, v1_ref, ow_ref, *, tq):
    dk = NSA_HEAD_DIM
    qi = pl.program_id(1)
    q = q_ref[0]
    t = qi * tq + lax.broadcasted_iota(jnp.int32, (1, tq), 1)
    s = []
    for c, k_ref in enumerate((k0_ref, k1_ref)):
        j = (qi - 1 + c) * tq + lax.broadcasted_iota(jnp.int32, (tq, 1), 0)
        bias = jnp.where((j >= 0) & (j <= t) & (t - j < WINDOW), 0.0, NEG_INF)
        sc = _dot(k_ref[:, dk:], q)
        s.append(jnp.concatenate([x + bias for x in _per_head(sc, tq)], axis=1))
    m = jnp.maximum(jnp.max(s[0], axis=0, keepdims=True), jnp.max(s[1], axis=0, keepdims=True))
    p = [jnp.exp2(x - m) for x in s]
    l = jnp.sum(p[0], axis=0, keepdims=True) + jnp.sum(p[1], axis=0, keepdims=True)
    o = _dot(v0_ref[...], p[0].astype(BF16)) + _dot(v1_ref[...], p[1].astype(BF16))
    ow_ref[0] = o / l


def _win_attn(qL, kr, vT, tq):
    g = NSA_KV_GROUPS
    s = qL.shape[2] // NSA_HPG
    assert WINDOW == tq
    q_spec = pl.BlockSpec((1, NSA_HEAD_DIM, NSA_HPG * tq), lambda gi, qi: (gi, 0, qi))

    def k_spec(c):
        return pl.BlockSpec((tq, 128), lambda gi, qi: (jnp.maximum(qi - 1 + c, 0), g + gi))

    def v_spec(c):
        return pl.BlockSpec((NSA_HEAD_DIM, tq), lambda gi, qi: (2 * gi + 1, jnp.maximum(qi - 1 + c, 0)))

    return pl.pallas_call(
        functools.partial(_win_attn_kernel, tq=tq),
        grid=(g, s // tq),
        in_specs=[q_spec, k_spec(0), k_spec(1), v_spec(0), v_spec(1)],
        out_specs=q_spec,
        out_shape=jax.ShapeDtypeStruct(qL.shape, F32),
        compiler_params=_params(("parallel", "arbitrary")),
        name="win_attn",
    )(qL, kr, kr, vT, vT)


def _slc_attn_kernel(qi_tab, ki_tab, q_ref, k_ref, vT_ref, selT_ref, oc_ref, ow_ref, gT_ref,
                     o_ref, rhs_ref, m_ref, l_ref, acc_ref, *, tq):
    dk = NSA_HEAD_DIM
    step = pl.program_id(1)
    qi = qi_tab[step]
    ki = ki_tab[step]

    @pl.when(ki == 0)
    def _():
        m_ref[...] = jnp.full_like(m_ref, NEG_INF)
        l_ref[...] = jnp.zeros_like(l_ref)
        acc_ref[...] = jnp.zeros_like(acc_ref)
        not_selected = ((selT_ref[0].astype(F32) - 1.0) * (-NEG_INF)).astype(BF16)
        rhs_ref[:SLC_LANES, :] = jnp.concatenate([not_selected] * NSA_HPG, axis=1)
        rhs_ref[SLC_LANES:SLC_LANES + dk, :] = q_ref[0]
        rhs_ref[SLC_LANES + dk:, :] = jnp.zeros((dk, NSA_HPG * tq), BF16)

    blk_of_key = ki * (tq // SLC_BLOCK) + lax.broadcasted_iota(jnp.int32, (tq, SLC_LANES), 0) // SLC_BLOCK
    one_hot = (lax.broadcasted_iota(jnp.int32, (tq, SLC_LANES), 1) == blk_of_key).astype(BF16)
    s = _dot(jnp.concatenate([one_hot, k_ref[...]], axis=1), rhs_ref[...])

    def update(s):
        m_old = m_ref[...]
        m_new = jnp.maximum(m_old, jnp.max(s, axis=0, keepdims=True))
        alpha = jnp.exp2(m_old - m_new)
        p = jnp.exp2(s - m_new)
        l_ref[...] = alpha * l_ref[...] + jnp.sum(p, axis=0, keepdims=True)
        acc_ref[...] = alpha * acc_ref[...] + _dot(vT_ref[...], p.astype(BF16))
        m_ref[...] = m_new

    @pl.when(ki < qi)
    def _():
        update(s)

    @pl.when(ki == qi)
    def _():
        causal = (lax.broadcasted_iota(jnp.int32, (tq, 1), 0) <= lax.broadcasted_iota(jnp.int32, (1, tq), 1))
        update(jnp.concatenate([jnp.where(causal, x, NEG_INF) for x in _per_head(s, tq)], axis=1))
        sig = _sigmoid_pair(gT_ref[...])[0]

        def gate(br):
            return jnp.concatenate([sig[br * NSA_HPG + hp:br * NSA_HPG + hp + 1] for hp in range(NSA_HPG)], axis=1)

        out = gate(0) * oc_ref[0] + gate(1) * (acc_ref[...] / l_ref[...]) + gate(2) * ow_ref[0]
        o_ref[0] = out.astype(o_ref.dtype)


def _slc_attn(qL, kr, vT, selT, ocL, owL, gT, tq):
    g = NSA_KV_GROUPS
    s = qL.shape[2] // NSA_HPG
    tk = tq
    pairs = [(qi, ki) for qi in range(s // tq) for ki in range(qi + 1)]
    qi_tab = jnp.asarray(np.array([p[0] for p in pairs], np.int32))
    ki_tab = jnp.asarray(np.array([p[1] for p in pairs], np.int32))
    q_spec = pl.BlockSpec((1, NSA_HEAD_DIM, NSA_HPG * tq), lambda gi, st, qt, kt: (gi, 0, qt[st]))
    grid_spec = pltpu.PrefetchScalarGridSpec(
        num_scalar_prefetch=2,
        grid=(g, len(pairs)),
        in_specs=[q_spec,
                  pl.BlockSpec((tk, 128), lambda gi, st, qt, kt: (kt[st], g + gi)),
                  pl.BlockSpec((NSA_HEAD_DIM, tk), lambda gi, st, qt, kt: (2 * gi, kt[st])),
                  pl.BlockSpec((1, SLC_LANES, tq), lambda gi, st, qt, kt: (gi, 0, qt[st])),
                  q_spec, q_spec,
                  pl.BlockSpec((GATE_ROWS, tq), lambda gi, st, qt, kt: (gi, qt[st]))],
        out_specs=q_spec,
        scratch_shapes=[pltpu.VMEM((SLC_LANES + 2 * NSA_HEAD_DIM, NSA_HPG * tq), BF16),
                        pltpu.VMEM((1, NSA_HPG * tq), F32),
                        pltpu.VMEM((1, NSA_HPG * tq), F32),
                        pltpu.VMEM((NSA_HEAD_DIM, NSA_HPG * tq), F32)])
    return pl.pallas_call(
        functools.partial(_slc_attn_kernel, tq=tq),
        grid_spec=grid_spec,
        out_shape=jax.ShapeDtypeStruct(qL.shape, BF16),
        compiler_params=_params(("parallel", "arbitrary")),
        name="slc_attn",
    )(qi_tab, ki_tab, qL, kr, vT, selT, ocL, owL, gT)


def _res_ln_epilogue(acc, x_ref, g_ref, b_ref, xo_ref, xb_ref):
    v = DN_ALPHA * x_ref[...] + acc
    mu = jnp.mean(v, axis=-1, keepdims=True)
    d = v - mu
    var = jnp.mean(d * d, axis=-1, keepdims=True)
    y = d * lax.rsqrt(var + LN_EPS) * g_ref[...] + b_ref[...]
    xo_ref[...] = y
    xb_ref[...] = y.astype(BF16)


def _mix_out_kernel(a1_ref, a2_ref, w_ref, x_ref, g_ref, b_ref, xo_ref, xb_ref, acc_ref):
    k = pl.program_id(1)

    @pl.when(k == 0)
    def _():
        acc_ref[...] = _dot(a1_ref[...], w_ref[...])

    @pl.when(k == 1)
    def _():
        _res_ln_epilogue(acc_ref[...] + _dot(a2_ref[...], w_ref[...]), x_ref, g_ref, b_ref, xo_ref, xb_ref)


def _mix_out(a1, a2, w, x, gamma, beta, tm):
    s, k1 = a1.shape
    n = w.shape[1]
    row = pl.BlockSpec((tm, n), lambda i, k: (i, 0))
    vec = pl.BlockSpec((1, n), lambda i, k: (0, 0))
    return pl.pallas_call(
        _mix_out_kernel,
        grid=(s // tm, 2),
        in_specs=[pl.BlockSpec((tm, k1), lambda i, k: (i, 0)),
                  pl.BlockSpec((tm, k1), lambda i, k: (i, 0)),
                  pl.BlockSpec((k1, n), lambda i, k: (k, 0)),
                  row, vec, vec],
        out_specs=[row, row],
        out_shape=[jax.ShapeDtypeStruct((s, n), F32), jax.ShapeDtypeStruct((s, n), BF16)],
        scratch_shapes=[pltpu.VMEM((tm, n), F32)],
        compiler_params=_params(("parallel", "arbitrary")),
        name="mix_out",
    )(a1, a2, w, x, gamma, beta)


def _ffn_down_kernel(a_ref, w_ref, x_ref, g_ref, b_ref, xo_ref, xb_ref, acc_ref):
    k = pl.program_id(1)

    @pl.when(k == 0)
    def _():
        acc_ref[...] = jnp.zeros_like(acc_ref)

    acc_ref[...] += _dot(a_ref[...], w_ref[...])

    @pl.when(k == pl.num_programs(1) - 1)
    def _():
        _res_ln_epilogue(acc_ref[...], x_ref, g_ref, b_ref, xo_ref, xb_ref)


def _ffn_down(a, w, x, gamma, beta, tm, tk):
    s, kk = a.shape
    n = w.shape[1]
    row = pl.BlockSpec((tm, n), lambda i, k: (i, 0))
    vec = pl.BlockSpec((1, n), lambda i, k: (0, 0))
    return pl.pallas_call(
        _ffn_down_kernel,
        grid=(s // tm, kk // tk),
        in_specs=[pl.BlockSpec((tm, tk), lambda i, k: (i, k)),
                  pl.BlockSpec((tk, n), lambda i, k: (k, 0)),
                  row, vec, vec],
        out_specs=[row, row],
        out_shape=[jax.ShapeDtypeStruct((s, n), F32), jax.ShapeDtypeStruct((s, n), BF16)],
        scratch_shapes=[pltpu.VMEM((tm, n), F32)],
        compiler_params=_params(("parallel", "arbitrary")),
        name="ffn_down",
    )(a, w, x, gamma, beta)


HALO = 16


def _ffn_up_kernel(a_ref, ap_ref, wg_ref, wu_ref, cwg_ref, cwu_ref, cbg_ref, cbu_ref, o_ref):
    ap = ap_ref[...]
    ap = jnp.where(pl.program_id(0) > 0, ap, jnp.zeros_like(ap))
    a = jnp.concatenate([ap, a_ref[...]], axis=0)

    def branch(w_ref, cw_ref, cb_ref):
        h = _dot(a, w_ref[...])
        cw = cw_ref[...]
        return (cw[2:3] * h[HALO:] + cw[1:2] * h[HALO - 1:-1] + cw[0:1] * h[HALO - 2:-2]) + cb_ref[...]

    gate = branch(wg_ref, cwg_ref, cbg_ref)
    up = branch(wu_ref, cwu_ref, cbu_ref)
    o_ref[...] = (_silu(gate) * up).astype(o_ref.dtype)


def _ffn_up(a, w, conv_w, conv_b, tm, tn):
    s, d = a.shape
    nj = D_FF // tn
    return pl.pallas_call(
        _ffn_up_kernel,
        grid=(s // tm, nj),
        in_specs=[pl.BlockSpec((tm, d), lambda i, j: (i, 0)),
                  pl.BlockSpec((HALO, d), lambda i, j: (jnp.maximum(i * (tm // HALO) - 1, 0), 0)),
                  pl.BlockSpec((d, tn), lambda i, j: (0, j)),
                  pl.BlockSpec((d, tn), lambda i, j: (0, j + nj)),
                  pl.BlockSpec((3, tn), lambda i, j: (0, j)),
                  pl.BlockSpec((3, tn), lambda i, j: (0, j + nj)),
                  pl.BlockSpec((1, tn), lambda i, j: (0, j)),
                  pl.BlockSpec((1, tn), lambda i, j: (0, j + nj))],
        out_specs=pl.BlockSpec((tm, tn), lambda i, j: (i, j)),
        out_shape=jax.ShapeDtypeStruct((s, D_FF), BF16),
        compiler_params=_params(("parallel", "arbitrary")),
        name="ffn_up",
    )(a, a, w, w, conv_w, conv_w, conv_b, conv_b)


def _rope_tables(s):
    half = NSA_HEAD_DIM // 2
    inv = ROPE_THETA ** (-jnp.arange(half, dtype=F32) / half)
    ang = jnp.arange(s).astype(F32)[:, None] * inv[None, :]
    cos, sin = jnp.cos(ang), jnp.sin(ang)
    cos64 = jnp.concatenate([cos, cos], -1)
    sin64 = jnp.concatenate([-sin, sin], -1)
    ones, zeros = jnp.ones_like(cos64), jnp.zeros_like(sin64)
    cos_tabs = jnp.stack([jnp.concatenate([cos64, ones], -1), jnp.concatenate([cos64, cos64], -1)])
    sin_tabs = jnp.stack([jnp.concatenate([sin64, zeros], -1), jnp.concatenate([sin64, sin64], -1)])
    q_scale = NSA_HEAD_DIM ** -0.5 * LOG2E
    return cos_tabs, sin_tabs, cos.T * q_scale, sin.T * q_scale


def _nsa_weight_layout(w_nsa):
    d = w_nsa.shape[0]
    g, dk = NSA_KV_GROUPS, NSA_HEAD_DIM
    q = w_nsa[:, :NSA_WIDTH]
    kv = w_nsa[:, NSA_WIDTH:NSA_WIDTH + 6 * NSA_KV_WIDTH].reshape(d, 6, g, dk)
    kc, vc, ks, vs, kw, vw = (kv[:, i] for i in range(6))
    w_rows = jnp.concatenate([jnp.stack([kc, vc], 2).reshape(d, 2 * g * dk),
                              jnp.stack([ks, kw], 2).reshape(d, 2 * g * dk)], axis=1)
    w_v = jnp.stack([vs, vw], 2).reshape(d, 2 * g * dk)
    gt = w_nsa[:, NSA_WIDTH + 6 * NSA_KV_WIDTH:].reshape(d, g, NSA_HPG, 3).transpose(0, 1, 3, 2)
    gt = jnp.pad(gt.reshape(d, g, 3 * NSA_HPG), ((0, 0), (0, 0), (0, GATE_ROWS - 3 * NSA_HPG)))
    w_vg = jnp.concatenate([w_v, gt.reshape(d, g * GATE_ROWS)], axis=1)
    return w_rows.astype(BF16), q.T.astype(BF16), w_vg.T.astype(BF16)


def kernel(x, w_in, w_out, hgrn_lb_logits, hgrn_norm_w, cmp_pe_k, cmp_pe_v, cmp_w1_k, cmp_w2_k,
           cmp_w1_v, cmp_w2_v, ln1_g, ln1_b, w_up, conv_w, conv_b, w_down, ln2_g, ln2_b):
    s = x.shape[1]
    n_blk = s // CMP_STRIDE
    g = NSA_KV_GROUPS
    p_lb = jax.nn.softmax(hgrn_lb_logits.astype(F32), axis=0)
    lower_bounds = jnp.cumsum(p_lb, axis=0) - p_lb[0:1]
    cos_tabs, sin_tabs, cosT, sinT = _rope_tables(s)
    hw = 4 * HGRN_WIDTH
    xf = x[0]
    xb = xf.astype(BF16)
    for l in range(DEPTH):
        w_h = w_in[l][:, :hw].astype(BF16)
        w_rows, w_qT, w_vgT = _nsa_weight_layout(w_in[l][:, hw:])
        xT = xb.T
        hh = _proj(xb, w_h, F32, 1024, 512)
        kr = _proj_rope(xb, w_rows, cos_tabs, sin_tabs, 1, 1024, 512)
        qL = _projT_rope(w_qT, xT, cosT, sinT, NSA_TQ)
        vT, gT = _projT_vg(w_vgT, xT, 2 * g * NSA_HEAD_DIM, 1024)
        o_h = _hgrn(hh, lower_bounds[l][None], hgrn_norm_w[l][None], 512, 4)

        kvc = kr[:, :2 * NSA_KV_WIDTH].reshape(n_blk, CMP_STRIDE, g, 2, NSA_HEAD_DIM)
        a_cmp = kvc.transpose(3, 2, 0, 1, 4).reshape(2, g, n_blk, CMP_STRIDE * NSA_HEAD_DIM)
        pe = jnp.stack([cmp_pe_k[l], cmp_pe_v[l]]).reshape(2, 1, CMP_BLOCK * NSA_HEAD_DIM)
        pe8 = jnp.pad(pe, ((0, 0), (0, 7), (0, 0))).astype(BF16)
        w1 = jnp.stack([cmp_w1_k[l], cmp_w1_v[l]]).astype(BF16)
        w2 = jnp.stack([cmp_w2_k[l], cmp_w2_v[l]]).astype(BF16)
        kv_cmp = _compress(a_cmp, pe8, w1, w2)
        ocL, selT = _cmp_select(qL, kv_cmp[0], kv_cmp[1].transpose(0, 2, 1), NSA_TQ)
        owL = _win_attn(qL, kr, vT, NSA_TQ)
        onL = _slc_attn(qL, kr, vT, selT, ocL, owL, gT, NSA_TQ)
        o_n = onL.reshape(g, NSA_HEAD_DIM, s // NSA_TQ, NSA_HPG, NSA_TQ).transpose(2, 4, 0, 3, 1).reshape(s, NSA_WIDTH)

        xf, xb = _mix_out(o_h, o_n, w_out[l].astype(BF16), xf, ln1_g[l][None], ln1_b[l][None], 512)
        u = _ffn_up(xb, w_up[l].astype(BF16), conv_w[l], conv_b[l][None], 1024, 512)
        xf, xb = _ffn_down(u, w_down[l].astype(BF16), xf, ln2_g[l][None], ln2_b[l][None], 512, 512)
    return xf[None]
```

```python
import functools

import numpy as np
import jax
import jax.numpy as jnp
from jax import lax
from jax.experimental import pallas as pl
from jax.experimental.pallas import tpu as pltpu

F32 = jnp.float32
BF16 = jnp.bfloat16

D_MODEL = 2048
DEPTH = 4
HGRN_WIDTH = 1024
HGRN_HEADS = 8
HEAD_LANES = 128
NSA_WIDTH = 1024
NSA_HEAD_DIM = 64
NSA_HEADS = 16
NSA_KV_GROUPS = 4
NSA_HPG = 4
NSA_KV_WIDTH = NSA_KV_GROUPS * NSA_HEAD_DIM
CMP_BLOCK = 32
CMP_STRIDE = 16
CMP_HIDDEN = 256
SLC_BLOCK = 64
SLC_TOPN = 16
SLC_LANES = 128
WINDOW = 512
D_FF = 5632
ROPE_THETA = 10000.0
LN_EPS = 1e-5
RMS_EPS = 1e-6
F_MIN = 1e-30
DN_ALPHA = (2 * DEPTH) ** 0.25
NEG_INF = -1e30
FORCE_SCORE = 1e9

GATE_ROWS = 16
LOG2E = 1.4426950408889634

NSA_TQ = 512
HGRN_CHUNK = 64
HGRN_SUB = 16
VMEM_LIMIT = 48 * 1024 * 1024
FFN_DOWN_VMEM_LIMIT = 56 * 1024 * 1024


def _params(sem):
    return pltpu.CompilerParams(dimension_semantics=sem, vmem_limit_bytes=VMEM_LIMIT)


def _dot(a, b):
    return jnp.dot(a, b, preferred_element_type=F32)


def _dot_nt(a, b):
    return lax.dot_general(a, b, (((1,), (1,)), ((), ())), preferred_element_type=F32)


def _dot_tn(a, b):
    return lax.dot_general(a, b, (((0,), (0,)), ((), ())), preferred_element_type=F32)


def _split3(x):
    hi = x.astype(BF16)
    r = x - hi.astype(F32)
    mid = r.astype(BF16)
    lo = (r - mid.astype(F32)).astype(BF16)
    return hi, mid, lo


def _sigmoid_pair(z):
    e = jnp.exp(-jnp.abs(z))
    r = 1.0 / (1.0 + e)
    er = e * r
    pos = z >= 0
    return jnp.where(pos, r, er), jnp.where(pos, er, r)


def _silu(x):
    return x * _sigmoid_pair(x)[0]


def _proj_kernel(x_ref, w_ref, o_ref):
    o_ref[...] = _dot(x_ref[...], w_ref[...]).astype(o_ref.dtype)


def _proj(x, w, out_dtype, tm, tn):
    m, k = x.shape
    n = w.shape[1]
    return pl.pallas_call(
        _proj_kernel,
        grid=(m // tm, n // tn),
        in_specs=[pl.BlockSpec((tm, k), lambda i, j: (i, 0)),
                  pl.BlockSpec((k, tn), lambda i, j: (0, j))],
        out_specs=pl.BlockSpec((tm, tn), lambda i, j: (i, j)),
        out_shape=jax.ShapeDtypeStruct((m, n), out_dtype),
        compiler_params=_params(("parallel", "arbitrary")),
        name="proj",
    )(x, w)


def _proj_rope_kernel(x_ref, w_ref, cos_ref, sin_ref, o_ref, *, tn):
    h = _dot(x_ref[...], w_ref[...])
    cos = cos_ref[0]
    sin = sin_ref[0]
    lane = lax.broadcasted_iota(jnp.int32, cos.shape, 1)
    first_half = (lane % NSA_HEAD_DIM) < (NSA_HEAD_DIM // 2)
    for c in range(tn // 128):
        hc = h[:, c * 128:(c + 1) * 128]
        rot = jnp.where(first_half, pltpu.roll(hc, 96, axis=1), pltpu.roll(hc, 32, axis=1))
        o_ref[:, c * 128:(c + 1) * 128] = (hc * cos + rot * sin).astype(o_ref.dtype)


def _proj_rope(x, w, cos_tabs, sin_tabs, n_first, tm, tn):
    m, k = x.shape
    n = w.shape[1]
    tab_spec = pl.BlockSpec((1, tm, 128), lambda i, j: (jnp.where(j < n_first, 0, 1), i, 0))
    return pl.pallas_call(
        functools.partial(_proj_rope_kernel, tn=tn),
        grid=(m // tm, n // tn),
        in_specs=[pl.BlockSpec((tm, k), lambda i, j: (i, 0)),
                  pl.BlockSpec((k, tn), lambda i, j: (0, j)),
                  tab_spec, tab_spec],
        out_specs=pl.BlockSpec((tm, tn), lambda i, j: (i, j)),
        out_shape=jax.ShapeDtypeStruct((m, n), BF16),
        compiler_params=_params(("parallel", "arbitrary")),
        name="proj_rope",
    )(x, w, cos_tabs, sin_tabs)


def _hgrn_chunk(q, z, v, g, lb, nw, st, tri, sub_row):
    C, c = HGRN_CHUNK, HGRN_SUB
    one_minus_lb = 1.0 - lb
    sig, sig_neg = _sigmoid_pair(z)
    f = lb + one_minus_lb * sig
    lf = jnp.log2(jnp.maximum(f, F_MIN))
    k = one_minus_lb * sig_neg
    qs = _silu(q)
    lf_hi, lf_mid, lf_lo = _split3(lf)
    b = _dot(tri, lf_hi) + _dot(tri, lf_mid) + _dot(tri, lf_lo)
    o_inter = _dot_nt((qs * jnp.exp2(b)).astype(BF16), st.astype(BF16))
    v16 = v.astype(BF16)
    outs = []
    for i in range(C // c):
        lo = i * c
        b_i = b[lo:lo + c]
        qs_i = qs[lo:lo + c]
        k_i = k[lo:lo + c]
        v_i = v[lo:lo + c]
        o_i = o_inter[lo:lo + c]
        if i > 0:
            beta = b[lo - 1:lo]
            qt = (qs_i * jnp.exp2(b_i - beta)).astype(BF16)
            kt = (k[:lo] * jnp.exp2(beta - b[:lo])).astype(BF16)
            att = _dot_nt(qt, kt)
            o_i = o_i + _dot(att.astype(BF16), v16[:lo])
        for s in range(c):
            d = jnp.exp2(b_i - b_i[s:s + 1])
            w = jnp.sum(qs_i * d * k_i[s:s + 1], axis=-1, keepdims=True)
            w = jnp.where(sub_row >= s, w, 0.0)
            o_i = o_i + w * v_i[s:s + 1]
        outs.append(o_i)
    o = jnp.concatenate(outs, axis=0)
    b_last = b[C - 1:C]
    kd = (k * jnp.exp2(b_last - b)).astype(BF16)
    st_new = jnp.exp2(b_last) * st + _dot_tn(v16, kd)
    o = o * lax.rsqrt(jnp.mean(o * o, axis=-1, keepdims=True) + RMS_EPS)
    return o * nw * _silu(g), st_new


def _hgrn_kernel(q_ref, z_ref, v_ref, g_ref, lb_ref, nw_ref, o_ref, st_ref, *, tb, hb):
    C, c = HGRN_CHUNK, HGRN_SUB

    @pl.when(pl.program_id(1) == 0)
    def _():
        st_ref[...] = jnp.zeros_like(st_ref)

    tri = (lax.broadcasted_iota(jnp.int32, (C, C), 1)
           <= lax.broadcasted_iota(jnp.int32, (C, C), 0)).astype(BF16)
    sub_row = lax.broadcasted_iota(jnp.int32, (c, 1), 0)

    def chunk(ci, carry):
        r0 = pl.multiple_of(ci * C, C)
        for h in range(hb):
            lanes = slice(h * HEAD_LANES, (h + 1) * HEAD_LANES)
            out, st_new = _hgrn_chunk(q_ref[pl.ds(r0, C), lanes], z_ref[pl.ds(r0, C), lanes],
                                      v_ref[pl.ds(r0, C), lanes], g_ref[pl.ds(r0, C), lanes],
                                      lb_ref[:, lanes], nw_ref[:, lanes], st_ref[h], tri, sub_row)
            st_ref[h] = st_new
            o_ref[pl.ds(r0, C), lanes] = out.astype(o_ref.dtype)
        return carry

    lax.fori_loop(0, tb // C, chunk, 0)


def _hgrn(hh, lb, nw, tb, hb):
    s = hh.shape[0]
    nhb = HGRN_HEADS // hb
    wide = hb * HEAD_LANES

    def col(off):
        return pl.BlockSpec((tb, wide), lambda h, t: (t, off * nhb + h))

    vec = pl.BlockSpec((1, wide), lambda h, t: (0, h))
    return pl.pallas_call(
        functools.partial(_hgrn_kernel, tb=tb, hb=hb),
        grid=(nhb, s // tb),
        in_specs=[col(0), col(1), col(2), col(3), vec, vec],
        out_specs=pl.BlockSpec((tb, wide), lambda h, t: (t, h)),
        out_shape=jax.ShapeDtypeStruct((s, HGRN_WIDTH), BF16),
        scratch_shapes=[pltpu.VMEM((hb, HEAD_LANES, HEAD_LANES), F32)],
        compiler_params=_params(("parallel", "arbitrary")),
        name="hgrn2",
    )(hh, hh, hh, hh, lb, nw)


def _projT_rope_kernel(w_ref, xT_ref, cos_ref, sin_ref, o_ref, *, tq):
    h = _dot(w_ref[...], xT_ref[...])
    cos = cos_ref[...]
    sin = sin_ref[...]
    half = NSA_HEAD_DIM // 2
    for hp in range(NSA_HPG):
        r = hp * NSA_HEAD_DIM
        x1 = h[r:r + half]
        x2 = h[r + half:r + 2 * half]
        o_ref[0, :half, hp * tq:(hp + 1) * tq] = (x1 * cos - x2 * sin).astype(o_ref.dtype)
        o_ref[0, half:, hp * tq:(hp + 1) * tq] = (x2 * cos + x1 * sin).astype(o_ref.dtype)


def _projT_rope(wT, xT, cosT, sinT, tq):
    n, d = wT.shape
    s = xT.shape[1]
    half = NSA_HEAD_DIM // 2
    gw = NSA_HPG * NSA_HEAD_DIM
    return pl.pallas_call(
        functools.partial(_projT_rope_kernel, tq=tq),
        grid=(s // tq, n // gw),
        in_specs=[pl.BlockSpec((gw, d), lambda i, j: (j, 0)),
                  pl.BlockSpec((d, tq), lambda i, j: (0, i)),
                  pl.BlockSpec((half, tq), lambda i, j: (0, i)),
                  pl.BlockSpec((half, tq), lambda i, j: (0, i))],
        out_specs=pl.BlockSpec((1, NSA_HEAD_DIM, NSA_HPG * tq), lambda i, j: (j, 0, i)),
        out_shape=jax.ShapeDtypeStruct((n // gw, NSA_HEAD_DIM, NSA_HPG * s), BF16),
        compiler_params=_params(("parallel", "arbitrary")),
        name="projT_rope",
    )(wT, xT, cosT, sinT)


def _projT_vg_kernel(w_ref, xT_ref, v_ref, g_ref):
    h = _dot(w_ref[...], xT_ref[...])
    nv = v_ref.shape[0]
    v_ref[...] = h[:nv].astype(v_ref.dtype)
    g_ref[...] = h[nv:]


def _projT_vg(wT, xT, nv, ts):
    n, d = wT.shape
    s = xT.shape[1]
    return pl.pallas_call(
        _projT_vg_kernel,
        grid=(s // ts,),
        in_specs=[pl.BlockSpec((n, d), lambda i: (0, 0)),
                  pl.BlockSpec((d, ts), lambda i: (0, i))],
        out_specs=[pl.BlockSpec((nv, ts), lambda i: (0, i)),
                   pl.BlockSpec((n - nv, ts), lambda i: (0, i))],
        out_shape=[jax.ShapeDtypeStruct((nv, s), BF16), jax.ShapeDtypeStruct((n - nv, s), F32)],
        compiler_params=_params(("parallel",)),
        name="projT_vg",
    )(wT, xT)


def _compress_kernel(a_ref, pe_ref, w1_ref, w2_ref, o_ref):
    a = a_ref[0, 0]
    w1 = w1_ref[0]
    half = w1.shape[0] // 2
    n_blk = a.shape[0]
    top = _dot(a, w1[:half])
    bot = _dot(a, w1[half:])
    pe_term = _dot(pe_ref[0], w1)[0:1]
    hid = top + pltpu.roll(bot, n_blk - 1, axis=0) + pe_term
    o_ref[0, 0] = _dot(_silu(hid).astype(BF16), w2_ref[0]).astype(o_ref.dtype)


def _compress(a, pe8, w1, w2):
    _, g, n_blk, ka = a.shape
    return pl.pallas_call(
        _compress_kernel,
        grid=(2, g),
        in_specs=[pl.BlockSpec((1, 1, n_blk, ka), lambda b, gi: (b, gi, 0, 0)),
                  pl.BlockSpec((1, 8, 2 * ka), lambda b, gi: (b, 0, 0)),
                  pl.BlockSpec((1, 2 * ka, CMP_HIDDEN), lambda b, gi: (b, 0, 0)),
                  pl.BlockSpec((1, CMP_HIDDEN, NSA_HEAD_DIM), lambda b, gi: (b, 0, 0))],
        out_specs=pl.BlockSpec((1, 1, n_blk, NSA_HEAD_DIM), lambda b, gi: (b, gi, 0, 0)),
        out_shape=jax.ShapeDtypeStruct((2, g, n_blk, NSA_HEAD_DIM), BF16),
        compiler_params=_params(("parallel", "arbitrary")),
        name="compress",
    )(a, pe8, w1, w2)


ONES_ROWS = 16


def _with_ones_rows(vT):
    return jnp.concatenate([vT, jnp.ones((ONES_ROWS, vT.shape[1]), vT.dtype)], axis=0)


def _per_head(x, tq):
    return [x[:, hp * tq:(hp + 1) * tq] for hp in range(NSA_HPG)]


def _cmp_select_kernel(q_ref, kc_ref, vcT_ref, oc_ref, selT_ref, *, tq):
    kc = kc_ref[0]
    vcT = vcT_ref[0]
    n_blk = kc.shape[0]
    t = pl.program_id(1) * tq + lax.broadcasted_iota(jnp.int32, (1, tq), 1)
    n_idx = lax.broadcasted_iota(jnp.int32, (n_blk, 1), 0)
    bias = jnp.where((n_idx * CMP_STRIDE + (CMP_BLOCK - 1)) <= t, 0.0, NEG_INF)
    any_visible = (t >= CMP_BLOCK - 1).astype(F32)
    s = _dot(kc, q_ref[0])
    s = jnp.concatenate([x + bias for x in _per_head(s, tq)], axis=1)
    e = jnp.exp2(s - jnp.max(s, axis=0, keepdims=True))
    scale = jnp.concatenate([any_visible] * NSA_HPG, axis=1) / jnp.sum(e, axis=0, keepdims=True)
    p = e * scale
    oc_ref[0] = _dot(vcT, p.astype(BF16))
    p_sum = functools.reduce(lambda a, b: a + b, _per_head(p, tq))
    sj = lax.broadcasted_iota(jnp.int32, (SLC_LANES, n_blk), 0) * SLC_BLOCK
    ci = lax.broadcasted_iota(jnp.int32, (SLC_LANES, n_blk), 1) * CMP_STRIDE
    overlap = ((ci < sj + SLC_BLOCK) & (ci + CMP_BLOCK > sj)).astype(BF16)
    p_hi, p_mid, p_lo = _split3(p_sum)
    imp = _dot(overlap, p_hi) + _dot(overlap, p_mid) + _dot(overlap, p_lo)
    blk = lax.broadcasted_iota(jnp.int32, (SLC_LANES, 1), 0)
    cur = t // SLC_BLOCK
    forced = (blk == 0) | (blk == cur) | (blk == cur - 1)
    causal = blk * SLC_BLOCK <= t
    work = jnp.where(forced, FORCE_SCORE, jnp.where(causal, imp, -1.0))
    blk_f = blk.astype(F32)
    sel = jnp.zeros((SLC_LANES, tq), F32)
    for _ in range(SLC_TOPN):
        mx = jnp.max(work, axis=0, keepdims=True)
        first = jnp.min(jnp.where(work == mx, blk_f, float(SLC_LANES)), axis=0, keepdims=True)
        pick = blk_f == first
        sel = jnp.where(pick & (mx >= 0.0), 1.0, sel)
        work = jnp.where(pick, -3e38, work)
    selT_ref[0] = sel.astype(selT_ref.dtype)


def _cmp_select(qL, k_cmp, v_cmpT, tq):
    g = NSA_KV_GROUPS
    s = qL.shape[2] // NSA_HPG
    n_blk = k_cmp.shape[1]
    q_spec = pl.BlockSpec((1, NSA_HEAD_DIM, NSA_HPG * tq), lambda gi, qi: (gi, 0, qi))
    return pl.pallas_call(
        functools.partial(_cmp_select_kernel, tq=tq),
        grid=(g, s // tq),
        in_specs=[q_spec,
                  pl.BlockSpec((1, n_blk, NSA_HEAD_DIM), lambda gi, qi: (gi, 0, 0)),
                  pl.BlockSpec((1, NSA_HEAD_DIM, n_blk), lambda gi, qi: (gi, 0, 0))],
        out_specs=[q_spec,
                   pl.BlockSpec((1, SLC_LANES, tq), lambda gi, qi: (gi, 0, qi))],
        out_shape=[jax.ShapeDtypeStruct(qL.shape, F32),
                   jax.ShapeDtypeStruct((g, SLC_LANES, s), BF16)],
        compiler_params=_params(("parallel", "arbitrary")),
        name="cmp_select",
    )(qL, k_cmp, v_cmpT)


def _win_attn_kernel(q_ref, k0_ref, k1_ref, v0_ref, v1_ref, ow_ref, *, tq):
    dk = NSA_HEAD_DIM
    qi = pl.program_id(1)
    q = q_ref[0]
    t = qi * tq + lax.broadcasted_iota(jnp.int32, (1, tq), 1)
    s = []
    for c, k_ref in enumerate((k0_ref, k1_ref)):
        j = (qi - 1 + c) * tq + lax.broadcasted_iota(jnp.int32, (tq, 1), 0)
        bias = jnp.where((j >= 0) & (j <= t) & (t - j < WINDOW), 0.0, NEG_INF)
        sc = _dot(k_ref[:, dk:], q)
        s.append(jnp.concatenate([x + bias for x in _per_head(sc, tq)], axis=1))
    m = jnp.maximum(jnp.max(s[0], axis=0, keepdims=True), jnp.max(s[1], axis=0, keepdims=True))
    o = (_dot(_with_ones_rows(v0_ref[...]), jnp.exp2(s[0] - m).astype(BF16))
         + _dot(_with_ones_rows(v1_ref[...]), jnp.exp2(s[1] - m).astype(BF16)))
    ow_ref[0] = o[:dk] / o[dk:dk + 1]


def _win_attn(qL, kr, vT, tq):
    g = NSA_KV_GROUPS
    s = qL.shape[2] // NSA_HPG
    assert WINDOW == tq
    q_spec = pl.BlockSpec((1, NSA_HEAD_DIM, NSA_HPG * tq), lambda gi, qi: (gi, 0, qi))

    def k_spec(c):
        return pl.BlockSpec((tq, 128), lambda gi, qi: (jnp.maximum(qi - 1 + c, 0), g + gi))

    def v_spec(c):
        return pl.BlockSpec((NSA_HEAD_DIM, tq), lambda gi, qi: (2 * gi + 1, jnp.maximum(qi - 1 + c, 0)))

    return pl.pallas_call(
        functools.partial(_win_attn_kernel, tq=tq),
        grid=(g, s // tq),
        in_specs=[q_spec, k_spec(0), k_spec(1), v_spec(0), v_spec(1)],
        out_specs=q_spec,
        out_shape=jax.ShapeDtypeStruct(qL.shape, F32),
        compiler_params=_params(("parallel", "arbitrary")),
        name="win_attn",
    )(qL, kr, kr, vT, vT)


def _slc_attn_kernel(qi_tab, ki_tab, q_ref, k_ref, vT_ref, selT_ref, oc_ref, ow_ref, gT_ref,
                     o_ref, rhs_ref, m_ref, acc_ref, *, tq):
    dk = NSA_HEAD_DIM
    step = pl.program_id(1)
    qi = qi_tab[step]
    ki = ki_tab[step]

    @pl.when(ki == 0)
    def _():
        m_ref[...] = jnp.full_like(m_ref, NEG_INF)
        acc_ref[...] = jnp.zeros_like(acc_ref)
        not_selected = ((selT_ref[0].astype(F32) - 1.0) * (-NEG_INF)).astype(BF16)
        rhs_ref[:SLC_LANES, :] = jnp.concatenate([not_selected] * NSA_HPG, axis=1)
        rhs_ref[SLC_LANES:SLC_LANES + dk, :] = q_ref[0]
        rhs_ref[SLC_LANES + dk:, :] = jnp.zeros((dk, NSA_HPG * tq), BF16)

    blk_of_key = ki * (tq // SLC_BLOCK) + lax.broadcasted_iota(jnp.int32, (tq, SLC_LANES), 0) // SLC_BLOCK
    one_hot = (lax.broadcasted_iota(jnp.int32, (tq, SLC_LANES), 1) == blk_of_key).astype(BF16)
    lhs = jnp.concatenate([one_hot, k_ref[...]], axis=1)

    def update(causal):
        v_aug = _with_ones_rows(vT_ref[...])
        scores = lambda hp: _dot(lhs, rhs_ref[:, hp * tq:(hp + 1) * tq])
        s_next = scores(0)
        for hp in range(NSA_HPG):
            lanes = slice(hp * tq, (hp + 1) * tq)
            s = s_next
            if hp + 1 < NSA_HPG:
                s_next = scores(hp + 1)
            if causal is not None:
                s = jnp.where(causal, s, NEG_INF)
            m_old = m_ref[:, lanes]
            m_new = jnp.maximum(m_old, jnp.max(s, axis=0, keepdims=True))
            p = jnp.exp2(s - m_new).astype(BF16)
            acc_ref[:, lanes] = jnp.exp2(m_old - m_new) * acc_ref[:, lanes] + _dot(v_aug, p)
            m_ref[:, lanes] = m_new

    @pl.when(ki < qi)
    def _():
        update(None)

    @pl.when(ki == qi)
    def _():
        update(lax.broadcasted_iota(jnp.int32, (tq, 1), 0) <= lax.broadcasted_iota(jnp.int32, (1, tq), 1))
        sig = _sigmoid_pair(gT_ref[...])[0]

        def gate(br):
            return jnp.concatenate([sig[br * NSA_HPG + hp:br * NSA_HPG + hp + 1] for hp in range(NSA_HPG)], axis=1)

        o_s = acc_ref[:dk, :] / acc_ref[dk:dk + 1, :]
        out = gate(0) * oc_ref[0] + gate(1) * o_s + gate(2) * ow_ref[0]
        o_ref[...] = jnp.concatenate(_per_head(out, tq), axis=0).T.astype(o_ref.dtype)


def _slc_attn(qL, kr, vT, selT, ocL, owL, gT, tq):
    g = NSA_KV_GROUPS
    s = qL.shape[2] // NSA_HPG
    tk = tq
    pairs = [(qi, ki) for qi in range(s // tq) for ki in range(qi + 1)]
    qi_tab = jnp.asarray(np.array([p[0] for p in pairs], np.int32))
    ki_tab = jnp.asarray(np.array([p[1] for p in pairs], np.int32))
    q_spec = pl.BlockSpec((1, NSA_HEAD_DIM, NSA_HPG * tq), lambda gi, st, qt, kt: (gi, 0, qt[st]))
    grid_spec = pltpu.PrefetchScalarGridSpec(
        num_scalar_prefetch=2,
        grid=(g, len(pairs)),
        in_specs=[q_spec,
                  pl.BlockSpec((tk, 128), lambda gi, st, qt, kt: (kt[st], g + gi)),
                  pl.BlockSpec((NSA_HEAD_DIM, tk), lambda gi, st, qt, kt: (2 * gi, kt[st])),
                  pl.BlockSpec((1, SLC_LANES, tq), lambda gi, st, qt, kt: (gi, 0, qt[st])),
                  q_spec, q_spec,
                  pl.BlockSpec((GATE_ROWS, tq), lambda gi, st, qt, kt: (gi, qt[st]))],
        out_specs=pl.BlockSpec((tq, NSA_HPG * NSA_HEAD_DIM), lambda gi, st, qt, kt: (qt[st], gi)),
        scratch_shapes=[pltpu.VMEM((SLC_LANES + 2 * NSA_HEAD_DIM, NSA_HPG * tq), BF16),
                        pltpu.VMEM((1, NSA_HPG * tq), F32),
                        pltpu.VMEM((NSA_HEAD_DIM + ONES_ROWS, NSA_HPG * tq), F32)])
    return pl.pallas_call(
        functools.partial(_slc_attn_kernel, tq=tq),
        grid_spec=grid_spec,
        out_shape=jax.ShapeDtypeStruct((s, NSA_WIDTH), BF16),
        compiler_params=_params(("parallel", "arbitrary")),
        name="slc_attn",
    )(qi_tab, ki_tab, qL, kr, vT, selT, ocL, owL, gT)


def _res_ln_epilogue(acc, x_ref, g_ref, b_ref, xo_ref, xb_ref):
    v = DN_ALPHA * x_ref[...] + acc
    mu = jnp.mean(v, axis=-1, keepdims=True)
    d = v - mu
    var = jnp.mean(d * d, axis=-1, keepdims=True)
    y = d * lax.rsqrt(var + LN_EPS) * g_ref[...] + b_ref[...]
    xo_ref[...] = y
    xb_ref[...] = y.astype(BF16)


def _mix_out_kernel(a1_ref, a2_ref, w_ref, x_ref, g_ref, b_ref, xo_ref, xb_ref):
    k1 = a1_ref.shape[1]
    y = _dot(a1_ref[...], w_ref[:k1, :]) + _dot(a2_ref[...], w_ref[k1:, :])
    _res_ln_epilogue(y, x_ref, g_ref, b_ref, xo_ref, xb_ref)


def _mix_out(a1, a2, w, x, gamma, beta, tm):
    s, k1 = a1.shape
    n = w.shape[1]
    row = pl.BlockSpec((tm, n), lambda i: (i, 0))
    vec = pl.BlockSpec((1, n), lambda i: (0, 0))
    return pl.pallas_call(
        _mix_out_kernel,
        grid=(s // tm,),
        in_specs=[pl.BlockSpec((tm, k1), lambda i: (i, 0)),
                  pl.BlockSpec((tm, k1), lambda i: (i, 0)),
                  pl.BlockSpec(w.shape, lambda i: (0, 0)),
                  row, vec, vec],
        out_specs=[row, row],
        out_shape=[jax.ShapeDtypeStruct((s, n), F32), jax.ShapeDtypeStruct((s, n), BF16)],
        compiler_params=_params(("parallel",)),
        name="mix_out",
    )(a1, a2, w, x, gamma, beta)


def _ffn_down_kernel(a_ref, w_ref, x_ref, g_ref, b_ref, xo_ref, xb_ref, acc_ref):
    k = pl.program_id(1)

    @pl.when(k == 0)
    def _():
        acc_ref[...] = jnp.zeros_like(acc_ref)

    acc_ref[...] += _dot(a_ref[...], w_ref[...])

    @pl.when(k == pl.num_programs(1) - 1)
    def _():
        _res_ln_epilogue(acc_ref[...], x_ref, g_ref, b_ref, xo_ref, xb_ref)


def _ffn_down(a, w, x, gamma, beta, tm, tk):
    s, kk = a.shape
    n = w.shape[1]
    row = pl.BlockSpec((tm, n), lambda i, k: (i, 0))
    vec = pl.BlockSpec((1, n), lambda i, k: (0, 0))
    res = pl.BlockSpec((tm, n), lambda i, k: (i, 0), pipeline_mode=pl.Buffered(1))
    return pl.pallas_call(
        _ffn_down_kernel,
        grid=(s // tm, kk // tk),
        in_specs=[pl.BlockSpec((tm, tk), lambda i, k: (i, k)),
                  pl.BlockSpec((tk, n), lambda i, k: (k, 0)),
                  res, vec, vec],
        out_specs=[row, row],
        out_shape=[jax.ShapeDtypeStruct((s, n), F32), jax.ShapeDtypeStruct((s, n), BF16)],
        scratch_shapes=[pltpu.VMEM((tm, n), F32)],
        compiler_params=pltpu.CompilerParams(dimension_semantics=("parallel", "arbitrary"),
                                             vmem_limit_bytes=FFN_DOWN_VMEM_LIMIT),
        name="ffn_down",
    )(a, w, x, gamma, beta)


HALO = 16


def _ffn_up_kernel(a_ref, ap_ref, wg_ref, wu_ref, cwg_ref, cwu_ref, cbg_ref, cbu_ref, o_ref):
    ap = ap_ref[...]
    ap = jnp.where(pl.program_id(0) > 0, ap, jnp.zeros_like(ap))
    a = jnp.concatenate([ap, a_ref[...]], axis=0)

    def branch(w_ref, cw_ref, cb_ref):
        h = _dot(a, w_ref[...])
        cw = cw_ref[...]
        return (cw[2:3] * h[HALO:] + cw[1:2] * h[HALO - 1:-1] + cw[0:1] * h[HALO - 2:-2]) + cb_ref[...]

    gate = branch(wg_ref, cwg_ref, cbg_ref)
    up = branch(wu_ref, cwu_ref, cbu_ref)
    o_ref[...] = (_silu(gate) * up).astype(o_ref.dtype)


def _ffn_up(a, w, conv_w, conv_b, tm, tn):
    s, d = a.shape
    nj = D_FF // tn
    return pl.pallas_call(
        _ffn_up_kernel,
        grid=(s // tm, nj),
        in_specs=[pl.BlockSpec((tm, d), lambda i, j: (i, 0)),
                  pl.BlockSpec((HALO, d), lambda i, j: (jnp.maximum(i * (tm // HALO) - 1, 0), 0)),
                  pl.BlockSpec((d, tn), lambda i, j: (0, j)),
                  pl.BlockSpec((d, tn), lambda i, j: (0, j + nj)),
                  pl.BlockSpec((3, tn), lambda i, j: (0, j)),
                  pl.BlockSpec((3, tn), lambda i, j: (0, j + nj)),
                  pl.BlockSpec((1, tn), lambda i, j: (0, j)),
                  pl.BlockSpec((1, tn), lambda i, j: (0, j + nj))],
        out_specs=pl.BlockSpec((tm, tn), lambda i, j: (i, j)),
        out_shape=jax.ShapeDtypeStruct((s, D_FF), BF16),
        compiler_params=_params(("parallel", "arbitrary")),
        name="ffn_up",
    )(a, a, w, w, conv_w, conv_w, conv_b, conv_b)


def _rope_tables(s):
    half = NSA_HEAD_DIM // 2
    inv = ROPE_THETA ** (-jnp.arange(half, dtype=F32) / half)
    ang = jnp.arange(s).astype(F32)[:, None] * inv[None, :]
    cos, sin = jnp.cos(ang), jnp.sin(ang)
    cos64 = jnp.concatenate([cos, cos], -1)
    sin64 = jnp.concatenate([-sin, sin], -1)
    ones, zeros = jnp.ones_like(cos64), jnp.zeros_like(sin64)
    cos_tabs = jnp.stack([jnp.concatenate([cos64, ones], -1), jnp.concatenate([cos64, cos64], -1)])
    sin_tabs = jnp.stack([jnp.concatenate([sin64, zeros], -1), jnp.concatenate([sin64, sin64], -1)])
    q_scale = NSA_HEAD_DIM ** -0.5 * LOG2E
    return cos_tabs, sin_tabs, cos.T * q_scale, sin.T * q_scale


def _nsa_weight_layout(w_nsa):
    d = w_nsa.shape[0]
    g, dk = NSA_KV_GROUPS, NSA_HEAD_DIM
    q = w_nsa[:, :NSA_WIDTH]
    kv = w_nsa[:, NSA_WIDTH:NSA_WIDTH + 6 * NSA_KV_WIDTH].reshape(d, 6, g, dk)
    kc, vc, ks, vs, kw, vw = (kv[:, i] for i in range(6))
    w_rows = jnp.concatenate([jnp.stack([kc, vc], 2).reshape(d, 2 * g * dk),
                              jnp.stack([ks, kw], 2).reshape(d, 2 * g * dk)], axis=1)
    w_v = jnp.stack([vs, vw], 2).reshape(d, 2 * g * dk)
    gt = w_nsa[:, NSA_WIDTH + 6 * NSA_KV_WIDTH:].reshape(d, g, NSA_HPG, 3).transpose(0, 1, 3, 2)
    gt = jnp.pad(gt.reshape(d, g, 3 * NSA_HPG), ((0, 0), (0, 0), (0, GATE_ROWS - 3 * NSA_HPG)))
    w_vg = jnp.concatenate([w_v, gt.reshape(d, g * GATE_ROWS)], axis=1)
    return w_rows.astype(BF16), q.T.astype(BF16), w_vg.T.astype(BF16)


def kernel(x, w_in, w_out, hgrn_lb_logits, hgrn_norm_w, cmp_pe_k, cmp_pe_v, cmp_w1_k, cmp_w2_k,
           cmp_w1_v, cmp_w2_v, ln1_g, ln1_b, w_up, conv_w, conv_b, w_down, ln2_g, ln2_b):
    s = x.shape[1]
    n_blk = s // CMP_STRIDE
    g = NSA_KV_GROUPS
    p_lb = jax.nn.softmax(hgrn_lb_logits.astype(F32), axis=0)
    lower_bounds = jnp.cumsum(p_lb, axis=0) - p_lb[0:1]
    cos_tabs, sin_tabs, cosT, sinT = _rope_tables(s)
    hw = 4 * HGRN_WIDTH
    xf = x[0]
    xb = xf.astype(BF16)
    for l in range(DEPTH):
        w_h = w_in[l][:, :hw].astype(BF16)
        w_rows, w_qT, w_vgT = _nsa_weight_layout(w_in[l][:, hw:])
        xT = xb.T
        hh = _proj(xb, w_h, F32, 1024, 512)
        kr = _proj_rope(xb, w_rows, cos_tabs, sin_tabs, 1, 1024, 512)
        qL = _projT_rope(w_qT, xT, cosT, sinT, NSA_TQ)
        vT, gT = _projT_vg(w_vgT, xT, 2 * g * NSA_HEAD_DIM, 1024)
        o_h = _hgrn(hh, lower_bounds[l][None], hgrn_norm_w[l][None], 512, 8)

        kvc = kr[:, :2 * NSA_KV_WIDTH].reshape(n_blk, CMP_STRIDE, g, 2, NSA_HEAD_DIM)
        a_cmp = kvc.transpose(3, 2, 0, 1, 4).reshape(2, g, n_blk, CMP_STRIDE * NSA_HEAD_DIM)
        pe = jnp.stack([cmp_pe_k[l], cmp_pe_v[l]]).reshape(2, 1, CMP_BLOCK * NSA_HEAD_DIM)
        pe8 = jnp.pad(pe, ((0, 0), (0, 7), (0, 0))).astype(BF16)
        w1 = jnp.stack([cmp_w1_k[l], cmp_w1_v[l]]).astype(BF16)
        w2 = jnp.stack([cmp_w2_k[l], cmp_w2_v[l]]).astype(BF16)
        kv_cmp = _compress(a_cmp, pe8, w1, w2)
        ocL, selT = _cmp_select(qL, kv_cmp[0], kv_cmp[1].transpose(0, 2, 1), NSA_TQ)
        owL = _win_attn(qL, kr, vT, NSA_TQ)
        o_n = _slc_attn(qL, kr, vT, selT, ocL, owL, gT, NSA_TQ)

        xf, xb = _mix_out(o_h, o_n, w_out[l].astype(BF16), xf, ln1_g[l][None], ln1_b[l][None], 512)
        u = _ffn_up(xb, w_up[l].astype(BF16), conv_w[l], conv_b[l][None], 1024, 512)
        xf, xb = _ffn_down(u, w_down[l].astype(BF16), xf, ln2_g[l][None], ln2_b[l][None], 1024, 512)
    return xf[None]
```

```python
import functools

import numpy as np
import jax
import jax.numpy as jnp
from jax import lax
from jax.experimental import pallas as pl
from jax.experimental.pallas import tpu as pltpu

F32 = jnp.float32
BF16 = jnp.bfloat16

D_MODEL = 2048
DEPTH = 4
HGRN_WIDTH = 1024
HGRN_HEADS = 8
HEAD_LANES = 128
NSA_WIDTH = 1024
NSA_HEAD_DIM = 64
NSA_HEADS = 16
NSA_KV_GROUPS = 4
NSA_HPG = 4
NSA_KV_WIDTH = NSA_KV_GROUPS * NSA_HEAD_DIM
CMP_BLOCK = 32
CMP_STRIDE = 16
CMP_HIDDEN = 256
SLC_BLOCK = 64
SLC_TOPN = 16
SLC_LANES = 128
WINDOW = 512
D_FF = 5632
ROPE_THETA = 10000.0
LN_EPS = 1e-5
RMS_EPS = 1e-6
F_MIN = 1e-30
DN_ALPHA = (2 * DEPTH) ** 0.25
NEG_INF = -1e30
FORCE_SCORE = 1e9

GATE_ROWS = 16
LOG2E = 1.4426950408889634

NSA_TQ = 512
SLC_CHUNK = 512
HGRN_CHUNK = 64
HGRN_SUB = 16
VMEM_LIMIT = 48 * 1024 * 1024
FFN_DOWN_VMEM_LIMIT = 56 * 1024 * 1024


def _params(sem):
    return pltpu.CompilerParams(dimension_semantics=sem, vmem_limit_bytes=VMEM_LIMIT)


def _dot(a, b):
    return jnp.dot(a, b, preferred_element_type=F32)


def _dot_nt(a, b):
    return lax.dot_general(a, b, (((1,), (1,)), ((), ())), preferred_element_type=F32)


def _dot_tn(a, b):
    return lax.dot_general(a, b, (((0,), (0,)), ((), ())), preferred_element_type=F32)


def _split3(x):
    hi = x.astype(BF16)
    r = x - hi.astype(F32)
    mid = r.astype(BF16)
    lo = (r - mid.astype(F32)).astype(BF16)
    return hi, mid, lo


def _sigmoid_pair(z):
    e = jnp.exp(-jnp.abs(z))
    r = 1.0 / (1.0 + e)
    er = e * r
    pos = z >= 0
    return jnp.where(pos, r, er), jnp.where(pos, er, r)


def _silu(x):
    return x * _sigmoid_pair(x)[0]


def _proj_kernel(x_ref, w_ref, o_ref):
    o_ref[...] = _dot(x_ref[...], w_ref[...]).astype(o_ref.dtype)


def _proj(x, w, l, n, out_dtype, tm, tn):
    m, k = x.shape
    return pl.pallas_call(
        _proj_kernel,
        grid=(m // tm, n // tn),
        in_specs=[pl.BlockSpec((tm, k), lambda i, j: (i, 0)),
                  pl.BlockSpec((None, k, tn), lambda i, j: (l, 0, j))],
        out_specs=pl.BlockSpec((tm, tn), lambda i, j: (i, j)),
        out_shape=jax.ShapeDtypeStruct((m, n), out_dtype),
        compiler_params=_params(("parallel", "arbitrary")),
        name="proj",
    )(x, w)


def _proj_rope_kernel(x_ref, w_ref, cos_ref, sin_ref, o_ref, *, tn):
    h = _dot(x_ref[...], w_ref[...])
    cos = cos_ref[0]
    sin = sin_ref[0]
    lane = lax.broadcasted_iota(jnp.int32, cos.shape, 1)
    first_half = (lane % NSA_HEAD_DIM) < (NSA_HEAD_DIM // 2)
    for c in range(tn // 128):
        hc = h[:, c * 128:(c + 1) * 128]
        rot = jnp.where(first_half, pltpu.roll(hc, 96, axis=1), pltpu.roll(hc, 32, axis=1))
        o_ref[:, c * 128:(c + 1) * 128] = (hc * cos + rot * sin).astype(o_ref.dtype)


def _proj_rope(x, w, l, col0, cos_tabs, sin_tabs, tab, out_dtype, tm, tn):
    m, k = x.shape
    tab_spec = pl.BlockSpec((1, tm, 128), lambda i: (tab, i, 0))
    return pl.pallas_call(
        functools.partial(_proj_rope_kernel, tn=tn),
        grid=(m // tm,),
        in_specs=[pl.BlockSpec((tm, k), lambda i: (i, 0)),
                  pl.BlockSpec((None, k, tn), lambda i: (l, 0, col0)),
                  tab_spec, tab_spec],
        out_specs=pl.BlockSpec((tm, tn), lambda i: (i, 0)),
        out_shape=jax.ShapeDtypeStruct((m, tn), out_dtype),
        compiler_params=_params(("parallel",)),
        name="proj_rope",
    )(x, w, cos_tabs, sin_tabs)


def _hgrn_chunk(q, z, v, g, lb, nw, st, tri, sub_row):
    C, c = HGRN_CHUNK, HGRN_SUB
    one_minus_lb = 1.0 - lb
    sig, sig_neg = _sigmoid_pair(z)
    f = lb + one_minus_lb * sig
    lf = jnp.log2(jnp.maximum(f, F_MIN))
    k = one_minus_lb * sig_neg
    qs = _silu(q)
    lf_hi, lf_mid, lf_lo = _split3(lf)
    b = _dot(tri, lf_hi) + _dot(tri, lf_mid) + _dot(tri, lf_lo)
    o_inter = _dot_nt((qs * jnp.exp2(b)).astype(BF16), st.astype(BF16))
    v16 = v.astype(BF16)
    outs = []
    for i in range(C // c):
        lo = i * c
        b_i = b[lo:lo + c]
        qs_i = qs[lo:lo + c]
        k_i = k[lo:lo + c]
        v_i = v[lo:lo + c]
        o_i = o_inter[lo:lo + c]
        if i > 0:
            beta = b[lo - 1:lo]
            qt = (qs_i * jnp.exp2(b_i - beta)).astype(BF16)
            kt = (k[:lo] * jnp.exp2(beta - b[:lo])).astype(BF16)
            att = _dot_nt(qt, kt)
            o_i = o_i + _dot(att.astype(BF16), v16[:lo])
        for s in range(c):
            d = jnp.exp2(b_i - b_i[s:s + 1])
            w = jnp.sum(qs_i * d * k_i[s:s + 1], axis=-1, keepdims=True)
            w = jnp.where(sub_row >= s, w, 0.0)
            o_i = o_i + w * v_i[s:s + 1]
        outs.append(o_i)
    o = jnp.concatenate(outs, axis=0)
    b_last = b[C - 1:C]
    kd = (k * jnp.exp2(b_last - b)).astype(BF16)
    st_new = jnp.exp2(b_last) * st + _dot_tn(v16, kd)
    o = o * lax.rsqrt(jnp.mean(o * o, axis=-1, keepdims=True) + RMS_EPS)
    return o * nw * _silu(g), st_new


def _hgrn_kernel(q_ref, z_ref, v_ref, g_ref, lb_ref, nw_ref, o_ref, st_ref, *, tb, hb):
    C, c = HGRN_CHUNK, HGRN_SUB

    @pl.when(pl.program_id(1) == 0)
    def _():
        st_ref[...] = jnp.zeros_like(st_ref)

    tri = (lax.broadcasted_iota(jnp.int32, (C, C), 1)
           <= lax.broadcasted_iota(jnp.int32, (C, C), 0)).astype(BF16)
    sub_row = lax.broadcasted_iota(jnp.int32, (c, 1), 0)

    def chunk(ci, carry):
        r0 = pl.multiple_of(ci * C, C)
        for h in range(hb):
            lanes = slice(h * HEAD_LANES, (h + 1) * HEAD_LANES)
            out, st_new = _hgrn_chunk(q_ref[pl.ds(r0, C), lanes], z_ref[pl.ds(r0, C), lanes],
                                      v_ref[pl.ds(r0, C), lanes], g_ref[pl.ds(r0, C), lanes],
                                      lb_ref[:, lanes], nw_ref[:, lanes], st_ref[h], tri, sub_row)
            st_ref[h] = st_new
            o_ref[pl.ds(r0, C), lanes] = out.astype(o_ref.dtype)
        return carry

    lax.fori_loop(0, tb // C, chunk, 0)


def _hgrn(hh, lb, nw, tb, hb):
    s = hh.shape[0]
    nhb = HGRN_HEADS // hb
    wide = hb * HEAD_LANES

    def col(off):
        return pl.BlockSpec((tb, wide), lambda h, t: (t, off * nhb + h))

    vec = pl.BlockSpec((1, wide), lambda h, t: (0, h))
    return pl.pallas_call(
        functools.partial(_hgrn_kernel, tb=tb, hb=hb),
        grid=(nhb, s // tb),
        in_specs=[col(0), col(1), col(2), col(3), vec, vec],
        out_specs=pl.BlockSpec((tb, wide), lambda h, t: (t, h)),
        out_shape=jax.ShapeDtypeStruct((s, HGRN_WIDTH), BF16),
        scratch_shapes=[pltpu.VMEM((hb, HEAD_LANES, HEAD_LANES), F32)],
        compiler_params=_params(("parallel", "arbitrary")),
        name="hgrn2",
    )(hh, hh, hh, hh, lb, nw)


def _projT_rope_kernel(w_ref, xT_ref, cos_ref, sin_ref, o_ref, *, tq):
    h = _dot(w_ref[...], xT_ref[...])
    cos = cos_ref[...]
    sin = sin_ref[...]
    half = NSA_HEAD_DIM // 2
    for hp in range(NSA_HPG):
        r = hp * NSA_HEAD_DIM
        x1 = h[r:r + half]
        x2 = h[r + half:r + 2 * half]
        o_ref[0, :half, hp * tq:(hp + 1) * tq] = (x1 * cos - x2 * sin).astype(o_ref.dtype)
        o_ref[0, half:, hp * tq:(hp + 1) * tq] = (x2 * cos + x1 * sin).astype(o_ref.dtype)


def _projT_rope(wT, l, xT, cosT, sinT, tq):
    _, n, d = wT.shape
    s = xT.shape[1]
    half = NSA_HEAD_DIM // 2
    gw = NSA_HPG * NSA_HEAD_DIM
    return pl.pallas_call(
        functools.partial(_projT_rope_kernel, tq=tq),
        grid=(s // tq, n // gw),
        in_specs=[pl.BlockSpec((None, gw, d), lambda i, j: (l, j, 0)),
                  pl.BlockSpec((d, tq), lambda i, j: (0, i)),
                  pl.BlockSpec((half, tq), lambda i, j: (0, i)),
                  pl.BlockSpec((half, tq), lambda i, j: (0, i))],
        out_specs=pl.BlockSpec((1, NSA_HEAD_DIM, NSA_HPG * tq), lambda i, j: (j, 0, i)),
        out_shape=jax.ShapeDtypeStruct((n // gw, NSA_HEAD_DIM, NSA_HPG * s), BF16),
        compiler_params=_params(("parallel", "arbitrary")),
        name="projT_rope",
    )(wT, xT, cosT, sinT)


def _projT_vg_kernel(w_ref, xT_ref, v_ref, g_ref):
    h = _dot(w_ref[...], xT_ref[...])
    nv = v_ref.shape[0]
    v_ref[...] = h[:nv].astype(v_ref.dtype)
    g_ref[...] = h[nv:]


def _projT_vg(wT, l, xT, nv, ts):
    _, n, d = wT.shape
    s = xT.shape[1]
    return pl.pallas_call(
        _projT_vg_kernel,
        grid=(s // ts,),
        in_specs=[pl.BlockSpec((None, n, d), lambda i: (l, 0, 0)),
                  pl.BlockSpec((d, ts), lambda i: (0, i))],
        out_specs=[pl.BlockSpec((nv, ts), lambda i: (0, i)),
                   pl.BlockSpec((n - nv, ts), lambda i: (0, i))],
        out_shape=[jax.ShapeDtypeStruct((nv, s), BF16), jax.ShapeDtypeStruct((n - nv, s), F32)],
        compiler_params=_params(("parallel",)),
        name="projT_vg",
    )(wT, xT)


def _compress_kernel(x_ref, pe_ref, w1_ref, w2_ref, kv_ref, kvT_ref):
    n_blk = x_ref.shape[0] // CMP_STRIDE
    width = 2 * CMP_HIDDEN
    top = jnp.zeros((n_blk, width), F32)
    bot = jnp.zeros((n_blk, width), F32)
    for j in range(CMP_STRIDE):
        rows = x_ref[pl.ds(j, n_blk, stride=CMP_STRIDE), :]
        top = top + _dot((rows + pe_ref[j]).astype(BF16), w1_ref[0, j])
        bot = bot + _dot((rows + pe_ref[CMP_STRIDE + j]).astype(BF16), w1_ref[0, CMP_STRIDE + j])
    hid = top + pltpu.roll(bot, n_blk - 1, axis=0)
    out = _dot(_silu(hid).astype(BF16), w2_ref[0])
    kv_ref[0] = out.astype(kv_ref.dtype)
    kvT_ref[0] = out.T.astype(kvT_ref.dtype)


def _compress(x, pe, w1, w2, l):
    s = x.shape[0]
    g = NSA_KV_GROUPS
    n_blk = s // CMP_STRIDE
    return pl.pallas_call(
        _compress_kernel,
        grid=(g,),
        in_specs=[pl.BlockSpec((s, 128), lambda gi: (0, gi)),
                  pl.BlockSpec(pe.shape, lambda gi: (0, 0, 0)),
                  pl.BlockSpec((1,) + w1.shape[1:], lambda gi: (l, 0, 0, 0)),
                  pl.BlockSpec((1,) + w2.shape[1:], lambda gi: (l, 0, 0))],
        out_specs=[pl.BlockSpec((1, n_blk, 128), lambda gi: (gi, 0, 0)),
                   pl.BlockSpec((1, 128, n_blk), lambda gi: (gi, 0, 0))],
        out_shape=[jax.ShapeDtypeStruct((g, n_blk, 128), BF16),
                   jax.ShapeDtypeStruct((g, 128, n_blk), BF16)],
        compiler_params=_params(("parallel",)),
        name="compress",
    )(x, pe, w1, w2)


ONES_ROWS = 16


def _with_ones_rows(vT):
    return jnp.concatenate([vT, jnp.ones((ONES_ROWS, vT.shape[1]), vT.dtype)], axis=0)


def _per_head(x, tq):
    return [x[:, hp * tq:(hp + 1) * tq] for hp in range(NSA_HPG)]


def _cmp_select_kernel(q_ref, kc_ref, vcT_ref, oc_ref, selT_ref, *, tq):
    kc = kc_ref[0, :, :NSA_HEAD_DIM]
    vcT = vcT_ref[0]
    n_blk = kc.shape[0]
    t = pl.program_id(1) * tq + lax.broadcasted_iota(jnp.int32, (1, tq), 1)
    n_idx = lax.broadcasted_iota(jnp.int32, (n_blk, 1), 0)
    bias = jnp.where((n_idx * CMP_STRIDE + (CMP_BLOCK - 1)) <= t, 0.0, NEG_INF)
    any_visible = (t >= CMP_BLOCK - 1).astype(F32)
    s = _dot(kc, q_ref[0])
    s = jnp.concatenate([x + bias for x in _per_head(s, tq)], axis=1)
    e = jnp.exp2(s - jnp.max(s, axis=0, keepdims=True))
    scale = jnp.concatenate([any_visible] * NSA_HPG, axis=1) / jnp.sum(e, axis=0, keepdims=True)
    p = e * scale
    oc_ref[0] = _dot(vcT, p.astype(BF16))
    p_sum = functools.reduce(lambda a, b: a + b, _per_head(p, tq))
    sj = lax.broadcasted_iota(jnp.int32, (SLC_LANES, n_blk), 0) * SLC_BLOCK
    ci = lax.broadcasted_iota(jnp.int32, (SLC_LANES, n_blk), 1) * CMP_STRIDE
    overlap = ((ci < sj + SLC_BLOCK) & (ci + CMP_BLOCK > sj)).astype(BF16)
    p_hi, p_mid, p_lo = _split3(p_sum)
    imp = _dot(overlap, p_hi) + _dot(overlap, p_mid) + _dot(overlap, p_lo)
    blk = lax.broadcasted_iota(jnp.int32, (SLC_LANES, 1), 0)
    cur = t // SLC_BLOCK
    forced = (blk == 0) | (blk == cur) | (blk == cur - 1)
    causal = blk * SLC_BLOCK <= t
    work = jnp.where(forced, FORCE_SCORE, jnp.where(causal, imp, -1.0))
    blk_f = blk.astype(F32)
    sel = jnp.zeros((SLC_LANES, tq), F32)
    for _ in range(SLC_TOPN):
        mx = jnp.max(work, axis=0, keepdims=True)
        first = jnp.min(jnp.where(work == mx, blk_f, float(SLC_LANES)), axis=0, keepdims=True)
        pick = blk_f == first
        sel = jnp.where(pick & (mx >= 0.0), 1.0, sel)
        work = jnp.where(pick, -3e38, work)
    selT_ref[0] = sel.astype(selT_ref.dtype)


def _cmp_select(qL, k_cmp, v_cmpT, tq):
    g = NSA_KV_GROUPS
    s = qL.shape[2] // NSA_HPG
    n_blk = k_cmp.shape[1]
    q_spec = pl.BlockSpec((1, NSA_HEAD_DIM, NSA_HPG * tq), lambda gi, qi: (gi, 0, qi))
    return pl.pallas_call(
        functools.partial(_cmp_select_kernel, tq=tq),
        grid=(g, s // tq),
        in_specs=[q_spec,
                  pl.BlockSpec((1, n_blk, 2 * NSA_HEAD_DIM), lambda gi, qi: (gi, 0, 0)),
                  pl.BlockSpec((1, NSA_HEAD_DIM, n_blk), lambda gi, qi: (gi, 1, 0))],
        out_specs=[q_spec,
                   pl.BlockSpec((1, SLC_LANES, tq), lambda gi, qi: (gi, 0, qi))],
        out_shape=[jax.ShapeDtypeStruct(qL.shape, F32),
                   jax.ShapeDtypeStruct((g, SLC_LANES, s), BF16)],
        compiler_params=_params(("parallel", "arbitrary")),
        name="cmp_select",
    )(qL, k_cmp, v_cmpT)


def _win_attn_kernel(q_ref, k0_ref, k1_ref, v0_ref, v1_ref, ow_ref, *, tq):
    dk = NSA_HEAD_DIM
    qi = pl.program_id(1)
    q = q_ref[0]
    t = qi * tq + lax.broadcasted_iota(jnp.int32, (1, tq), 1)
    s = []
    for c, k_ref in enumerate((k0_ref, k1_ref)):
        j = (qi - 1 + c) * tq + lax.broadcasted_iota(jnp.int32, (tq, 1), 0)
        bias = jnp.where((j >= 0) & (j <= t) & (t - j < WINDOW), 0.0, NEG_INF)
        sc = _dot(k_ref[:, dk:], q)
        s.append(jnp.concatenate([x + bias for x in _per_head(sc, tq)], axis=1))
    m = jnp.maximum(jnp.max(s[0], axis=0, keepdims=True), jnp.max(s[1], axis=0, keepdims=True))
    o = (_dot(_with_ones_rows(v0_ref[...]), jnp.exp2(s[0] - m).astype(BF16))
         + _dot(_with_ones_rows(v1_ref[...]), jnp.exp2(s[1] - m).astype(BF16)))
    ow_ref[0] = o[:dk] / o[dk:dk + 1]


def _win_attn(qL, kr, vT, tq):
    g = NSA_KV_GROUPS
    s = qL.shape[2] // NSA_HPG
    assert WINDOW == tq
    q_spec = pl.BlockSpec((1, NSA_HEAD_DIM, NSA_HPG * tq), lambda gi, qi: (gi, 0, qi))

    def k_spec(c):
        return pl.BlockSpec((tq, 128), lambda gi, qi: (jnp.maximum(qi - 1 + c, 0), gi))

    def v_spec(c):
        return pl.BlockSpec((NSA_HEAD_DIM, tq), lambda gi, qi: (2 * gi + 1, jnp.maximum(qi - 1 + c, 0)))

    return pl.pallas_call(
        functools.partial(_win_attn_kernel, tq=tq),
        grid=(g, s // tq),
        in_specs=[q_spec, k_spec(0), k_spec(1), v_spec(0), v_spec(1)],
        out_specs=q_spec,
        out_shape=jax.ShapeDtypeStruct(qL.shape, F32),
        compiler_params=_params(("parallel", "arbitrary")),
        name="win_attn",
    )(qL, kr, kr, vT, vT)


def _slc_attn_kernel(qi_tab, ki_tab, q_ref, k_ref, vT_ref, selT_ref, oc_ref, ow_ref, gT_ref,
                     o_ref, rhs_ref, m_ref, acc_ref, *, tq):
    dk = NSA_HEAD_DIM
    step = pl.program_id(1)
    qi = qi_tab[step]
    ki = ki_tab[step]

    @pl.when(ki == 0)
    def _():
        m_ref[...] = jnp.full_like(m_ref, NEG_INF)
        acc_ref[...] = jnp.zeros_like(acc_ref)
        not_selected = ((selT_ref[0].astype(F32) - 1.0) * (-NEG_INF)).astype(BF16)
        rhs_ref[:SLC_LANES, :] = jnp.concatenate([not_selected] * NSA_HPG, axis=1)
        rhs_ref[SLC_LANES:SLC_LANES + dk, :] = q_ref[0]
        rhs_ref[SLC_LANES + dk:, :] = jnp.zeros((dk, NSA_HPG * tq), BF16)

    blk_of_key = ki * (tq // SLC_BLOCK) + lax.broadcasted_iota(jnp.int32, (tq, SLC_LANES), 0) // SLC_BLOCK
    one_hot = (lax.broadcasted_iota(jnp.int32, (tq, SLC_LANES), 1) == blk_of_key).astype(BF16)
    lhs = jnp.concatenate([one_hot, k_ref[...]], axis=1)

    def update(causal):
        v_aug = _with_ones_rows(vT_ref[...])
        cw = SLC_CHUNK
        n_chunks = NSA_HPG * tq // cw
        scores = lambda c: _dot(lhs, rhs_ref[:, c * cw:(c + 1) * cw])
        s_next = scores(0)
        for c in range(n_chunks):
            lanes = slice(c * cw, (c + 1) * cw)
            s = s_next
            if c + 1 < n_chunks:
                s_next = scores(c + 1)
            if causal:
                pos = (c * cw) % tq + lax.broadcasted_iota(jnp.int32, (1, cw), 1)
                s = jnp.where(lax.broadcasted_iota(jnp.int32, (tq, 1), 0) <= pos, s, NEG_INF)
            m_old = m_ref[:, lanes]
            m_new = jnp.maximum(m_old, jnp.max(s, axis=0, keepdims=True))
            p = jnp.exp2(s - m_new).astype(BF16)
            acc_ref[:, lanes] = jnp.exp2(m_old - m_new) * acc_ref[:, lanes] + _dot(v_aug, p)
            m_ref[:, lanes] = m_new

    @pl.when(ki < qi)
    def _():
        update(False)

    @pl.when(ki == qi)
    def _():
        update(True)
        sig = _sigmoid_pair(gT_ref[...])[0]

        def gate(br):
            return jnp.concatenate([sig[br * NSA_HPG + hp:br * NSA_HPG + hp + 1] for hp in range(NSA_HPG)], axis=1)

        o_s = acc_ref[:dk, :] / acc_ref[dk:dk + 1, :]
        out = gate(0) * oc_ref[0] + gate(1) * o_s + gate(2) * ow_ref[0]
        o_ref[...] = jnp.concatenate(_per_head(out, tq), axis=0).T.astype(o_ref.dtype)


def _slc_attn(qL, kr, vT, selT, ocL, owL, gT, tq):
    g = NSA_KV_GROUPS
    s = qL.shape[2] // NSA_HPG
    tk = tq
    pairs = [(qi, ki) for qi in range(s // tq) for ki in range(qi + 1)]
    qi_tab = jnp.asarray(np.array([p[0] for p in pairs], np.int32))
    ki_tab = jnp.asarray(np.array([p[1] for p in pairs], np.int32))
    q_spec = pl.BlockSpec((1, NSA_HEAD_DIM, NSA_HPG * tq), lambda gi, st, qt, kt: (gi, 0, qt[st]))
    grid_spec = pltpu.PrefetchScalarGridSpec(
        num_scalar_prefetch=2,
        grid=(g, len(pairs)),
        in_specs=[q_spec,
                  pl.BlockSpec((tk, 128), lambda gi, st, qt, kt: (kt[st], gi)),
                  pl.BlockSpec((NSA_HEAD_DIM, tk), lambda gi, st, qt, kt: (2 * gi, kt[st])),
                  pl.BlockSpec((1, SLC_LANES, tq), lambda gi, st, qt, kt: (gi, 0, qt[st])),
                  q_spec, q_spec,
                  pl.BlockSpec((GATE_ROWS, tq), lambda gi, st, qt, kt: (gi, qt[st]))],
        out_specs=pl.BlockSpec((tq, NSA_HPG * NSA_HEAD_DIM), lambda gi, st, qt, kt: (qt[st], gi)),
        scratch_shapes=[pltpu.VMEM((SLC_LANES + 2 * NSA_HEAD_DIM, NSA_HPG * tq), BF16),
                        pltpu.VMEM((1, NSA_HPG * tq), F32),
                        pltpu.VMEM((NSA_HEAD_DIM + ONES_ROWS, NSA_HPG * tq), F32)])
    return pl.pallas_call(
        functools.partial(_slc_attn_kernel, tq=tq),
        grid_spec=grid_spec,
        out_shape=jax.ShapeDtypeStruct((s, NSA_WIDTH), BF16),
        compiler_params=_params(("parallel", "arbitrary")),
        name="slc_attn",
    )(qi_tab, ki_tab, qL, kr, vT, selT, ocL, owL, gT)


def _res_ln_epilogue(acc, x_ref, g_ref, b_ref, xo_ref, xb_ref):
    v = DN_ALPHA * x_ref[...] + acc
    mu = jnp.mean(v, axis=-1, keepdims=True)
    d = v - mu
    var = jnp.mean(d * d, axis=-1, keepdims=True)
    y = d * lax.rsqrt(var + LN_EPS) * g_ref[...] + b_ref[...]
    xo_ref[...] = y
    xb_ref[...] = y.astype(BF16)


def _mix_out_kernel(a1_ref, a2_ref, w_ref, x_ref, g_ref, b_ref, xo_ref, xb_ref):
    k1 = a1_ref.shape[1]
    y = _dot(a1_ref[...], w_ref[:k1, :]) + _dot(a2_ref[...], w_ref[k1:, :])
    _res_ln_epilogue(y, x_ref, g_ref, b_ref, xo_ref, xb_ref)


def _mix_out(a1, a2, w, l, x, gamma, beta, tm):
    s, k1 = a1.shape
    n = w.shape[2]
    row = pl.BlockSpec((tm, n), lambda i: (i, 0))
    vec = pl.BlockSpec((1, n), lambda i: (0, 0))
    return pl.pallas_call(
        _mix_out_kernel,
        grid=(s // tm,),
        in_specs=[pl.BlockSpec((tm, k1), lambda i: (i, 0)),
                  pl.BlockSpec((tm, k1), lambda i: (i, 0)),
                  pl.BlockSpec((None,) + w.shape[1:], lambda i: (l, 0, 0)),
                  row, vec, vec],
        out_specs=[row, row],
        out_shape=[jax.ShapeDtypeStruct((s, n), F32), jax.ShapeDtypeStruct((s, n), BF16)],
        compiler_params=_params(("parallel",)),
        name="mix_out",
    )(a1, a2, w, x, gamma, beta)


def _ffn_down_kernel(a_ref, w_ref, x_ref, g_ref, b_ref, xo_ref, xb_ref, acc_ref):
    k = pl.program_id(1)

    @pl.when(k == 0)
    def _():
        acc_ref[...] = _dot(a_ref[...], w_ref[...])

    @pl.when(k > 0)
    def _():
        acc_ref[...] += _dot(a_ref[...], w_ref[...])

    @pl.when(k == pl.num_programs(1) - 1)
    def _():
        _res_ln_epilogue(acc_ref[...], x_ref, g_ref, b_ref, xo_ref, xb_ref)


def _ffn_down(a, w, l, x, gamma, beta, tm, tk):
    s, kk = a.shape
    n = w.shape[2]
    row = pl.BlockSpec((tm, n), lambda i, k: (i, 0))
    vec = pl.BlockSpec((1, n), lambda i, k: (0, 0))
    res = pl.BlockSpec((tm, n), lambda i, k: (i, 0), pipeline_mode=pl.Buffered(1))
    return pl.pallas_call(
        _ffn_down_kernel,
        grid=(s // tm, kk // tk),
        in_specs=[pl.BlockSpec((tm, tk), lambda i, k: (i, k)),
                  pl.BlockSpec((None, tk, n), lambda i, k: (l, k, 0)),
                  res, vec, vec],
        out_specs=[row, row],
        out_shape=[jax.ShapeDtypeStruct((s, n), F32), jax.ShapeDtypeStruct((s, n), BF16)],
        scratch_shapes=[pltpu.VMEM((tm, n), F32)],
        compiler_params=pltpu.CompilerParams(dimension_semantics=("parallel", "arbitrary"),
                                             vmem_limit_bytes=FFN_DOWN_VMEM_LIMIT),
        name="ffn_down",
    )(a, w, x, gamma, beta)


HALO = 16


def _ffn_up_kernel(a_ref, ap_ref, wg_ref, wu_ref, cwg_ref, cwu_ref, cbg_ref, cbu_ref, o_ref, wgb_ref, wub_ref):
    i = pl.program_id(1)

    @pl.when(i == 0)
    def _():
        wgb_ref[...] = wg_ref[...].astype(BF16)
        wub_ref[...] = wu_ref[...].astype(BF16)

    ap = ap_ref[...]
    ap = jnp.where(i > 0, ap, jnp.zeros_like(ap))
    a = jnp.concatenate([ap, a_ref[...]], axis=0)

    def branch(w_ref, cw_ref, cb_ref):
        h = _dot(a, w_ref[...])
        cw = cw_ref[...]
        return (cw[2:3] * h[HALO:] + cw[1:2] * h[HALO - 1:-1] + cw[0:1] * h[HALO - 2:-2]) + cb_ref[...]

    gate = branch(wgb_ref, cwg_ref, cbg_ref)
    up = branch(wub_ref, cwu_ref, cbu_ref)
    o_ref[...] = (_silu(gate) * up).astype(o_ref.dtype)


def _ffn_up(a, w, l, conv_w, conv_b, tm, tn):
    s, d = a.shape
    nj = D_FF // tn
    return pl.pallas_call(
        _ffn_up_kernel,
        grid=(nj, s // tm),
        in_specs=[pl.BlockSpec((tm, d), lambda j, i: (i, 0)),
                  pl.BlockSpec((HALO, d), lambda j, i: (jnp.maximum(i * (tm // HALO) - 1, 0), 0)),
                  pl.BlockSpec((None, d, tn), lambda j, i: (l, 0, j)),
                  pl.BlockSpec((None, d, tn), lambda j, i: (l, 0, j + nj)),
                  pl.BlockSpec((3, tn), lambda j, i: (0, j)),
                  pl.BlockSpec((3, tn), lambda j, i: (0, j + nj)),
                  pl.BlockSpec((1, tn), lambda j, i: (0, j)),
                  pl.BlockSpec((1, tn), lambda j, i: (0, j + nj))],
        out_specs=pl.BlockSpec((tm, tn), lambda j, i: (i, j)),
        out_shape=jax.ShapeDtypeStruct((s, D_FF), BF16),
        scratch_shapes=[pltpu.VMEM((d, tn), BF16), pltpu.VMEM((d, tn), BF16)],
        compiler_params=_params(("parallel", "arbitrary")),
        name="ffn_up",
    )(a, a, w, w, conv_w, conv_w, conv_b, conv_b)


def _rope_tables(s):
    half = NSA_HEAD_DIM // 2
    inv = ROPE_THETA ** (-jnp.arange(half, dtype=F32) / half)
    ang = jnp.arange(s).astype(F32)[:, None] * inv[None, :]
    cos, sin = jnp.cos(ang), jnp.sin(ang)
    cos64 = jnp.concatenate([cos, cos], -1)
    sin64 = jnp.concatenate([-sin, sin], -1)
    ones, zeros = jnp.ones_like(cos64), jnp.zeros_like(sin64)
    cos_tabs = jnp.stack([jnp.concatenate([cos64, ones], -1), jnp.concatenate([cos64, cos64], -1)])
    sin_tabs = jnp.stack([jnp.concatenate([sin64, zeros], -1), jnp.concatenate([sin64, sin64], -1)])
    q_scale = NSA_HEAD_DIM ** -0.5 * LOG2E
    return cos_tabs, sin_tabs, cos.T * q_scale, sin.T * q_scale


def _nsa_weight_layout(w_nsa):
    nl, d, _ = w_nsa.shape
    g, dk = NSA_KV_GROUPS, NSA_HEAD_DIM
    q = w_nsa[:, :, :NSA_WIDTH]
    kv = w_nsa[:, :, NSA_WIDTH:NSA_WIDTH + 6 * NSA_KV_WIDTH].reshape(nl, d, 6, g, dk)
    kc, vc, ks, vs, kw, vw = (kv[:, :, i] for i in range(6))
    w_rows = jnp.concatenate([jnp.stack([kc, vc], 3).reshape(nl, d, 2 * g * dk),
                              jnp.stack([ks, kw], 3).reshape(nl, d, 2 * g * dk)], axis=2)
    w_v = jnp.stack([vs, vw], 3).reshape(nl, d, 2 * g * dk)
    gt = w_nsa[:, :, NSA_WIDTH + 6 * NSA_KV_WIDTH:].reshape(nl, d, g, NSA_HPG, 3).transpose(0, 1, 2, 4, 3)
    gt = jnp.pad(gt.reshape(nl, d, g, 3 * NSA_HPG), ((0, 0), (0, 0), (0, 0), (0, GATE_ROWS - 3 * NSA_HPG)))
    w_vg = jnp.concatenate([w_v, gt.reshape(nl, d, g * GATE_ROWS)], axis=2)
    return w_rows.astype(BF16), q.transpose(0, 2, 1).astype(BF16), w_vg.transpose(0, 2, 1).astype(BF16)


def _block_diag(a, b):
    top = jnp.concatenate([a, jnp.zeros(a.shape[:-1] + b.shape[-1:], a.dtype)], axis=-1)
    bot = jnp.concatenate([jnp.zeros(b.shape[:-1] + a.shape[-1:], b.dtype), b], axis=-1)
    return jnp.concatenate([top, bot], axis=-2)


def _compress_weight_layout(w1_k, w1_v, w2_k, w2_v):
    nl = w1_k.shape[0]
    per_token = lambda w: w.reshape(nl, CMP_BLOCK, NSA_HEAD_DIM, CMP_HIDDEN)
    return (_block_diag(per_token(w1_k), per_token(w1_v)).astype(BF16),
            _block_diag(w2_k, w2_v).astype(BF16))


def kernel(x, w_in, w_out, hgrn_lb_logits, hgrn_norm_w, cmp_pe_k, cmp_pe_v, cmp_w1_k, cmp_w2_k,
           cmp_w1_v, cmp_w2_v, ln1_g, ln1_b, w_up, conv_w, conv_b, w_down, ln2_g, ln2_b):
    s = x.shape[1]
    g = NSA_KV_GROUPS
    p_lb = jax.nn.softmax(hgrn_lb_logits.astype(F32), axis=0)
    lower_bounds = jnp.cumsum(p_lb, axis=0) - p_lb[0:1]
    cos_tabs, sin_tabs, cosT, sinT = _rope_tables(s)
    hw = 4 * HGRN_WIDTH
    w_in_b, w_out_b, w_down_b = (w.astype(BF16) for w in (w_in, w_out, w_down))
    w_rows, w_qT, w_vgT = _nsa_weight_layout(w_in[:, :, hw:])
    cmp_w1, cmp_w2 = _compress_weight_layout(cmp_w1_k, cmp_w1_v, cmp_w2_k, cmp_w2_v)
    xf = x[0]
    xb = xf.astype(BF16)
    for l in range(DEPTH):
        xT = xb.T
        hh = _proj(xb, w_in_b, l, hw, F32, 1024, 512)
        kvc = _proj_rope(xb, w_rows, l, 0, cos_tabs, sin_tabs, 0, F32, 1024, 512)
        kr = _proj_rope(xb, w_rows, l, 1, cos_tabs, sin_tabs, 1, BF16, 1024, 512)
        qL = _projT_rope(w_qT, l, xT, cosT, sinT, NSA_TQ)
        vT, gT = _projT_vg(w_vgT, l, xT, 2 * g * NSA_HEAD_DIM, 1024)
        o_h = _hgrn(hh, lower_bounds[l][None], hgrn_norm_w[l][None], 512, 8)

        pe = jnp.concatenate([cmp_pe_k[l], cmp_pe_v[l]], axis=-1)[:, None, :]
        kv_cmp, kv_cmpT = _compress(kvc, pe, cmp_w1, cmp_w2, l)
        ocL, selT = _cmp_select(qL, kv_cmp, kv_cmpT, NSA_TQ)
        owL = _win_attn(qL, kr, vT, NSA_TQ)
        o_n = _slc_attn(qL, kr, vT, selT, ocL, owL, gT, NSA_TQ)

        xf, xb = _mix_out(o_h, o_n, w_out_b, l, xf, ln1_g[l][None], ln1_b[l][None], 512)
        u = _ffn_up(xb, w_up, l, conv_w[l], conv_b[l][None], 1024, 512)
        xf, xb = _ffn_down(u, w_down_b, l, xf, ln2_g[l][None], ln2_b[l][None], 1024, 512)
    return xf[None]
```

```python
import functools

import numpy as np
import jax
import jax.numpy as jnp
from jax import lax
from jax.experimental import pallas as pl
from jax.experimental.pallas import tpu as pltpu

F32 = jnp.float32
BF16 = jnp.bfloat16

D_MODEL = 2048
DEPTH = 4
HGRN_WIDTH = 1024
HGRN_HEADS = 8
HEAD_LANES = 128
NSA_WIDTH = 1024
NSA_HEAD_DIM = 64
NSA_HEADS = 16
NSA_KV_GROUPS = 4
NSA_HPG = 4
NSA_KV_WIDTH = NSA_KV_GROUPS * NSA_HEAD_DIM
CMP_BLOCK = 32
CMP_STRIDE = 16
CMP_HIDDEN = 256
SLC_BLOCK = 64
SLC_TOPN = 16
SLC_LANES = 128
WINDOW = 512
D_FF = 5632
ROPE_THETA = 10000.0
LN_EPS = 1e-5
RMS_EPS = 1e-6
F_MIN = 1e-30
DN_ALPHA = (2 * DEPTH) ** 0.25
NEG_INF = -1e30
FORCE_SCORE = 1e9

GATE_ROWS = 16
LOG2E = 1.4426950408889634

NSA_TQ = 512
SLC_CHUNK = 512
HGRN_CHUNK = 64
HGRN_SUB = 16
VMEM_LIMIT = 48 * 1024 * 1024
FFN_DOWN_VMEM_LIMIT = 56 * 1024 * 1024


def _params(sem):
    return pltpu.CompilerParams(dimension_semantics=sem, vmem_limit_bytes=VMEM_LIMIT)


def _dot(a, b):
    return jnp.dot(a, b, preferred_element_type=F32)


def _dot_nt(a, b):
    return lax.dot_general(a, b, (((1,), (1,)), ((), ())), preferred_element_type=F32)


def _dot_tn(a, b):
    return lax.dot_general(a, b, (((0,), (0,)), ((), ())), preferred_element_type=F32)


def _split3(x):
    hi = x.astype(BF16)
    r = x - hi.astype(F32)
    mid = r.astype(BF16)
    lo = (r - mid.astype(F32)).astype(BF16)
    return hi, mid, lo


def _sigmoid_pair(z):
    e = jnp.exp(-jnp.abs(z))
    r = 1.0 / (1.0 + e)
    er = e * r
    pos = z >= 0
    return jnp.where(pos, r, er), jnp.where(pos, er, r)


def _silu(x):
    return x * _sigmoid_pair(x)[0]


def _proj_kernel(x_ref, w_ref, o_ref):
    o_ref[...] = _dot(x_ref[...], w_ref[...]).astype(o_ref.dtype)


def _proj(x, w, l, n, out_dtype, tm, tn):
    m, k = x.shape
    return pl.pallas_call(
        _proj_kernel,
        grid=(m // tm, n // tn),
        in_specs=[pl.BlockSpec((tm, k), lambda i, j: (i, 0)),
                  pl.BlockSpec((None, k, tn), lambda i, j: (l, 0, j))],
        out_specs=pl.BlockSpec((tm, tn), lambda i, j: (i, j)),
        out_shape=jax.ShapeDtypeStruct((m, n), out_dtype),
        compiler_params=_params(("parallel", "arbitrary")),
        name="proj",
    )(x, w)


def _proj_rope_kernel(x_ref, w_ref, cos_ref, sin_ref, o_ref, *, tn, block_one_hot):
    h = _dot(x_ref[...], w_ref[...])
    cos = cos_ref[0]
    sin = sin_ref[0]
    tm = cos.shape[0]
    lane = lax.broadcasted_iota(jnp.int32, cos.shape, 1)
    first_half = (lane % NSA_HEAD_DIM) < (NSA_HEAD_DIM // 2)
    if block_one_hot:
        pos = pl.program_id(0) * tm + lax.broadcasted_iota(jnp.int32, cos.shape, 0)
        one_hot = (lane == pos // SLC_BLOCK).astype(o_ref.dtype)
    for c in range(tn // 128):
        hc = h[:, c * 128:(c + 1) * 128]
        rot = jnp.where(first_half, pltpu.roll(hc, 96, axis=1), pltpu.roll(hc, 32, axis=1))
        roped = (hc * cos + rot * sin).astype(o_ref.dtype)
        if block_one_hot:
            o_ref[:, 2 * c * 128:(2 * c + 1) * 128] = one_hot
            o_ref[:, (2 * c + 1) * 128:(2 * c + 2) * 128] = roped
        else:
            o_ref[:, c * 128:(c + 1) * 128] = roped


def _proj_rope(x, w, l, col0, cos_tabs, sin_tabs, tab, out_dtype, tm, tn, block_one_hot=False):
    m, k = x.shape
    n_out = 2 * tn if block_one_hot else tn
    tab_spec = pl.BlockSpec((1, tm, 128), lambda i: (tab, i, 0))
    return pl.pallas_call(
        functools.partial(_proj_rope_kernel, tn=tn, block_one_hot=block_one_hot),
        grid=(m // tm,),
        in_specs=[pl.BlockSpec((tm, k), lambda i: (i, 0)),
                  pl.BlockSpec((None, k, tn), lambda i: (l, 0, col0)),
                  tab_spec, tab_spec],
        out_specs=pl.BlockSpec((tm, n_out), lambda i: (i, 0)),
        out_shape=jax.ShapeDtypeStruct((m, n_out), out_dtype),
        compiler_params=_params(("parallel",)),
        name="proj_rope",
    )(x, w, cos_tabs, sin_tabs)


def _hgrn_chunk(q, z, v, g, lb, nw, st, tri, sub_row):
    C, c = HGRN_CHUNK, HGRN_SUB
    one_minus_lb = 1.0 - lb
    sig, sig_neg = _sigmoid_pair(z)
    f = lb + one_minus_lb * sig
    lf = jnp.log2(jnp.maximum(f, F_MIN))
    k = one_minus_lb * sig_neg
    qs = _silu(q)
    lf_hi, lf_mid, lf_lo = _split3(lf)
    b = _dot(tri, lf_hi) + _dot(tri, lf_mid) + _dot(tri, lf_lo)
    o_inter = _dot_nt((qs * jnp.exp2(b)).astype(BF16), st.astype(BF16))
    v16 = v.astype(BF16)
    outs = []
    for i in range(C // c):
        lo = i * c
        b_i = b[lo:lo + c]
        qs_i = qs[lo:lo + c]
        k_i = k[lo:lo + c]
        v_i = v[lo:lo + c]
        o_i = o_inter[lo:lo + c]
        if i > 0:
            beta = b[lo - 1:lo]
            qt = (qs_i * jnp.exp2(b_i - beta)).astype(BF16)
            kt = (k[:lo] * jnp.exp2(beta - b[:lo])).astype(BF16)
            att = _dot_nt(qt, kt)
            o_i = o_i + _dot(att.astype(BF16), v16[:lo])
        for s in range(c):
            d = jnp.exp2(b_i - b_i[s:s + 1])
            w = jnp.sum(qs_i * d * k_i[s:s + 1], axis=-1, keepdims=True)
            w = jnp.where(sub_row >= s, w, 0.0)
            o_i = o_i + w * v_i[s:s + 1]
        outs.append(o_i)
    o = jnp.concatenate(outs, axis=0)
    b_last = b[C - 1:C]
    kd = (k * jnp.exp2(b_last - b)).astype(BF16)
    st_new = jnp.exp2(b_last) * st + _dot_tn(v16, kd)
    o = o * lax.rsqrt(jnp.mean(o * o, axis=-1, keepdims=True) + RMS_EPS)
    return o * nw * _silu(g), st_new


def _hgrn_kernel(q_ref, z_ref, v_ref, g_ref, lb_ref, nw_ref, o_ref, st_ref, *, tb, hb):
    C, c = HGRN_CHUNK, HGRN_SUB

    @pl.when(pl.program_id(1) == 0)
    def _():
        st_ref[...] = jnp.zeros_like(st_ref)

    tri = (lax.broadcasted_iota(jnp.int32, (C, C), 1)
           <= lax.broadcasted_iota(jnp.int32, (C, C), 0)).astype(BF16)
    sub_row = lax.broadcasted_iota(jnp.int32, (c, 1), 0)

    def chunk(ci, carry):
        r0 = pl.multiple_of(ci * C, C)
        for h in range(hb):
            lanes = slice(h * HEAD_LANES, (h + 1) * HEAD_LANES)
            out, st_new = _hgrn_chunk(q_ref[pl.ds(r0, C), lanes], z_ref[pl.ds(r0, C), lanes],
                                      v_ref[pl.ds(r0, C), lanes], g_ref[pl.ds(r0, C), lanes],
                                      lb_ref[:, lanes], nw_ref[:, lanes], st_ref[h], tri, sub_row)
            st_ref[h] = st_new
            o_ref[pl.ds(r0, C), lanes] = out.astype(o_ref.dtype)
        return carry

    lax.fori_loop(0, tb // C, chunk, 0)


def _hgrn(hh, lb, nw, tb, hb):
    s = hh.shape[0]
    nhb = HGRN_HEADS // hb
    wide = hb * HEAD_LANES

    def col(off):
        return pl.BlockSpec((tb, wide), lambda h, t: (t, off * nhb + h))

    vec = pl.BlockSpec((1, wide), lambda h, t: (0, h))
    return pl.pallas_call(
        functools.partial(_hgrn_kernel, tb=tb, hb=hb),
        grid=(nhb, s // tb),
        in_specs=[col(0), col(1), col(2), col(3), vec, vec],
        out_specs=pl.BlockSpec((tb, wide), lambda h, t: (t, h)),
        out_shape=jax.ShapeDtypeStruct((s, HGRN_WIDTH), BF16),
        scratch_shapes=[pltpu.VMEM((hb, HEAD_LANES, HEAD_LANES), F32)],
        compiler_params=_params(("parallel", "arbitrary")),
        name="hgrn2",
    )(hh, hh, hh, hh, lb, nw)


def _projT_rope_kernel(w_ref, xT_ref, cos_ref, sin_ref, o_ref, *, tq):
    h = _dot(w_ref[...], xT_ref[...])
    cos = cos_ref[...]
    sin = sin_ref[...]
    half = NSA_HEAD_DIM // 2
    for hp in range(NSA_HPG):
        r = hp * NSA_HEAD_DIM
        x1 = h[r:r + half]
        x2 = h[r + half:r + 2 * half]
        o_ref[0, :half, hp * tq:(hp + 1) * tq] = (x1 * cos - x2 * sin).astype(o_ref.dtype)
        o_ref[0, half:, hp * tq:(hp + 1) * tq] = (x2 * cos + x1 * sin).astype(o_ref.dtype)


def _projT_rope(wT, l, xT, cosT, sinT, tq):
    _, n, d = wT.shape
    s = xT.shape[1]
    half = NSA_HEAD_DIM // 2
    gw = NSA_HPG * NSA_HEAD_DIM
    return pl.pallas_call(
        functools.partial(_projT_rope_kernel, tq=tq),
        grid=(s // tq, n // gw),
        in_specs=[pl.BlockSpec((None, gw, d), lambda i, j: (l, j, 0)),
                  pl.BlockSpec((d, tq), lambda i, j: (0, i)),
                  pl.BlockSpec((half, tq), lambda i, j: (0, i)),
                  pl.BlockSpec((half, tq), lambda i, j: (0, i))],
        out_specs=pl.BlockSpec((1, NSA_HEAD_DIM, NSA_HPG * tq), lambda i, j: (j, 0, i)),
        out_shape=jax.ShapeDtypeStruct((n // gw, NSA_HEAD_DIM, NSA_HPG * s), BF16),
        compiler_params=_params(("parallel", "arbitrary")),
        name="projT_rope",
    )(wT, xT, cosT, sinT)


def _projT_vg_kernel(w_ref, xT_ref, v_ref, g_ref):
    h = _dot(w_ref[...], xT_ref[...])
    nv = v_ref.shape[0]
    v_ref[...] = h[:nv].astype(v_ref.dtype)
    g_ref[...] = h[nv:]


def _projT_vg(wT, l, xT, nv, ts):
    _, n, d = wT.shape
    s = xT.shape[1]
    return pl.pallas_call(
        _projT_vg_kernel,
        grid=(s // ts,),
        in_specs=[pl.BlockSpec((None, n, d), lambda i: (l, 0, 0)),
                  pl.BlockSpec((d, ts), lambda i: (0, i))],
        out_specs=[pl.BlockSpec((nv, ts), lambda i: (0, i)),
                   pl.BlockSpec((n - nv, ts), lambda i: (0, i))],
        out_shape=[jax.ShapeDtypeStruct((nv, s), BF16), jax.ShapeDtypeStruct((n - nv, s), F32)],
        compiler_params=_params(("parallel",)),
        name="projT_vg",
    )(wT, xT)


def _compress_kernel(x_ref, pe_ref, w1_ref, w2_ref, kv_ref, kvT_ref):
    n_blk = x_ref.shape[0] // CMP_STRIDE
    width = 2 * CMP_HIDDEN
    top = jnp.zeros((n_blk, width), F32)
    bot = jnp.zeros((n_blk, width), F32)
    for j in range(CMP_STRIDE):
        rows = x_ref[pl.ds(j, n_blk, stride=CMP_STRIDE), :]
        top = top + _dot((rows + pe_ref[j]).astype(BF16), w1_ref[0, j])
        bot = bot + _dot((rows + pe_ref[CMP_STRIDE + j]).astype(BF16), w1_ref[0, CMP_STRIDE + j])
    hid = top + pltpu.roll(bot, n_blk - 1, axis=0)
    out = _dot(_silu(hid).astype(BF16), w2_ref[0])
    kv_ref[0] = out.astype(kv_ref.dtype)
    kvT_ref[0] = out.T.astype(kvT_ref.dtype)


def _compress(x, pe, w1, w2, l):
    s = x.shape[0]
    g = NSA_KV_GROUPS
    n_blk = s // CMP_STRIDE
    return pl.pallas_call(
        _compress_kernel,
        grid=(g,),
        in_specs=[pl.BlockSpec((s, 128), lambda gi: (0, gi)),
                  pl.BlockSpec(pe.shape, lambda gi: (0, 0, 0)),
                  pl.BlockSpec((1,) + w1.shape[1:], lambda gi: (l, 0, 0, 0)),
                  pl.BlockSpec((1,) + w2.shape[1:], lambda gi: (l, 0, 0))],
        out_specs=[pl.BlockSpec((1, n_blk, 128), lambda gi: (gi, 0, 0)),
                   pl.BlockSpec((1, 128, n_blk), lambda gi: (gi, 0, 0))],
        out_shape=[jax.ShapeDtypeStruct((g, n_blk, 128), BF16),
                   jax.ShapeDtypeStruct((g, 128, n_blk), BF16)],
        compiler_params=_params(("parallel",)),
        name="compress",
    )(x, pe, w1, w2)


ONES_ROWS = 16


def _with_ones_rows(vT):
    return jnp.concatenate([vT, jnp.ones((ONES_ROWS, vT.shape[1]), vT.dtype)], axis=0)


def _per_head(x, tq):
    return [x[:, hp * tq:(hp + 1) * tq] for hp in range(NSA_HPG)]


CMP_ROW_STEP = 128
PICKED = -3e38


def _cmp_select_kernel(q_ref, kc_ref, vcT_ref, oc_ref, selT_ref, imp_ref, *, tq):
    n_blk = kc_ref.shape[1]
    qi = pl.program_id(1)
    t = qi * tq + lax.broadcasted_iota(jnp.int32, (1, tq), 1)
    any_visible = (t >= CMP_BLOCK - 1).astype(F32)

    def attend(rows):
        kc = kc_ref[0, :rows, :NSA_HEAD_DIM]
        vcT = vcT_ref[0, :, :rows]
        n_idx = lax.broadcasted_iota(jnp.int32, (rows, 1), 0)
        bias = jnp.where((n_idx * CMP_STRIDE + (CMP_BLOCK - 1)) <= t, 0.0, NEG_INF)
        s = _dot(kc, q_ref[0])
        s = jnp.concatenate([x + bias for x in _per_head(s, tq)], axis=1)
        e = jnp.exp2(s - jnp.max(s, axis=0, keepdims=True))
        scale = jnp.concatenate([any_visible] * NSA_HPG, axis=1) / jnp.sum(e, axis=0, keepdims=True)
        p = e * scale
        oc_ref[0] = _dot(vcT, p.astype(BF16))
        p_sum = functools.reduce(lambda a, b: a + b, _per_head(p, tq))
        sj = lax.broadcasted_iota(jnp.int32, (SLC_LANES, rows), 0) * SLC_BLOCK
        ci = lax.broadcasted_iota(jnp.int32, (SLC_LANES, rows), 1) * CMP_STRIDE
        overlap = ((ci < sj + SLC_BLOCK) & (ci + CMP_BLOCK > sj)).astype(BF16)
        p_hi, p_mid, p_lo = _split3(p_sum)
        imp_ref[...] = _dot(overlap, p_hi) + _dot(overlap, p_mid) + _dot(overlap, p_lo)

    n_visible = (qi + 1) * (tq // CMP_STRIDE)
    n_steps = n_blk // CMP_ROW_STEP
    for b in range(1, n_steps + 1):
        below = n_visible <= b * CMP_ROW_STEP
        above = n_visible > (b - 1) * CMP_ROW_STEP
        pl.when(above & below if b < n_steps else above)(functools.partial(attend, b * CMP_ROW_STEP))

    blk = lax.broadcasted_iota(jnp.int32, (SLC_LANES, 1), 0)
    cur = t // SLC_BLOCK
    forced = (blk == 0) | (blk == cur) | (blk == cur - 1)
    causal = blk * SLC_BLOCK <= t
    score = jnp.where(forced, FORCE_SCORE, jnp.where(causal, imp_ref[...], -1.0))
    blk_f = blk.astype(F32)
    work = score
    for _ in range(SLC_TOPN):
        mx = jnp.max(work, axis=0, keepdims=True)
        first = jnp.min(jnp.where(work == mx, blk_f, float(SLC_LANES)), axis=0, keepdims=True)
        work = jnp.where(blk_f == first, PICKED, work)
    selT_ref[0] = jnp.where((work == PICKED) & (score >= 0.0), 1.0, 0.0).astype(selT_ref.dtype)


def _cmp_select(qL, k_cmp, v_cmpT, tq):
    g = NSA_KV_GROUPS
    s = qL.shape[2] // NSA_HPG
    n_blk = k_cmp.shape[1]
    q_spec = pl.BlockSpec((1, NSA_HEAD_DIM, NSA_HPG * tq), lambda gi, qi: (gi, 0, qi))
    return pl.pallas_call(
        functools.partial(_cmp_select_kernel, tq=tq),
        grid=(g, s // tq),
        in_specs=[q_spec,
                  pl.BlockSpec((1, n_blk, 2 * NSA_HEAD_DIM), lambda gi, qi: (gi, 0, 0)),
                  pl.BlockSpec((1, NSA_HEAD_DIM, n_blk), lambda gi, qi: (gi, 1, 0))],
        out_specs=[q_spec,
                   pl.BlockSpec((1, SLC_LANES, tq), lambda gi, qi: (gi, 0, qi))],
        out_shape=[jax.ShapeDtypeStruct(qL.shape, F32),
                   jax.ShapeDtypeStruct((g, SLC_LANES, s), BF16)],
        scratch_shapes=[pltpu.VMEM((SLC_LANES, tq), F32)],
        compiler_params=_params(("parallel", "arbitrary")),
        name="cmp_select",
    )(qL, k_cmp, v_cmpT)


def _win_attn_kernel(q_ref, k0_ref, k1_ref, v0_ref, v1_ref, ow_ref, *, tq):
    dk = NSA_HEAD_DIM
    qi = pl.program_id(1)
    q = q_ref[0]
    t = qi * tq + lax.broadcasted_iota(jnp.int32, (1, tq), 1)
    s = []
    for c, k_ref in enumerate((k0_ref, k1_ref)):
        j = (qi - 1 + c) * tq + lax.broadcasted_iota(jnp.int32, (tq, 1), 0)
        bias = jnp.where((j >= 0) & (j <= t) & (t - j < WINDOW), 0.0, NEG_INF)
        sc = _dot(k_ref[:, dk:], q)
        s.append(jnp.concatenate([x + bias for x in _per_head(sc, tq)], axis=1))
    m = jnp.maximum(jnp.max(s[0], axis=0, keepdims=True), jnp.max(s[1], axis=0, keepdims=True))
    o = (_dot(_with_ones_rows(v0_ref[...]), jnp.exp2(s[0] - m).astype(BF16))
         + _dot(_with_ones_rows(v1_ref[...]), jnp.exp2(s[1] - m).astype(BF16)))
    ow_ref[0] = o[:dk] / o[dk:dk + 1]


def _win_attn(qL, kr, vT, tq):
    g = NSA_KV_GROUPS
    s = qL.shape[2] // NSA_HPG
    assert WINDOW == tq
    q_spec = pl.BlockSpec((1, NSA_HEAD_DIM, NSA_HPG * tq), lambda gi, qi: (gi, 0, qi))

    def k_spec(c):
        return pl.BlockSpec((tq, 128), lambda gi, qi: (jnp.maximum(qi - 1 + c, 0), 2 * gi + 1))

    def v_spec(c):
        return pl.BlockSpec((NSA_HEAD_DIM, tq), lambda gi, qi: (2 * gi + 1, jnp.maximum(qi - 1 + c, 0)))

    return pl.pallas_call(
        functools.partial(_win_attn_kernel, tq=tq),
        grid=(g, s // tq),
        in_specs=[q_spec, k_spec(0), k_spec(1), v_spec(0), v_spec(1)],
        out_specs=q_spec,
        out_shape=jax.ShapeDtypeStruct(qL.shape, F32),
        compiler_params=_params(("parallel", "arbitrary")),
        name="win_attn",
    )(qL, kr, kr, vT, vT)


def _slc_attn_kernel(qi_tab, ki_tab, q_ref, k_ref, vT_ref, selT_ref, oc_ref, ow_ref, gT_ref,
                     o_ref, rhs_ref, m_ref, acc_ref, *, tq):
    dk = NSA_HEAD_DIM
    step = pl.program_id(1)
    qi = qi_tab[step]
    ki = ki_tab[step]

    @pl.when(ki == 0)
    def _():
        m_ref[...] = jnp.full_like(m_ref, NEG_INF)
        acc_ref[...] = jnp.zeros_like(acc_ref)
        not_selected = ((selT_ref[0].astype(F32) - 1.0) * (-NEG_INF)).astype(BF16)
        rhs_ref[:SLC_LANES, :] = jnp.concatenate([not_selected] * NSA_HPG, axis=1)
        rhs_ref[SLC_LANES:SLC_LANES + dk, :] = q_ref[0]
        rhs_ref[SLC_LANES + dk:, :] = jnp.zeros((dk, NSA_HPG * tq), BF16)

    lhs = k_ref[...]

    def update(causal):
        v_aug = _with_ones_rows(vT_ref[...])
        cw = SLC_CHUNK
        n_chunks = NSA_HPG * tq // cw
        scores = lambda c: _dot(lhs, rhs_ref[:, c * cw:(c + 1) * cw])
        s_next = scores(0)
        for c in range(n_chunks):
            lanes = slice(c * cw, (c + 1) * cw)
            s = s_next
            if c + 1 < n_chunks:
                s_next = scores(c + 1)
            if causal:
                pos = (c * cw) % tq + lax.broadcasted_iota(jnp.int32, (1, cw), 1)
                s = jnp.where(lax.broadcasted_iota(jnp.int32, (tq, 1), 0) <= pos, s, NEG_INF)
            m_old = m_ref[:, lanes]
            m_new = jnp.maximum(m_old, jnp.max(s, axis=0, keepdims=True))
            p = jnp.exp2(s - m_new).astype(BF16)
            acc_ref[:, lanes] = jnp.exp2(m_old - m_new) * acc_ref[:, lanes] + _dot(v_aug, p)
            m_ref[:, lanes] = m_new

    @pl.when(ki < qi)
    def _():
        update(False)

    @pl.when(ki == qi)
    def _():
        update(True)
        sig = _sigmoid_pair(gT_ref[...])[0]

        def gate(br):
            return jnp.concatenate([sig[br * NSA_HPG + hp:br * NSA_HPG + hp + 1] for hp in range(NSA_HPG)], axis=1)

        o_s = acc_ref[:dk, :] / acc_ref[dk:dk + 1, :]
        out = gate(0) * oc_ref[0] + gate(1) * o_s + gate(2) * ow_ref[0]
        o_ref[...] = jnp.concatenate(_per_head(out, tq), axis=0).T.astype(o_ref.dtype)


def _slc_attn(qL, kr, vT, selT, ocL, owL, gT, tq):
    g = NSA_KV_GROUPS
    s = qL.shape[2] // NSA_HPG
    tk = tq
    pairs = [(qi, ki) for qi in range(s // tq) for ki in range(qi + 1)]
    qi_tab = jnp.asarray(np.array([p[0] for p in pairs], np.int32))
    ki_tab = jnp.asarray(np.array([p[1] for p in pairs], np.int32))
    q_spec = pl.BlockSpec((1, NSA_HEAD_DIM, NSA_HPG * tq), lambda gi, st, qt, kt: (gi, 0, qt[st]))
    grid_spec = pltpu.PrefetchScalarGridSpec(
        num_scalar_prefetch=2,
        grid=(g, len(pairs)),
        in_specs=[q_spec,
                  pl.BlockSpec((tk, 256), lambda gi, st, qt, kt: (kt[st], gi)),
                  pl.BlockSpec((NSA_HEAD_DIM, tk), lambda gi, st, qt, kt: (2 * gi, kt[st])),
                  pl.BlockSpec((1, SLC_LANES, tq), lambda gi, st, qt, kt: (gi, 0, qt[st])),
                  q_spec, q_spec,
                  pl.BlockSpec((GATE_ROWS, tq), lambda gi, st, qt, kt: (gi, qt[st]))],
        out_specs=pl.BlockSpec((tq, NSA_HPG * NSA_HEAD_DIM), lambda gi, st, qt, kt: (qt[st], gi)),
        scratch_shapes=[pltpu.VMEM((SLC_LANES + 2 * NSA_HEAD_DIM, NSA_HPG * tq), BF16),
                        pltpu.VMEM((1, NSA_HPG * tq), F32),
                        pltpu.VMEM((NSA_HEAD_DIM + ONES_ROWS, NSA_HPG * tq), F32)])
    return pl.pallas_call(
        functools.partial(_slc_attn_kernel, tq=tq),
        grid_spec=grid_spec,
        out_shape=jax.ShapeDtypeStruct((s, NSA_WIDTH), BF16),
        compiler_params=_params(("parallel", "arbitrary")),
        name="slc_attn",
    )(qi_tab, ki_tab, qL, kr, vT, selT, ocL, owL, gT)


def _res_ln_epilogue(acc, x_ref, g_ref, b_ref, xo_ref, xb_ref):
    v = DN_ALPHA * x_ref[...] + acc
    mu = jnp.mean(v, axis=-1, keepdims=True)
    d = v - mu
    var = jnp.mean(d * d, axis=-1, keepdims=True)
    y = d * lax.rsqrt(var + LN_EPS) * g_ref[...] + b_ref[...]
    xo_ref[...] = y
    xb_ref[...] = y.astype(BF16)


def _mix_out_kernel(a1_ref, a2_ref, w_ref, x_ref, g_ref, b_ref, xo_ref, xb_ref):
    k1 = a1_ref.shape[1]
    y = _dot(a1_ref[...], w_ref[:k1, :]) + _dot(a2_ref[...], w_ref[k1:, :])
    _res_ln_epilogue(y, x_ref, g_ref, b_ref, xo_ref, xb_ref)


def _mix_out(a1, a2, w, l, x, gamma, beta, tm):
    s, k1 = a1.shape
    n = w.shape[2]
    row = pl.BlockSpec((tm, n), lambda i: (i, 0))
    vec = pl.BlockSpec((1, n), lambda i: (0, 0))
    return pl.pallas_call(
        _mix_out_kernel,
        grid=(s // tm,),
        in_specs=[pl.BlockSpec((tm, k1), lambda i: (i, 0)),
                  pl.BlockSpec((tm, k1), lambda i: (i, 0)),
                  pl.BlockSpec((None,) + w.shape[1:], lambda i: (l, 0, 0)),
                  row, vec, vec],
        out_specs=[row, row],
        out_shape=[jax.ShapeDtypeStruct((s, n), F32), jax.ShapeDtypeStruct((s, n), BF16)],
        compiler_params=_params(("parallel",)),
        name="mix_out",
    )(a1, a2, w, x, gamma, beta)


def _ffn_down_kernel(a_ref, w_ref, x_ref, g_ref, b_ref, xo_ref, xb_ref, acc_ref):
    k = pl.program_id(1)

    @pl.when(k == 0)
    def _():
        acc_ref[...] = _dot(a_ref[...], w_ref[...])

    @pl.when(k > 0)
    def _():
        acc_ref[...] += _dot(a_ref[...], w_ref[...])

    @pl.when(k == pl.num_programs(1) - 1)
    def _():
        _res_ln_epilogue(acc_ref[...], x_ref, g_ref, b_ref, xo_ref, xb_ref)


def _ffn_down(a, w, l, x, gamma, beta, tm, tk):
    s, kk = a.shape
    n = w.shape[2]
    row = pl.BlockSpec((tm, n), lambda i, k: (i, 0))
    vec = pl.BlockSpec((1, n), lambda i, k: (0, 0))
    res = pl.BlockSpec((tm, n), lambda i, k: (i, 0), pipeline_mode=pl.Buffered(1))
    return pl.pallas_call(
        _ffn_down_kernel,
        grid=(s // tm, kk // tk),
        in_specs=[pl.BlockSpec((tm, tk), lambda i, k: (i, k)),
                  pl.BlockSpec((None, tk, n), lambda i, k: (l, k, 0)),
                  res, vec, vec],
        out_specs=[row, row],
        out_shape=[jax.ShapeDtypeStruct((s, n), F32), jax.ShapeDtypeStruct((s, n), BF16)],
        scratch_shapes=[pltpu.VMEM((tm, n), F32)],
        compiler_params=pltpu.CompilerParams(dimension_semantics=("parallel", "arbitrary"),
                                             vmem_limit_bytes=FFN_DOWN_VMEM_LIMIT),
        name="ffn_down",
    )(a, w, x, gamma, beta)


HALO = 16


def _ffn_up_kernel(a_ref, ap_ref, wg_ref, wu_ref, cwg_ref, cwu_ref, cbg_ref, cbu_ref, o_ref, wgb_ref, wub_ref):
    i = pl.program_id(1)

    @pl.when(i == 0)
    def _():
        wgb_ref[...] = wg_ref[...].astype(BF16)
        wub_ref[...] = wu_ref[...].astype(BF16)

    ap = ap_ref[...]
    ap = jnp.where(i > 0, ap, jnp.zeros_like(ap))
    a = jnp.concatenate([ap, a_ref[...]], axis=0)

    def branch(w_ref, cw_ref, cb_ref):
        h = _dot(a, w_ref[...])
        cw = cw_ref[...]
        return (cw[2:3] * h[HALO:] + cw[1:2] * h[HALO - 1:-1] + cw[0:1] * h[HALO - 2:-2]) + cb_ref[...]

    gate = branch(wgb_ref, cwg_ref, cbg_ref)
    up = branch(wub_ref, cwu_ref, cbu_ref)
    o_ref[...] = (_silu(gate) * up).astype(o_ref.dtype)


def _ffn_up(a, w, l, conv_w, conv_b, tm, tn):
    s, d = a.shape
    nj = D_FF // tn
    return pl.pallas_call(
        _ffn_up_kernel,
        grid=(nj, s // tm),
        in_specs=[pl.BlockSpec((tm, d), lambda j, i: (i, 0)),
                  pl.BlockSpec((HALO, d), lambda j, i: (jnp.maximum(i * (tm // HALO) - 1, 0), 0)),
                  pl.BlockSpec((None, d, tn), lambda j, i: (l, 0, j)),
                  pl.BlockSpec((None, d, tn), lambda j, i: (l, 0, j + nj)),
                  pl.BlockSpec((3, tn), lambda j, i: (0, j)),
                  pl.BlockSpec((3, tn), lambda j, i: (0, j + nj)),
                  pl.BlockSpec((1, tn), lambda j, i: (0, j)),
                  pl.BlockSpec((1, tn), lambda j, i: (0, j + nj))],
        out_specs=pl.BlockSpec((tm, tn), lambda j, i: (i, j)),
        out_shape=jax.ShapeDtypeStruct((s, D_FF), BF16),
        scratch_shapes=[pltpu.VMEM((d, tn), BF16), pltpu.VMEM((d, tn), BF16)],
        compiler_params=_params(("parallel", "arbitrary")),
        name="ffn_up",
    )(a, a, w, w, conv_w, conv_w, conv_b, conv_b)


def _rope_tables(s):
    half = NSA_HEAD_DIM // 2
    inv = ROPE_THETA ** (-jnp.arange(half, dtype=F32) / half)
    ang = jnp.arange(s).astype(F32)[:, None] * inv[None, :]
    cos, sin = jnp.cos(ang), jnp.sin(ang)
    cos64 = jnp.concatenate([cos, cos], -1)
    sin64 = jnp.concatenate([-sin, sin], -1)
    ones, zeros = jnp.ones_like(cos64), jnp.zeros_like(sin64)
    cos_tabs = jnp.stack([jnp.concatenate([cos64, ones], -1), jnp.concatenate([cos64, cos64], -1)])
    sin_tabs = jnp.stack([jnp.concatenate([sin64, zeros], -1), jnp.concatenate([sin64, sin64], -1)])
    q_scale = NSA_HEAD_DIM ** -0.5 * LOG2E
    return cos_tabs, sin_tabs, cos.T * q_scale, sin.T * q_scale


def _nsa_weight_layout(w_nsa):
    nl, d, _ = w_nsa.shape
    g, dk = NSA_KV_GROUPS, NSA_HEAD_DIM
    q = w_nsa[:, :, :NSA_WIDTH]
    kv = w_nsa[:, :, NSA_WIDTH:NSA_WIDTH + 6 * NSA_KV_WIDTH].reshape(nl, d, 6, g, dk)
    kc, vc, ks, vs, kw, vw = (kv[:, :, i] for i in range(6))
    w_rows = jnp.concatenate([jnp.stack([kc, vc], 3).reshape(nl, d, 2 * g * dk),
                              jnp.stack([ks, kw], 3).reshape(nl, d, 2 * g * dk)], axis=2)
    w_v = jnp.stack([vs, vw], 3).reshape(nl, d, 2 * g * dk)
    gt = w_nsa[:, :, NSA_WIDTH + 6 * NSA_KV_WIDTH:].reshape(nl, d, g, NSA_HPG, 3).transpose(0, 1, 2, 4, 3)
    gt = jnp.pad(gt.reshape(nl, d, g, 3 * NSA_HPG), ((0, 0), (0, 0), (0, 0), (0, GATE_ROWS - 3 * NSA_HPG)))
    w_vg = jnp.concatenate([w_v, gt.reshape(nl, d, g * GATE_ROWS)], axis=2)
    return w_rows.astype(BF16), q.transpose(0, 2, 1).astype(BF16), w_vg.transpose(0, 2, 1).astype(BF16)


def _block_diag(a, b):
    top = jnp.concatenate([a, jnp.zeros(a.shape[:-1] + b.shape[-1:], a.dtype)], axis=-1)
    bot = jnp.concatenate([jnp.zeros(b.shape[:-1] + a.shape[-1:], b.dtype), b], axis=-1)
    return jnp.concatenate([top, bot], axis=-2)


def _compress_weight_layout(w1_k, w1_v, w2_k, w2_v):
    nl = w1_k.shape[0]
    per_token = lambda w: w.reshape(nl, CMP_BLOCK, NSA_HEAD_DIM, CMP_HIDDEN)
    return (_block_diag(per_token(w1_k), per_token(w1_v)).astype(BF16),
            _block_diag(w2_k, w2_v).astype(BF16))


def kernel(x, w_in, w_out, hgrn_lb_logits, hgrn_norm_w, cmp_pe_k, cmp_pe_v, cmp_w1_k, cmp_w2_k,
           cmp_w1_v, cmp_w2_v, ln1_g, ln1_b, w_up, conv_w, conv_b, w_down, ln2_g, ln2_b):
    s = x.shape[1]
    g = NSA_KV_GROUPS
    p_lb = jax.nn.softmax(hgrn_lb_logits.astype(F32), axis=0)
    lower_bounds = jnp.cumsum(p_lb, axis=0) - p_lb[0:1]
    cos_tabs, sin_tabs, cosT, sinT = _rope_tables(s)
    hw = 4 * HGRN_WIDTH
    w_hgrn_b, w_out_b, w_down_b = (w.astype(BF16) for w in (w_in[:, :, :hw], w_out, w_down))
    w_rows, w_qT, w_vgT = _nsa_weight_layout(w_in[:, :, hw:])
    cmp_w1, cmp_w2 = _compress_weight_layout(cmp_w1_k, cmp_w1_v, cmp_w2_k, cmp_w2_v)
    xf = x[0]
    xb = xf.astype(BF16)
    for l in range(DEPTH):
        xT = xb.T
        hh = _proj(xb, w_hgrn_b, l, hw, F32, 1024, 512)
        kvc = _proj_rope(xb, w_rows, l, 0, cos_tabs, sin_tabs, 0, F32, 1024, 512)
        kr = _proj_rope(xb, w_rows, l, 1, cos_tabs, sin_tabs, 1, BF16, 1024, 512, block_one_hot=True)
        qL = _projT_rope(w_qT, l, xT, cosT, sinT, NSA_TQ)
        vT, gT = _projT_vg(w_vgT, l, xT, 2 * g * NSA_HEAD_DIM, 1024)
        o_h = _hgrn(hh, lower_bounds[l][None], hgrn_norm_w[l][None], 512, 8)

        pe = jnp.concatenate([cmp_pe_k[l], cmp_pe_v[l]], axis=-1)[:, None, :]
        kv_cmp, kv_cmpT = _compress(kvc, pe, cmp_w1, cmp_w2, l)
        ocL, selT = _cmp_select(qL, kv_cmp, kv_cmpT, NSA_TQ)
        owL = _win_attn(qL, kr, vT, NSA_TQ)
        o_n = _slc_attn(qL, kr, vT, selT, ocL, owL, gT, NSA_TQ)

        xf, xb = _mix_out(o_h, o_n, w_out_b, l, xf, ln1_g[l][None], ln1_b[l][None], 512)
        u = _ffn_up(xb, w_up, l, conv_w[l], conv_b[l][None], 1024, 512)
        xf, xb = _ffn_down(u, w_down_b, l, xf, ln2_g[l][None], ln2_b[l][None], 1024, 512)
    return xf[None]
```

```python
import functools

import numpy as np
import jax
import jax.numpy as jnp
from jax import lax
from jax.experimental import pallas as pl
from jax.experimental.pallas import tpu as pltpu

F32 = jnp.float32
BF16 = jnp.bfloat16

D_MODEL = 2048
DEPTH = 4
HGRN_WIDTH = 1024
HGRN_HEADS = 8
HEAD_LANES = 128
NSA_WIDTH = 1024
NSA_HEAD_DIM = 64
NSA_HEADS = 16
NSA_KV_GROUPS = 4
NSA_HPG = 4
NSA_KV_WIDTH = NSA_KV_GROUPS * NSA_HEAD_DIM
CMP_BLOCK = 32
CMP_STRIDE = 16
CMP_HIDDEN = 256
SLC_BLOCK = 64
SLC_TOPN = 16
SLC_LANES = 128
WINDOW = 512
D_FF = 5632
ROPE_THETA = 10000.0
LN_EPS = 1e-5
RMS_EPS = 1e-6
F_MIN = 1e-30
DN_ALPHA = (2 * DEPTH) ** 0.25
NEG_INF = -1e30
FORCE_SCORE = 1e9

GATE_ROWS = 16
LOG2E = 1.4426950408889634

NSA_TQ = 512
SLC_CHUNK = 512
HGRN_CHUNK = 64
HGRN_SUB = 16
VMEM_LIMIT = 48 * 1024 * 1024
FFN_DOWN_VMEM_LIMIT = 56 * 1024 * 1024


def _params(sem):
    return pltpu.CompilerParams(dimension_semantics=sem, vmem_limit_bytes=VMEM_LIMIT)


def _dot(a, b):
    return jnp.dot(a, b, preferred_element_type=F32)


def _dot_nt(a, b):
    return lax.dot_general(a, b, (((1,), (1,)), ((), ())), preferred_element_type=F32)


def _dot_tn(a, b):
    return lax.dot_general(a, b, (((0,), (0,)), ((), ())), preferred_element_type=F32)


def _split3(x):
    hi = x.astype(BF16)
    r = x - hi.astype(F32)
    mid = r.astype(BF16)
    lo = (r - mid.astype(F32)).astype(BF16)
    return hi, mid, lo


def _sigmoid_pair(z):
    e = jnp.exp(-jnp.abs(z))
    r = 1.0 / (1.0 + e)
    er = e * r
    pos = z >= 0
    return jnp.where(pos, r, er), jnp.where(pos, er, r)


def _silu(x):
    return x * _sigmoid_pair(x)[0]


def _proj_kernel(x_ref, w_ref, o_ref):
    o_ref[...] = _dot(x_ref[...], w_ref[...]).astype(o_ref.dtype)


def _proj(x, w, l, n, out_dtype, tm, tn):
    m, k = x.shape
    return pl.pallas_call(
        _proj_kernel,
        grid=(m // tm, n // tn),
        in_specs=[pl.BlockSpec((tm, k), lambda i, j: (i, 0)),
                  pl.BlockSpec((None, k, tn), lambda i, j: (l, 0, j))],
        out_specs=pl.BlockSpec((tm, tn), lambda i, j: (i, j)),
        out_shape=jax.ShapeDtypeStruct((m, n), out_dtype),
        compiler_params=_params(("parallel", "arbitrary")),
        name="proj",
    )(x, w)


def _proj_rope_kernel(x_ref, w_ref, cos_ref, sin_ref, o_ref, *, tn, block_one_hot):
    h = _dot(x_ref[...], w_ref[...])
    cos = cos_ref[0]
    sin = sin_ref[0]
    tm = cos.shape[0]
    lane = lax.broadcasted_iota(jnp.int32, cos.shape, 1)
    first_half = (lane % NSA_HEAD_DIM) < (NSA_HEAD_DIM // 2)
    if block_one_hot:
        pos = pl.program_id(0) * tm + lax.broadcasted_iota(jnp.int32, cos.shape, 0)
        one_hot = (lane == pos // SLC_BLOCK).astype(o_ref.dtype)
    for c in range(tn // 128):
        hc = h[:, c * 128:(c + 1) * 128]
        rot = jnp.where(first_half, pltpu.roll(hc, 96, axis=1), pltpu.roll(hc, 32, axis=1))
        roped = (hc * cos + rot * sin).astype(o_ref.dtype)
        if block_one_hot:
            o_ref[:, 2 * c * 128:(2 * c + 1) * 128] = one_hot
            o_ref[:, (2 * c + 1) * 128:(2 * c + 2) * 128] = roped
        else:
            o_ref[:, c * 128:(c + 1) * 128] = roped


def _proj_rope(x, w, l, col0, cos_tabs, sin_tabs, tab, out_dtype, tm, tn, block_one_hot=False):
    m, k = x.shape
    n_out = 2 * tn if block_one_hot else tn
    tab_spec = pl.BlockSpec((1, tm, 128), lambda i: (tab, i, 0))
    return pl.pallas_call(
        functools.partial(_proj_rope_kernel, tn=tn, block_one_hot=block_one_hot),
        grid=(m // tm,),
        in_specs=[pl.BlockSpec((tm, k), lambda i: (i, 0)),
                  pl.BlockSpec((None, k, tn), lambda i: (l, 0, col0)),
                  tab_spec, tab_spec],
        out_specs=pl.BlockSpec((tm, n_out), lambda i: (i, 0)),
        out_shape=jax.ShapeDtypeStruct((m, n_out), out_dtype),
        compiler_params=_params(("parallel",)),
        name="proj_rope",
    )(x, w, cos_tabs, sin_tabs)


def _hgrn_chunk(q, z, v, g, lb, nw, st, tri, sub_row):
    C, c = HGRN_CHUNK, HGRN_SUB
    one_minus_lb = 1.0 - lb
    sig, sig_neg = _sigmoid_pair(z)
    f = lb + one_minus_lb * sig
    lf = jnp.log2(jnp.maximum(f, F_MIN))
    k = one_minus_lb * sig_neg
    qs = _silu(q)
    lf_hi, lf_mid, lf_lo = _split3(lf)
    b = _dot(tri, lf_hi) + _dot(tri, lf_mid) + _dot(tri, lf_lo)
    o_inter = _dot_nt((qs * jnp.exp2(b)).astype(BF16), st.astype(BF16))
    v16 = v.astype(BF16)
    outs = []
    for i in range(C // c):
        lo = i * c
        b_i = b[lo:lo + c]
        qs_i = qs[lo:lo + c]
        k_i = k[lo:lo + c]
        v_i = v[lo:lo + c]
        o_i = o_inter[lo:lo + c]
        if i > 0:
            beta = b[lo - 1:lo]
            qt = (qs_i * jnp.exp2(b_i - beta)).astype(BF16)
            kt = (k[:lo] * jnp.exp2(beta - b[:lo])).astype(BF16)
            att = _dot_nt(qt, kt)
            o_i = o_i + _dot(att.astype(BF16), v16[:lo])
        for s in range(c):
            d = jnp.exp2(b_i - b_i[s:s + 1])
            w = jnp.sum(qs_i * d * k_i[s:s + 1], axis=-1, keepdims=True)
            w = jnp.where(sub_row >= s, w, 0.0)
            o_i = o_i + w * v_i[s:s + 1]
        outs.append(o_i)
    o = jnp.concatenate(outs, axis=0)
    b_last = b[C - 1:C]
    kd = (k * jnp.exp2(b_last - b)).astype(BF16)
    st_new = jnp.exp2(b_last) * st + _dot_tn(v16, kd)
    o = o * lax.rsqrt(jnp.mean(o * o, axis=-1, keepdims=True) + RMS_EPS)
    return o * nw * _silu(g), st_new


def _hgrn_kernel(q_ref, z_ref, v_ref, g_ref, lb_ref, nw_ref, o_ref, st_ref, *, tb, hb):
    C, c = HGRN_CHUNK, HGRN_SUB

    @pl.when(pl.program_id(1) == 0)
    def _():
        st_ref[...] = jnp.zeros_like(st_ref)

    tri = (lax.broadcasted_iota(jnp.int32, (C, C), 1)
           <= lax.broadcasted_iota(jnp.int32, (C, C), 0)).astype(BF16)
    sub_row = lax.broadcasted_iota(jnp.int32, (c, 1), 0)

    def chunk(ci, carry):
        r0 = pl.multiple_of(ci * C, C)
        for h in range(hb):
            lanes = slice(h * HEAD_LANES, (h + 1) * HEAD_LANES)
            out, st_new = _hgrn_chunk(q_ref[pl.ds(r0, C), lanes], z_ref[pl.ds(r0, C), lanes],
                                      v_ref[pl.ds(r0, C), lanes], g_ref[pl.ds(r0, C), lanes],
                                      lb_ref[:, lanes], nw_ref[:, lanes], st_ref[h], tri, sub_row)
            st_ref[h] = st_new
            o_ref[pl.ds(r0, C), lanes] = out.astype(o_ref.dtype)
        return carry

    lax.fori_loop(0, tb // C, chunk, 0)


def _hgrn(hh, lb, nw, tb, hb):
    s = hh.shape[0]
    nhb = HGRN_HEADS // hb
    wide = hb * HEAD_LANES

    def col(off):
        return pl.BlockSpec((tb, wide), lambda h, t: (t, off * nhb + h))

    vec = pl.BlockSpec((1, wide), lambda h, t: (0, h))
    return pl.pallas_call(
        functools.partial(_hgrn_kernel, tb=tb, hb=hb),
        grid=(nhb, s // tb),
        in_specs=[col(0), col(1), col(2), col(3), vec, vec],
        out_specs=pl.BlockSpec((tb, wide), lambda h, t: (t, h)),
        out_shape=jax.ShapeDtypeStruct((s, HGRN_WIDTH), BF16),
        scratch_shapes=[pltpu.VMEM((hb, HEAD_LANES, HEAD_LANES), F32)],
        compiler_params=_params(("parallel", "arbitrary")),
        name="hgrn2",
    )(hh, hh, hh, hh, lb, nw)


def _projT_rope_kernel(w_ref, xT_ref, cos_ref, sin_ref, o_ref, *, tq):
    h = _dot(w_ref[...], xT_ref[...])
    cos = cos_ref[...]
    sin = sin_ref[...]
    half = NSA_HEAD_DIM // 2
    for hp in range(NSA_HPG):
        r = hp * NSA_HEAD_DIM
        x1 = h[r:r + half]
        x2 = h[r + half:r + 2 * half]
        o_ref[0, :half, hp * tq:(hp + 1) * tq] = (x1 * cos - x2 * sin).astype(o_ref.dtype)
        o_ref[0, half:, hp * tq:(hp + 1) * tq] = (x2 * cos + x1 * sin).astype(o_ref.dtype)


def _projT_rope(wT, l, xT, cosT, sinT, tq):
    _, n, d = wT.shape
    s = xT.shape[1]
    half = NSA_HEAD_DIM // 2
    gw = NSA_HPG * NSA_HEAD_DIM
    return pl.pallas_call(
        functools.partial(_projT_rope_kernel, tq=tq),
        grid=(s // tq, n // gw),
        in_specs=[pl.BlockSpec((None, gw, d), lambda i, j: (l, j, 0)),
                  pl.BlockSpec((d, tq), lambda i, j: (0, i)),
                  pl.BlockSpec((half, tq), lambda i, j: (0, i)),
                  pl.BlockSpec((half, tq), lambda i, j: (0, i))],
        out_specs=pl.BlockSpec((1, NSA_HEAD_DIM, NSA_HPG * tq), lambda i, j: (j, 0, i)),
        out_shape=jax.ShapeDtypeStruct((n // gw, NSA_HEAD_DIM, NSA_HPG * s), BF16),
        compiler_params=_params(("parallel", "arbitrary")),
        name="projT_rope",
    )(wT, xT, cosT, sinT)


def _projT_vg_kernel(w_ref, xT_ref, v_ref, g_ref):
    h = _dot(w_ref[...], xT_ref[...])
    nv = v_ref.shape[0]
    v_ref[...] = h[:nv].astype(v_ref.dtype)
    g_ref[...] = h[nv:]


def _projT_vg(wT, l, xT, nv, ts):
    _, n, d = wT.shape
    s = xT.shape[1]
    return pl.pallas_call(
        _projT_vg_kernel,
        grid=(s // ts,),
        in_specs=[pl.BlockSpec((None, n, d), lambda i: (l, 0, 0)),
                  pl.BlockSpec((d, ts), lambda i: (0, i))],
        out_specs=[pl.BlockSpec((nv, ts), lambda i: (0, i)),
                   pl.BlockSpec((n - nv, ts), lambda i: (0, i))],
        out_shape=[jax.ShapeDtypeStruct((nv, s), BF16), jax.ShapeDtypeStruct((n - nv, s), F32)],
        compiler_params=_params(("parallel",)),
        name="projT_vg",
    )(wT, xT)


def _compress_kernel(x_ref, pe_ref, w1_ref, w2_ref, kv_ref, kvT_ref):
    n_blk = x_ref.shape[0] // CMP_STRIDE
    width = 2 * CMP_HIDDEN
    top = jnp.zeros((n_blk, width), F32)
    bot = jnp.zeros((n_blk, width), F32)
    for j in range(CMP_STRIDE):
        rows = x_ref[pl.ds(j, n_blk, stride=CMP_STRIDE), :]
        top = top + _dot((rows + pe_ref[j]).astype(BF16), w1_ref[0, j])
        bot = bot + _dot((rows + pe_ref[CMP_STRIDE + j]).astype(BF16), w1_ref[0, CMP_STRIDE + j])
    hid = top + pltpu.roll(bot, n_blk - 1, axis=0)
    out = _dot(_silu(hid).astype(BF16), w2_ref[0])
    kv_ref[0] = out.astype(kv_ref.dtype)
    kvT_ref[0] = out.T.astype(kvT_ref.dtype)


def _compress(x, pe, w1, w2, l):
    s = x.shape[0]
    g = NSA_KV_GROUPS
    n_blk = s // CMP_STRIDE
    return pl.pallas_call(
        _compress_kernel,
        grid=(g,),
        in_specs=[pl.BlockSpec((s, 128), lambda gi: (0, gi)),
                  pl.BlockSpec(pe.shape, lambda gi: (0, 0, 0)),
                  pl.BlockSpec((1,) + w1.shape[1:], lambda gi: (l, 0, 0, 0)),
                  pl.BlockSpec((1,) + w2.shape[1:], lambda gi: (l, 0, 0))],
        out_specs=[pl.BlockSpec((1, n_blk, 128), lambda gi: (gi, 0, 0)),
                   pl.BlockSpec((1, 128, n_blk), lambda gi: (gi, 0, 0))],
        out_shape=[jax.ShapeDtypeStruct((g, n_blk, 128), BF16),
                   jax.ShapeDtypeStruct((g, 128, n_blk), BF16)],
        compiler_params=_params(("parallel",)),
        name="compress",
    )(x, pe, w1, w2)


ONES_ROWS = 16


def _with_ones_rows(vT):
    return jnp.concatenate([vT, jnp.ones((ONES_ROWS, vT.shape[1]), vT.dtype)], axis=0)


def _per_head(x, tq):
    return [x[:, hp * tq:(hp + 1) * tq] for hp in range(NSA_HPG)]


CMP_ROW_STEP = 128
PICKED = -3e38


def _cmp_select_kernel(q_ref, kc_ref, vcT_ref, oc_ref, selT_ref, imp_ref, *, tq):
    n_blk = kc_ref.shape[1]
    qi = pl.program_id(1)
    t = qi * tq + lax.broadcasted_iota(jnp.int32, (1, tq), 1)
    any_visible = (t >= CMP_BLOCK - 1).astype(F32)

    def attend(rows):
        kc = kc_ref[0, :rows, :NSA_HEAD_DIM]
        vcT = vcT_ref[0, :, :rows]
        n_idx = lax.broadcasted_iota(jnp.int32, (rows, 1), 0)
        bias = jnp.where((n_idx * CMP_STRIDE + (CMP_BLOCK - 1)) <= t, 0.0, NEG_INF)
        s = _dot(kc, q_ref[0])
        s = jnp.concatenate([x + bias for x in _per_head(s, tq)], axis=1)
        e = jnp.exp2(s - jnp.max(s, axis=0, keepdims=True))
        scale = jnp.concatenate([any_visible] * NSA_HPG, axis=1) / jnp.sum(e, axis=0, keepdims=True)
        p = e * scale
        oc_ref[0] = _dot(vcT, p.astype(BF16))
        p_sum = functools.reduce(lambda a, b: a + b, _per_head(p, tq))
        sj = lax.broadcasted_iota(jnp.int32, (SLC_LANES, rows), 0) * SLC_BLOCK
        ci = lax.broadcasted_iota(jnp.int32, (SLC_LANES, rows), 1) * CMP_STRIDE
        overlap = ((ci < sj + SLC_BLOCK) & (ci + CMP_BLOCK > sj)).astype(BF16)
        p_hi, p_mid, p_lo = _split3(p_sum)
        imp_ref[...] = _dot(overlap, p_hi) + _dot(overlap, p_mid) + _dot(overlap, p_lo)

    n_visible = (qi + 1) * (tq // CMP_STRIDE)
    n_steps = n_blk // CMP_ROW_STEP
    for b in range(1, n_steps + 1):
        below = n_visible <= b * CMP_ROW_STEP
        above = n_visible > (b - 1) * CMP_ROW_STEP
        pl.when(above & below if b < n_steps else above)(functools.partial(attend, b * CMP_ROW_STEP))

    blk = lax.broadcasted_iota(jnp.int32, (SLC_LANES, 1), 0)
    cur = t // SLC_BLOCK
    forced = (blk == 0) | (blk == cur) | (blk == cur - 1)
    causal = blk * SLC_BLOCK <= t
    score = jnp.where(forced, FORCE_SCORE, jnp.where(causal, imp_ref[...], -1.0))
    blk_f = blk.astype(F32)
    work = score
    for _ in range(SLC_TOPN):
        mx = jnp.max(work, axis=0, keepdims=True)
        first = jnp.min(jnp.where(work == mx, blk_f, float(SLC_LANES)), axis=0, keepdims=True)
        work = jnp.where(blk_f == first, PICKED, work)
    selT_ref[0] = jnp.where((work == PICKED) & (score >= 0.0), 1.0, 0.0).astype(selT_ref.dtype)


def _cmp_select(qL, k_cmp, v_cmpT, tq):
    g = NSA_KV_GROUPS
    s = qL.shape[2] // NSA_HPG
    n_blk = k_cmp.shape[1]
    q_spec = pl.BlockSpec((1, NSA_HEAD_DIM, NSA_HPG * tq), lambda gi, qi: (gi, 0, qi))
    return pl.pallas_call(
        functools.partial(_cmp_select_kernel, tq=tq),
        grid=(g, s // tq),
        in_specs=[q_spec,
                  pl.BlockSpec((1, n_blk, 2 * NSA_HEAD_DIM), lambda gi, qi: (gi, 0, 0)),
                  pl.BlockSpec((1, NSA_HEAD_DIM, n_blk), lambda gi, qi: (gi, 1, 0))],
        out_specs=[q_spec,
                   pl.BlockSpec((1, SLC_LANES, tq), lambda gi, qi: (gi, 0, qi))],
        out_shape=[jax.ShapeDtypeStruct(qL.shape, F32),
                   jax.ShapeDtypeStruct((g, SLC_LANES, s), BF16)],
        scratch_shapes=[pltpu.VMEM((SLC_LANES, tq), F32)],
        compiler_params=_params(("parallel", "arbitrary")),
        name="cmp_select",
    )(qL, k_cmp, v_cmpT)


def _win_attn_kernel(q_ref, k0_ref, k1_ref, v0_ref, v1_ref, ow_ref, *, tq):
    dk = NSA_HEAD_DIM
    qi = pl.program_id(1)
    q = q_ref[0]
    t = qi * tq + lax.broadcasted_iota(jnp.int32, (1, tq), 1)
    s = []
    for c, k_ref in enumerate((k0_ref, k1_ref)):
        j = (qi - 1 + c) * tq + lax.broadcasted_iota(jnp.int32, (tq, 1), 0)
        bias = jnp.where((j >= 0) & (j <= t) & (t - j < WINDOW), 0.0, NEG_INF)
        sc = _dot(k_ref[:, dk:], q)
        s.append(jnp.concatenate([x + bias for x in _per_head(sc, tq)], axis=1))
    m = jnp.maximum(jnp.max(s[0], axis=0, keepdims=True), jnp.max(s[1], axis=0, keepdims=True))
    o = (_dot(_with_ones_rows(v0_ref[...]), jnp.exp2(s[0] - m).astype(BF16))
         + _dot(_with_ones_rows(v1_ref[...]), jnp.exp2(s[1] - m).astype(BF16)))
    ow_ref[0] = o[:dk] / o[dk:dk + 1]


def _win_attn(qL, kr, vT, tq):
    g = NSA_KV_GROUPS
    s = qL.shape[2] // NSA_HPG
    assert WINDOW == tq
    q_spec = pl.BlockSpec((1, NSA_HEAD_DIM, NSA_HPG * tq), lambda gi, qi: (gi, 0, qi))

    def k_spec(c):
        return pl.BlockSpec((tq, 128), lambda gi, qi: (jnp.maximum(qi - 1 + c, 0), 2 * gi + 1))

    def v_spec(c):
        return pl.BlockSpec((NSA_HEAD_DIM, tq), lambda gi, qi: (2 * gi + 1, jnp.maximum(qi - 1 + c, 0)))

    return pl.pallas_call(
        functools.partial(_win_attn_kernel, tq=tq),
        grid=(g, s // tq),
        in_specs=[q_spec, k_spec(0), k_spec(1), v_spec(0), v_spec(1)],
        out_specs=q_spec,
        out_shape=jax.ShapeDtypeStruct(qL.shape, F32),
        compiler_params=_params(("parallel", "arbitrary")),
        name="win_attn",
    )(qL, kr, kr, vT, vT)


def _slc_attn_kernel(qi_tab, ki_tab, q_ref, k_ref, vT_ref, selT_ref, oc_ref, ow_ref, gT_ref,
                     o_ref, rhs_ref, m_ref, acc_ref, *, tq):
    dk = NSA_HEAD_DIM
    step = pl.program_id(1)
    qi = qi_tab[step]
    ki = ki_tab[step]

    @pl.when(ki == 0)
    def _():
        m_ref[...] = jnp.full_like(m_ref, NEG_INF)
        acc_ref[...] = jnp.zeros_like(acc_ref)
        not_selected = ((selT_ref[0].astype(F32) - 1.0) * (-NEG_INF)).astype(BF16)
        rhs_ref[:SLC_LANES, :] = jnp.concatenate([not_selected] * NSA_HPG, axis=1)
        rhs_ref[SLC_LANES:SLC_LANES + dk, :] = q_ref[0]
        rhs_ref[SLC_LANES + dk:, :] = jnp.zeros((dk, NSA_HPG * tq), BF16)

    def update(sub_tiles):
        chains = [(sub, hp, causal) for sub, causal in sub_tiles for hp in range(NSA_HPG)]
        v_aug = {sub: _with_ones_rows(vT_ref[:, sub * tq:(sub + 1) * tq]) for sub, _ in sub_tiles}
        scores = lambda sub, hp: _dot(k_ref[sub * tq:(sub + 1) * tq, :], rhs_ref[:, hp * tq:(hp + 1) * tq])
        s_next = scores(*chains[0][:2])
        for n, (sub, hp, causal) in enumerate(chains):
            lanes = slice(hp * tq, (hp + 1) * tq)
            s = s_next
            if n + 1 < len(chains):
                s_next = scores(*chains[n + 1][:2])
            if causal:
                s = jnp.where(lax.broadcasted_iota(jnp.int32, (tq, 1), 0)
                              <= lax.broadcasted_iota(jnp.int32, (1, tq), 1), s, NEG_INF)
            m_old = m_ref[:, lanes]
            m_new = jnp.maximum(m_old, jnp.max(s, axis=0, keepdims=True))
            p = jnp.exp2(s - m_new).astype(BF16)
            acc_ref[:, lanes] = jnp.exp2(m_old - m_new) * acc_ref[:, lanes] + _dot(v_aug[sub], p)
            m_ref[:, lanes] = m_new

    def finish():
        sig = _sigmoid_pair(gT_ref[...])[0]

        def gate(br):
            return jnp.concatenate([sig[br * NSA_HPG + hp:br * NSA_HPG + hp + 1] for hp in range(NSA_HPG)], axis=1)

        o_s = acc_ref[:dk, :] / acc_ref[dk:dk + 1, :]
        out = gate(0) * oc_ref[0] + gate(1) * o_s + gate(2) * ow_ref[0]
        o_ref[...] = jnp.concatenate(_per_head(out, tq), axis=0).T.astype(o_ref.dtype)

    ahead = qi - 2 * ki

    @pl.when(ahead >= 2)
    def _():
        update([(0, False), (1, False)])

    @pl.when(ahead == 1)
    def _():
        update([(0, False), (1, True)])
        finish()

    @pl.when(ahead == 0)
    def _():
        update([(0, True)])
        finish()


def _slc_attn(qL, kr, vT, selT, ocL, owL, gT, tq):
    g = NSA_KV_GROUPS
    s = qL.shape[2] // NSA_HPG
    tk = 2 * tq
    pairs = [(qi, ki) for qi in range(s // tq) for ki in range(qi // 2 + 1)]
    qi_tab = jnp.asarray(np.array([p[0] for p in pairs], np.int32))
    ki_tab = jnp.asarray(np.array([p[1] for p in pairs], np.int32))
    q_spec = pl.BlockSpec((1, NSA_HEAD_DIM, NSA_HPG * tq), lambda gi, st, qt, kt: (gi, 0, qt[st]))
    grid_spec = pltpu.PrefetchScalarGridSpec(
        num_scalar_prefetch=2,
        grid=(g, len(pairs)),
        in_specs=[q_spec,
                  pl.BlockSpec((tk, 256), lambda gi, st, qt, kt: (kt[st], gi)),
                  pl.BlockSpec((NSA_HEAD_DIM, tk), lambda gi, st, qt, kt: (2 * gi, kt[st])),
                  pl.BlockSpec((1, SLC_LANES, tq), lambda gi, st, qt, kt: (gi, 0, qt[st])),
                  q_spec, q_spec,
                  pl.BlockSpec((GATE_ROWS, tq), lambda gi, st, qt, kt: (gi, qt[st]))],
        out_specs=pl.BlockSpec((tq, NSA_HPG * NSA_HEAD_DIM), lambda gi, st, qt, kt: (qt[st], gi)),
        scratch_shapes=[pltpu.VMEM((SLC_LANES + 2 * NSA_HEAD_DIM, NSA_HPG * tq), BF16),
                        pltpu.VMEM((1, NSA_HPG * tq), F32),
                        pltpu.VMEM((NSA_HEAD_DIM + ONES_ROWS, NSA_HPG * tq), F32)])
    return pl.pallas_call(
        functools.partial(_slc_attn_kernel, tq=tq),
        grid_spec=grid_spec,
        out_shape=jax.ShapeDtypeStruct((s, NSA_WIDTH), BF16),
        compiler_params=_params(("parallel", "arbitrary")),
        name="slc_attn",
    )(qi_tab, ki_tab, qL, kr, vT, selT, ocL, owL, gT)


def _res_ln_epilogue(acc, x_ref, g_ref, b_ref, xo_ref, xb_ref):
    v = DN_ALPHA * x_ref[...] + acc
    mu = jnp.mean(v, axis=-1, keepdims=True)
    d = v - mu
    var = jnp.mean(d * d, axis=-1, keepdims=True)
    y = d * lax.rsqrt(var + LN_EPS) * g_ref[...] + b_ref[...]
    xo_ref[...] = y
    xb_ref[...] = y.astype(BF16)


def _mix_out_kernel(a1_ref, a2_ref, w_ref, x_ref, g_ref, b_ref, xo_ref, xb_ref):
    k1 = a1_ref.shape[1]
    y = _dot(a1_ref[...], w_ref[:k1, :]) + _dot(a2_ref[...], w_ref[k1:, :])
    _res_ln_epilogue(y, x_ref, g_ref, b_ref, xo_ref, xb_ref)


def _mix_out(a1, a2, w, l, x, gamma, beta, tm):
    s, k1 = a1.shape
    n = w.shape[2]
    row = pl.BlockSpec((tm, n), lambda i: (i, 0))
    vec = pl.BlockSpec((1, n), lambda i: (0, 0))
    return pl.pallas_call(
        _mix_out_kernel,
        grid=(s // tm,),
        in_specs=[pl.BlockSpec((tm, k1), lambda i: (i, 0)),
                  pl.BlockSpec((tm, k1), lambda i: (i, 0)),
                  pl.BlockSpec((None,) + w.shape[1:], lambda i: (l, 0, 0)),
                  row, vec, vec],
        out_specs=[row, row],
        out_shape=[jax.ShapeDtypeStruct((s, n), F32), jax.ShapeDtypeStruct((s, n), BF16)],
        compiler_params=_params(("parallel",)),
        name="mix_out",
    )(a1, a2, w, x, gamma, beta)


def _ffn_down_kernel(a_ref, w_ref, x_ref, g_ref, b_ref, xo_ref, xb_ref, acc_ref):
    k = pl.program_id(1)

    @pl.when(k == 0)
    def _():
        acc_ref[...] = _dot(a_ref[...], w_ref[...])

    @pl.when(k > 0)
    def _():
        acc_ref[...] += _dot(a_ref[...], w_ref[...])

    @pl.when(k == pl.num_programs(1) - 1)
    def _():
        _res_ln_epilogue(acc_ref[...], x_ref, g_ref, b_ref, xo_ref, xb_ref)


def _ffn_down(a, w, l, x, gamma, beta, tm, tk):
    s, kk = a.shape
    n = w.shape[2]
    row = pl.BlockSpec((tm, n), lambda i, k: (i, 0))
    vec = pl.BlockSpec((1, n), lambda i, k: (0, 0))
    res = pl.BlockSpec((tm, n), lambda i, k: (i, 0), pipeline_mode=pl.Buffered(1))
    return pl.pallas_call(
        _ffn_down_kernel,
        grid=(s // tm, kk // tk),
        in_specs=[pl.BlockSpec((tm, tk), lambda i, k: (i, k)),
                  pl.BlockSpec((None, tk, n), lambda i, k: (l, k, 0)),
                  res, vec, vec],
        out_specs=[row, row],
        out_shape=[jax.ShapeDtypeStruct((s, n), F32), jax.ShapeDtypeStruct((s, n), BF16)],
        scratch_shapes=[pltpu.VMEM((tm, n), F32)],
        compiler_params=pltpu.CompilerParams(dimension_semantics=("parallel", "arbitrary"),
                                             vmem_limit_bytes=FFN_DOWN_VMEM_LIMIT),
        name="ffn_down",
    )(a, w, x, gamma, beta)


HALO = 16


def _ffn_up_kernel(a_ref, ap_ref, wg_ref, wu_ref, cwg_ref, cwu_ref, cbg_ref, cbu_ref, o_ref, wgb_ref, wub_ref):
    i = pl.program_id(1)

    @pl.when(i == 0)
    def _():
        wgb_ref[...] = wg_ref[...].astype(BF16)
        wub_ref[...] = wu_ref[...].astype(BF16)

    ap = ap_ref[...]
    ap = jnp.where(i > 0, ap, jnp.zeros_like(ap))
    a = jnp.concatenate([ap, a_ref[...]], axis=0)

    def branch(w_ref, cw_ref, cb_ref):
        h = _dot(a, w_ref[...])
        cw = cw_ref[...]
        return (cw[2:3] * h[HALO:] + cw[1:2] * h[HALO - 1:-1] + cw[0:1] * h[HALO - 2:-2]) + cb_ref[...]

    gate = branch(wgb_ref, cwg_ref, cbg_ref)
    up = branch(wub_ref, cwu_ref, cbu_ref)
    o_ref[...] = (_silu(gate) * up).astype(o_ref.dtype)


def _ffn_up(a, w, l, conv_w, conv_b, tm, tn):
    s, d = a.shape
    nj = D_FF // tn
    return pl.pallas_call(
        _ffn_up_kernel,
        grid=(nj, s // tm),
        in_specs=[pl.BlockSpec((tm, d), lambda j, i: (i, 0)),
                  pl.BlockSpec((HALO, d), lambda j, i: (jnp.maximum(i * (tm // HALO) - 1, 0), 0)),
                  pl.BlockSpec((None, d, tn), lambda j, i: (l, 0, j)),
                  pl.BlockSpec((None, d, tn), lambda j, i: (l, 0, j + nj)),
                  pl.BlockSpec((3, tn), lambda j, i: (0, j)),
                  pl.BlockSpec((3, tn), lambda j, i: (0, j + nj)),
                  pl.BlockSpec((1, tn), lambda j, i: (0, j)),
                  pl.BlockSpec((1, tn), lambda j, i: (0, j + nj))],
        out_specs=pl.BlockSpec((tm, tn), lambda j, i: (i, j)),
        out_shape=jax.ShapeDtypeStruct((s, D_FF), BF16),
        scratch_shapes=[pltpu.VMEM((d, tn), BF16), pltpu.VMEM((d, tn), BF16)],
        compiler_params=_params(("parallel", "arbitrary")),
        name="ffn_up",
    )(a, a, w, w, conv_w, conv_w, conv_b, conv_b)


def _rope_tables(s):
    half = NSA_HEAD_DIM // 2
    inv = ROPE_THETA ** (-jnp.arange(half, dtype=F32) / half)
    ang = jnp.arange(s).astype(F32)[:, None] * inv[None, :]
    cos, sin = jnp.cos(ang), jnp.sin(ang)
    cos64 = jnp.concatenate([cos, cos], -1)
    sin64 = jnp.concatenate([-sin, sin], -1)
    ones, zeros = jnp.ones_like(cos64), jnp.zeros_like(sin64)
    cos_tabs = jnp.stack([jnp.concatenate([cos64, ones], -1), jnp.concatenate([cos64, cos64], -1)])
    sin_tabs = jnp.stack([jnp.concatenate([sin64, zeros], -1), jnp.concatenate([sin64, sin64], -1)])
    q_scale = NSA_HEAD_DIM ** -0.5 * LOG2E
    return cos_tabs, sin_tabs, cos.T * q_scale, sin.T * q_scale


def _nsa_weight_layout(w_nsa):
    nl, d, _ = w_nsa.shape
    g, dk = NSA_KV_GROUPS, NSA_HEAD_DIM
    q = w_nsa[:, :, :NSA_WIDTH]
    kv = w_nsa[:, :, NSA_WIDTH:NSA_WIDTH + 6 * NSA_KV_WIDTH].reshape(nl, d, 6, g, dk)
    kc, vc, ks, vs, kw, vw = (kv[:, :, i] for i in range(6))
    w_rows = jnp.concatenate([jnp.stack([kc, vc], 3).reshape(nl, d, 2 * g * dk),
                              jnp.stack([ks, kw], 3).reshape(nl, d, 2 * g * dk)], axis=2)
    w_v = jnp.stack([vs, vw], 3).reshape(nl, d, 2 * g * dk)
    gt = w_nsa[:, :, NSA_WIDTH + 6 * NSA_KV_WIDTH:].reshape(nl, d, g, NSA_HPG, 3).transpose(0, 1, 2, 4, 3)
    gt = jnp.pad(gt.reshape(nl, d, g, 3 * NSA_HPG), ((0, 0), (0, 0), (0, 0), (0, GATE_ROWS - 3 * NSA_HPG)))
    w_vg = jnp.concatenate([w_v, gt.reshape(nl, d, g * GATE_ROWS)], axis=2)
    return w_rows.astype(BF16), q.transpose(0, 2, 1).astype(BF16), w_vg.transpose(0, 2, 1).astype(BF16)


def _block_diag(a, b):
    top = jnp.concatenate([a, jnp.zeros(a.shape[:-1] + b.shape[-1:], a.dtype)], axis=-1)
    bot = jnp.concatenate([jnp.zeros(b.shape[:-1] + a.shape[-1:], b.dtype), b], axis=-1)
    return jnp.concatenate([top, bot], axis=-2)


def _compress_weight_layout(w1_k, w1_v, w2_k, w2_v):
    nl = w1_k.shape[0]
    per_token = lambda w: w.reshape(nl, CMP_BLOCK, NSA_HEAD_DIM, CMP_HIDDEN)
    return (_block_diag(per_token(w1_k), per_token(w1_v)).astype(BF16),
            _block_diag(w2_k, w2_v).astype(BF16))


def kernel(x, w_in, w_out, hgrn_lb_logits, hgrn_norm_w, cmp_pe_k, cmp_pe_v, cmp_w1_k, cmp_w2_k,
           cmp_w1_v, cmp_w2_v, ln1_g, ln1_b, w_up, conv_w, conv_b, w_down, ln2_g, ln2_b):
    s = x.shape[1]
    g = NSA_KV_GROUPS
    p_lb = jax.nn.softmax(hgrn_lb_logits.astype(F32), axis=0)
    lower_bounds = jnp.cumsum(p_lb, axis=0) - p_lb[0:1]
    cos_tabs, sin_tabs, cosT, sinT = _rope_tables(s)
    hw = 4 * HGRN_WIDTH
    w_hgrn_b, w_out_b, w_down_b = (w.astype(BF16) for w in (w_in[:, :, :hw], w_out, w_down))
    w_rows, w_qT, w_vgT = _nsa_weight_layout(w_in[:, :, hw:])
    cmp_w1, cmp_w2 = _compress_weight_layout(cmp_w1_k, cmp_w1_v, cmp_w2_k, cmp_w2_v)
    xf = x[0]
    xb = xf.astype(BF16)
    for l in range(DEPTH):
        xT = xb.T
        hh = _proj(xb, w_hgrn_b, l, hw, F32, 1024, 512)
        kvc = _proj_rope(xb, w_rows, l, 0, cos_tabs, sin_tabs, 0, F32, 1024, 512)
        kr = _proj_rope(xb, w_rows, l, 1, cos_tabs, sin_tabs, 1, BF16, 1024, 512, block_one_hot=True)
        qL = _projT_rope(w_qT, l, xT, cosT, sinT, NSA_TQ)
        vT, gT = _projT_vg(w_vgT, l, xT, 2 * g * NSA_HEAD_DIM, 1024)
        o_h = _hgrn(hh, lower_bounds[l][None], hgrn_norm_w[l][None], 512, 8)

        pe = jnp.concatenate([cmp_pe_k[l], cmp_pe_v[l]], axis=-1)[:, None, :]
        kv_cmp, kv_cmpT = _compress(kvc, pe, cmp_w1, cmp_w2, l)
        ocL, selT = _cmp_select(qL, kv_cmp, kv_cmpT, NSA_TQ)
        owL = _win_attn(qL, kr, vT, NSA_TQ)
        o_n = _slc_attn(qL, kr, vT, selT, ocL, owL, gT, NSA_TQ)

        xf, xb = _mix_out(o_h, o_n, w_out_b, l, xf, ln1_g[l][None], ln1_b[l][None], 512)
        u = _ffn_up(xb, w_up, l, conv_w[l], conv_b[l][None], 1024, 512)
        xf, xb = _ffn_down(u, w_down_b, l, xf, ln2_g[l][None], ln2_b[l][None], 1024, 512)
    return xf[None]
```

```python
import functools

import numpy as np
import jax
import jax.numpy as jnp
from jax import lax
from jax.experimental import pallas as pl
from jax.experimental.pallas import tpu as pltpu

F32 = jnp.float32
BF16 = jnp.bfloat16

D_MODEL = 2048
DEPTH = 4
HGRN_WIDTH = 1024
HGRN_HEADS = 8
HEAD_LANES = 128
NSA_WIDTH = 1024
NSA_HEAD_DIM = 64
NSA_HEADS = 16
NSA_KV_GROUPS = 4
NSA_HPG = 4
NSA_KV_WIDTH = NSA_KV_GROUPS * NSA_HEAD_DIM
CMP_BLOCK = 32
CMP_STRIDE = 16
CMP_HIDDEN = 256
SLC_BLOCK = 64
SLC_TOPN = 16
SLC_LANES = 128
WINDOW = 512
D_FF = 5632
ROPE_THETA = 10000.0
LN_EPS = 1e-5
RMS_EPS = 1e-6
F_MIN = 1e-30
DN_ALPHA = (2 * DEPTH) ** 0.25
NEG_INF = -1e30
FORCE_SCORE = 1e9

GATE_ROWS = 16
LOG2E = 1.4426950408889634

NSA_TQ = 512
SLC_CHUNK = 512
HGRN_CHUNK = 64
HGRN_SUB = 16
VMEM_LIMIT = 48 * 1024 * 1024
FFN_DOWN_VMEM_LIMIT = 56 * 1024 * 1024


def _params(sem):
    return pltpu.CompilerParams(dimension_semantics=sem, vmem_limit_bytes=VMEM_LIMIT)


def _dot(a, b):
    return jnp.dot(a, b, preferred_element_type=F32)


def _dot_nt(a, b):
    return lax.dot_general(a, b, (((1,), (1,)), ((), ())), preferred_element_type=F32)


def _dot_tn(a, b):
    return lax.dot_general(a, b, (((0,), (0,)), ((), ())), preferred_element_type=F32)


def _split3(x):
    hi = x.astype(BF16)
    r = x - hi.astype(F32)
    mid = r.astype(BF16)
    lo = (r - mid.astype(F32)).astype(BF16)
    return hi, mid, lo


def _sigmoid_pair(z):
    e = jnp.exp(-jnp.abs(z))
    r = 1.0 / (1.0 + e)
    er = e * r
    pos = z >= 0
    return jnp.where(pos, r, er), jnp.where(pos, er, r)


def _silu(x):
    return x * _sigmoid_pair(x)[0]


def _proj_kernel(x_ref, w_ref, o_ref, wb_ref):
    @pl.when(pl.program_id(1) == 0)
    def _():
        wb_ref[...] = w_ref[...].astype(BF16)

    o_ref[...] = _dot(x_ref[...], wb_ref[...]).astype(o_ref.dtype)


def _proj(x, w, l, n, out_dtype, tm, tn):
    m, k = x.shape
    return pl.pallas_call(
        _proj_kernel,
        grid=(n // tn, m // tm),
        in_specs=[pl.BlockSpec((tm, k), lambda j, i: (i, 0)),
                  pl.BlockSpec((None, k, tn), lambda j, i: (l, 0, j))],
        out_specs=pl.BlockSpec((tm, tn), lambda j, i: (i, j)),
        out_shape=jax.ShapeDtypeStruct((m, n), out_dtype),
        scratch_shapes=[pltpu.VMEM((k, tn), BF16)],
        compiler_params=_params(("parallel", "arbitrary")),
        name="proj",
    )(x, w)


def _proj_rope_kernel(x_ref, w_ref, cos_ref, sin_ref, o_ref, *, tn, block_one_hot):
    h = _dot(x_ref[...], w_ref[...])
    cos = cos_ref[0]
    sin = sin_ref[0]
    tm = cos.shape[0]
    lane = lax.broadcasted_iota(jnp.int32, cos.shape, 1)
    first_half = (lane % NSA_HEAD_DIM) < (NSA_HEAD_DIM // 2)
    if block_one_hot:
        pos = pl.program_id(0) * tm + lax.broadcasted_iota(jnp.int32, cos.shape, 0)
        one_hot = (lane == pos // SLC_BLOCK).astype(o_ref.dtype)
    for c in range(tn // 128):
        hc = h[:, c * 128:(c + 1) * 128]
        rot = jnp.where(first_half, pltpu.roll(hc, 96, axis=1), pltpu.roll(hc, 32, axis=1))
        roped = (hc * cos + rot * sin).astype(o_ref.dtype)
        if block_one_hot:
            o_ref[:, 2 * c * 128:(2 * c + 1) * 128] = one_hot
            o_ref[:, (2 * c + 1) * 128:(2 * c + 2) * 128] = roped
        else:
            o_ref[:, c * 128:(c + 1) * 128] = roped


def _proj_rope(x, w, l, col0, cos_tabs, sin_tabs, tab, out_dtype, tm, tn, block_one_hot=False):
    m, k = x.shape
    n_out = 2 * tn if block_one_hot else tn
    tab_spec = pl.BlockSpec((1, tm, 128), lambda i: (tab, i, 0))
    return pl.pallas_call(
        functools.partial(_proj_rope_kernel, tn=tn, block_one_hot=block_one_hot),
        grid=(m // tm,),
        in_specs=[pl.BlockSpec((tm, k), lambda i: (i, 0)),
                  pl.BlockSpec((None, k, tn), lambda i: (l, 0, col0)),
                  tab_spec, tab_spec],
        out_specs=pl.BlockSpec((tm, n_out), lambda i: (i, 0)),
        out_shape=jax.ShapeDtypeStruct((m, n_out), out_dtype),
        compiler_params=_params(("parallel",)),
        name="proj_rope",
    )(x, w, cos_tabs, sin_tabs)


def _hgrn_chunk(q, z, v, g, lb, nw, st, tri, sub_row):
    C, c = HGRN_CHUNK, HGRN_SUB
    one_minus_lb = 1.0 - lb
    sig, sig_neg = _sigmoid_pair(z)
    f = lb + one_minus_lb * sig
    lf = jnp.log2(jnp.maximum(f, F_MIN))
    k = one_minus_lb * sig_neg
    qs = _silu(q)
    lf_hi, lf_mid, lf_lo = _split3(lf)
    b = _dot(tri, lf_hi) + _dot(tri, lf_mid) + _dot(tri, lf_lo)
    o_inter = _dot_nt((qs * jnp.exp2(b)).astype(BF16), st.astype(BF16))
    v16 = v.astype(BF16)
    outs = []
    for i in range(C // c):
        lo = i * c
        b_i = b[lo:lo + c]
        qs_i = qs[lo:lo + c]
        k_i = k[lo:lo + c]
        v_i = v[lo:lo + c]
        o_i = o_inter[lo:lo + c]
        if i > 0:
            beta = b[lo - 1:lo]
            qt = (qs_i * jnp.exp2(b_i - beta)).astype(BF16)
            kt = (k[:lo] * jnp.exp2(beta - b[:lo])).astype(BF16)
            att = _dot_nt(qt, kt)
            o_i = o_i + _dot(att.astype(BF16), v16[:lo])
        for s in range(c):
            d = jnp.exp2(b_i - b_i[s:s + 1])
            w = jnp.sum(qs_i * d * k_i[s:s + 1], axis=-1, keepdims=True)
            w = jnp.where(sub_row >= s, w, 0.0)
            o_i = o_i + w * v_i[s:s + 1]
        outs.append(o_i)
    o = jnp.concatenate(outs, axis=0)
    b_last = b[C - 1:C]
    kd = (k * jnp.exp2(b_last - b)).astype(BF16)
    st_new = jnp.exp2(b_last) * st + _dot_tn(v16, kd)
    o = o * lax.rsqrt(jnp.mean(o * o, axis=-1, keepdims=True) + RMS_EPS)
    return o * nw * _silu(g), st_new


def _hgrn_kernel(q_ref, z_ref, v_ref, g_ref, lb_ref, nw_ref, o_ref, st_ref, *, tb, hb):
    C, c = HGRN_CHUNK, HGRN_SUB

    @pl.when(pl.program_id(1) == 0)
    def _():
        st_ref[...] = jnp.zeros_like(st_ref)

    tri = (lax.broadcasted_iota(jnp.int32, (C, C), 1)
           <= lax.broadcasted_iota(jnp.int32, (C, C), 0)).astype(BF16)
    sub_row = lax.broadcasted_iota(jnp.int32, (c, 1), 0)

    def chunk(ci, carry):
        r0 = pl.multiple_of(ci * C, C)
        for h in range(hb):
            lanes = slice(h * HEAD_LANES, (h + 1) * HEAD_LANES)
            out, st_new = _hgrn_chunk(q_ref[pl.ds(r0, C), lanes], z_ref[pl.ds(r0, C), lanes],
                                      v_ref[pl.ds(r0, C), lanes], g_ref[pl.ds(r0, C), lanes],
                                      lb_ref[:, lanes], nw_ref[:, lanes], st_ref[h], tri, sub_row)
            st_ref[h] = st_new
            o_ref[pl.ds(r0, C), lanes] = out.astype(o_ref.dtype)
        return carry

    lax.fori_loop(0, tb // C, chunk, 0)


def _hgrn(hh, lb, nw, tb, hb):
    s = hh.shape[0]
    nhb = HGRN_HEADS // hb
    wide = hb * HEAD_LANES

    def col(off):
        return pl.BlockSpec((tb, wide), lambda h, t: (t, off * nhb + h))

    vec = pl.BlockSpec((1, wide), lambda h, t: (0, h))
    return pl.pallas_call(
        functools.partial(_hgrn_kernel, tb=tb, hb=hb),
        grid=(nhb, s // tb),
        in_specs=[col(0), col(1), col(2), col(3), vec, vec],
        out_specs=pl.BlockSpec((tb, wide), lambda h, t: (t, h)),
        out_shape=jax.ShapeDtypeStruct((s, HGRN_WIDTH), BF16),
        scratch_shapes=[pltpu.VMEM((hb, HEAD_LANES, HEAD_LANES), F32)],
        compiler_params=_params(("parallel", "arbitrary")),
        name="hgrn2",
    )(hh, hh, hh, hh, lb, nw)


def _projT_rope_kernel(w_ref, x_ref, cos_ref, sin_ref, o_ref, *, tq):
    h = _dot_nt(w_ref[...], x_ref[...])
    cos = cos_ref[...]
    sin = sin_ref[...]
    half = NSA_HEAD_DIM // 2
    for hp in range(NSA_HPG):
        r = hp * NSA_HEAD_DIM
        x1 = h[r:r + half]
        x2 = h[r + half:r + 2 * half]
        o_ref[0, :half, hp * tq:(hp + 1) * tq] = (x1 * cos - x2 * sin).astype(o_ref.dtype)
        o_ref[0, half:, hp * tq:(hp + 1) * tq] = (x2 * cos + x1 * sin).astype(o_ref.dtype)


def _projT_rope(wT, l, x, cosT, sinT, tq):
    _, n, d = wT.shape
    s = x.shape[0]
    half = NSA_HEAD_DIM // 2
    gw = NSA_HPG * NSA_HEAD_DIM
    return pl.pallas_call(
        functools.partial(_projT_rope_kernel, tq=tq),
        grid=(s // tq, n // gw),
        in_specs=[pl.BlockSpec((None, gw, d), lambda i, j: (l, j, 0)),
                  pl.BlockSpec((tq, d), lambda i, j: (i, 0)),
                  pl.BlockSpec((half, tq), lambda i, j: (0, i)),
                  pl.BlockSpec((half, tq), lambda i, j: (0, i))],
        out_specs=pl.BlockSpec((1, NSA_HEAD_DIM, NSA_HPG * tq), lambda i, j: (j, 0, i)),
        out_shape=jax.ShapeDtypeStruct((n // gw, NSA_HEAD_DIM, NSA_HPG * s), BF16),
        compiler_params=_params(("parallel", "arbitrary")),
        name="projT_rope",
    )(wT, x, cosT, sinT)


def _projT_vg_kernel(w_ref, x_ref, v_ref, g_ref):
    h = _dot_nt(w_ref[...], x_ref[...])
    nv = v_ref.shape[0]
    v_ref[...] = h[:nv].astype(v_ref.dtype)
    g_ref[...] = h[nv:]


def _projT_vg(wT, l, x, nv, ts):
    _, n, d = wT.shape
    s = x.shape[0]
    return pl.pallas_call(
        _projT_vg_kernel,
        grid=(s // ts,),
        in_specs=[pl.BlockSpec((None, n, d), lambda i: (l, 0, 0)),
                  pl.BlockSpec((ts, d), lambda i: (i, 0))],
        out_specs=[pl.BlockSpec((nv, ts), lambda i: (0, i)),
                   pl.BlockSpec((n - nv, ts), lambda i: (0, i))],
        out_shape=[jax.ShapeDtypeStruct((nv, s), BF16), jax.ShapeDtypeStruct((n - nv, s), F32)],
        compiler_params=_params(("parallel",)),
        name="projT_vg",
    )(wT, x)


def _compress_kernel(x_ref, pe_ref, w1_ref, w2_ref, kv_ref, kvT_ref):
    n_blk = x_ref.shape[0] // CMP_STRIDE
    width = 2 * CMP_HIDDEN
    top = jnp.zeros((n_blk, width), F32)
    bot = jnp.zeros((n_blk, width), F32)
    for j in range(CMP_STRIDE):
        rows = x_ref[pl.ds(j, n_blk, stride=CMP_STRIDE), :]
        top = top + _dot((rows + pe_ref[j]).astype(BF16), w1_ref[0, j])
        bot = bot + _dot((rows + pe_ref[CMP_STRIDE + j]).astype(BF16), w1_ref[0, CMP_STRIDE + j])
    hid = top + pltpu.roll(bot, n_blk - 1, axis=0)
    out = _dot(_silu(hid).astype(BF16), w2_ref[0])
    kv_ref[0] = out.astype(kv_ref.dtype)
    kvT_ref[0] = out.T.astype(kvT_ref.dtype)


def _compress(x, pe, w1, w2, l):
    s = x.shape[0]
    g = NSA_KV_GROUPS
    n_blk = s // CMP_STRIDE
    return pl.pallas_call(
        _compress_kernel,
        grid=(g,),
        in_specs=[pl.BlockSpec((s, 128), lambda gi: (0, gi)),
                  pl.BlockSpec(pe.shape, lambda gi: (0, 0, 0)),
                  pl.BlockSpec((1,) + w1.shape[1:], lambda gi: (l, 0, 0, 0)),
                  pl.BlockSpec((1,) + w2.shape[1:], lambda gi: (l, 0, 0))],
        out_specs=[pl.BlockSpec((1, n_blk, 128), lambda gi: (gi, 0, 0)),
                   pl.BlockSpec((1, 128, n_blk), lambda gi: (gi, 0, 0))],
        out_shape=[jax.ShapeDtypeStruct((g, n_blk, 128), BF16),
                   jax.ShapeDtypeStruct((g, 128, n_blk), BF16)],
        compiler_params=_params(("parallel",)),
        name="compress",
    )(x, pe, w1, w2)


ONES_ROWS = 16


def _with_ones_rows(vT):
    return jnp.concatenate([vT, jnp.ones((ONES_ROWS, vT.shape[1]), vT.dtype)], axis=0)


def _per_head(x, tq):
    return [x[:, hp * tq:(hp + 1) * tq] for hp in range(NSA_HPG)]


CMP_ROW_STEP = 128
PICKED = -3e38


def _cmp_select_kernel(q_ref, kc_ref, vcT_ref, oc_ref, selT_ref, imp_ref, *, tq):
    n_blk = kc_ref.shape[1]
    qi = pl.program_id(1)
    t = qi * tq + lax.broadcasted_iota(jnp.int32, (1, tq), 1)
    any_visible = (t >= CMP_BLOCK - 1).astype(F32)

    def attend(rows):
        kc = kc_ref[0, :rows, :NSA_HEAD_DIM]
        vcT = vcT_ref[0, :, :rows]
        n_idx = lax.broadcasted_iota(jnp.int32, (rows, 1), 0)
        bias = jnp.where((n_idx * CMP_STRIDE + (CMP_BLOCK - 1)) <= t, 0.0, NEG_INF)
        s = _dot(kc, q_ref[0])
        s = jnp.concatenate([x + bias for x in _per_head(s, tq)], axis=1)
        e = jnp.exp2(s - jnp.max(s, axis=0, keepdims=True))
        scale = jnp.concatenate([any_visible] * NSA_HPG, axis=1) / jnp.sum(e, axis=0, keepdims=True)
        p = e * scale
        oc_ref[0] = _dot(vcT, p.astype(BF16))
        p_sum = functools.reduce(lambda a, b: a + b, _per_head(p, tq))
        sj = lax.broadcasted_iota(jnp.int32, (SLC_LANES, rows), 0) * SLC_BLOCK
        ci = lax.broadcasted_iota(jnp.int32, (SLC_LANES, rows), 1) * CMP_STRIDE
        overlap = ((ci < sj + SLC_BLOCK) & (ci + CMP_BLOCK > sj)).astype(BF16)
        p_hi, p_mid, p_lo = _split3(p_sum)
        imp_ref[...] = _dot(overlap, p_hi) + _dot(overlap, p_mid) + _dot(overlap, p_lo)

    n_visible = (qi + 1) * (tq // CMP_STRIDE)
    n_steps = n_blk // CMP_ROW_STEP
    for b in range(1, n_steps + 1):
        below = n_visible <= b * CMP_ROW_STEP
        above = n_visible > (b - 1) * CMP_ROW_STEP
        pl.when(above & below if b < n_steps else above)(functools.partial(attend, b * CMP_ROW_STEP))

    blk = lax.broadcasted_iota(jnp.int32, (SLC_LANES, 1), 0)
    cur = t // SLC_BLOCK
    forced = (blk == 0) | (blk == cur) | (blk == cur - 1)
    causal = blk * SLC_BLOCK <= t
    score = jnp.where(forced, FORCE_SCORE, jnp.where(causal, imp_ref[...], -1.0))
    blk_f = blk.astype(F32)
    work = score
    for _ in range(SLC_TOPN):
        mx = jnp.max(work, axis=0, keepdims=True)
        first = jnp.min(jnp.where(work == mx, blk_f, float(SLC_LANES)), axis=0, keepdims=True)
        work = jnp.where(blk_f == first, PICKED, work)
    selT_ref[0] = jnp.where((work == PICKED) & (score >= 0.0), 1.0, 0.0).astype(selT_ref.dtype)


def _cmp_select(qL, k_cmp, v_cmpT, tq):
    g = NSA_KV_GROUPS
    s = qL.shape[2] // NSA_HPG
    n_blk = k_cmp.shape[1]
    q_spec = pl.BlockSpec((1, NSA_HEAD_DIM, NSA_HPG * tq), lambda gi, qi: (gi, 0, qi))
    return pl.pallas_call(
        functools.partial(_cmp_select_kernel, tq=tq),
        grid=(g, s // tq),
        in_specs=[q_spec,
                  pl.BlockSpec((1, n_blk, 2 * NSA_HEAD_DIM), lambda gi, qi: (gi, 0, 0)),
                  pl.BlockSpec((1, NSA_HEAD_DIM, n_blk), lambda gi, qi: (gi, 1, 0))],
        out_specs=[q_spec,
                   pl.BlockSpec((1, SLC_LANES, tq), lambda gi, qi: (gi, 0, qi))],
        out_shape=[jax.ShapeDtypeStruct(qL.shape, F32),
                   jax.ShapeDtypeStruct((g, SLC_LANES, s), BF16)],
        scratch_shapes=[pltpu.VMEM((SLC_LANES, tq), F32)],
        compiler_params=_params(("parallel", "arbitrary")),
        name="cmp_select",
    )(qL, k_cmp, v_cmpT)


def _win_attn_kernel(q_ref, k0_ref, k1_ref, v0_ref, v1_ref, ow_ref, *, tq):
    dk = NSA_HEAD_DIM
    qi = pl.program_id(1)
    q = q_ref[0]
    t = qi * tq + lax.broadcasted_iota(jnp.int32, (1, tq), 1)
    s = []
    for c, k_ref in enumerate((k0_ref, k1_ref)):
        j = (qi - 1 + c) * tq + lax.broadcasted_iota(jnp.int32, (tq, 1), 0)
        bias = jnp.where((j >= 0) & (j <= t) & (t - j < WINDOW), 0.0, NEG_INF)
        sc = _dot(k_ref[:, dk:], q)
        s.append(jnp.concatenate([x + bias for x in _per_head(sc, tq)], axis=1))
    m = jnp.maximum(jnp.max(s[0], axis=0, keepdims=True), jnp.max(s[1], axis=0, keepdims=True))
    o = (_dot(_with_ones_rows(v0_ref[...]), jnp.exp2(s[0] - m).astype(BF16))
         + _dot(_with_ones_rows(v1_ref[...]), jnp.exp2(s[1] - m).astype(BF16)))
    ow_ref[0] = o[:dk] / o[dk:dk + 1]


def _win_attn(qL, kr, vT, tq):
    g = NSA_KV_GROUPS
    s = qL.shape[2] // NSA_HPG
    assert WINDOW == tq
    q_spec = pl.BlockSpec((1, NSA_HEAD_DIM, NSA_HPG * tq), lambda gi, qi: (gi, 0, qi))

    def k_spec(c):
        return pl.BlockSpec((tq, 128), lambda gi, qi: (jnp.maximum(qi - 1 + c, 0), 2 * gi + 1))

    def v_spec(c):
        return pl.BlockSpec((NSA_HEAD_DIM, tq), lambda gi, qi: (2 * gi + 1, jnp.maximum(qi - 1 + c, 0)))

    return pl.pallas_call(
        functools.partial(_win_attn_kernel, tq=tq),
        grid=(g, s // tq),
        in_specs=[q_spec, k_spec(0), k_spec(1), v_spec(0), v_spec(1)],
        out_specs=q_spec,
        out_shape=jax.ShapeDtypeStruct(qL.shape, F32),
        compiler_params=_params(("parallel", "arbitrary")),
        name="win_attn",
    )(qL, kr, kr, vT, vT)


def _slc_attn_kernel(qi_tab, ki_tab, q_ref, k_ref, vT_ref, selT_ref, oc_ref, ow_ref, gT_ref,
                     o_ref, rhs_ref, m_ref, acc_ref, *, tq):
    dk = NSA_HEAD_DIM
    step = pl.program_id(1)
    qi = qi_tab[step]
    ki = ki_tab[step]

    @pl.when(ki == 0)
    def _():
        m_ref[...] = jnp.full_like(m_ref, NEG_INF)
        acc_ref[...] = jnp.zeros_like(acc_ref)
        not_selected = ((selT_ref[0].astype(F32) - 1.0) * (-NEG_INF)).astype(BF16)
        rhs_ref[:SLC_LANES, :] = jnp.concatenate([not_selected] * NSA_HPG, axis=1)
        rhs_ref[SLC_LANES:SLC_LANES + dk, :] = q_ref[0]
        rhs_ref[SLC_LANES + dk:, :] = jnp.zeros((dk, NSA_HPG * tq), BF16)

    def update(sub_tiles):
        chains = [(sub, hp, causal) for sub, causal in sub_tiles for hp in range(NSA_HPG)]
        v_aug = {sub: _with_ones_rows(vT_ref[:, sub * tq:(sub + 1) * tq]) for sub, _ in sub_tiles}
        scores = lambda sub, hp: _dot(k_ref[sub * tq:(sub + 1) * tq, :], rhs_ref[:, hp * tq:(hp + 1) * tq])
        s_next = scores(*chains[0][:2])
        for n, (sub, hp, causal) in enumerate(chains):
            lanes = slice(hp * tq, (hp + 1) * tq)
            s = s_next
            if n + 1 < len(chains):
                s_next = scores(*chains[n + 1][:2])
            if causal:
                s = jnp.where(lax.broadcasted_iota(jnp.int32, (tq, 1), 0)
                              <= lax.broadcasted_iota(jnp.int32, (1, tq), 1), s, NEG_INF)
            m_old = m_ref[:, lanes]
            m_new = jnp.maximum(m_old, jnp.max(s, axis=0, keepdims=True))
            p = jnp.exp2(s - m_new).astype(BF16)
            acc_ref[:, lanes] = jnp.exp2(m_old - m_new) * acc_ref[:, lanes] + _dot(v_aug[sub], p)
            m_ref[:, lanes] = m_new

    def finish():
        sig = _sigmoid_pair(gT_ref[...])[0]

        def gate(br):
            return jnp.concatenate([sig[br * NSA_HPG + hp:br * NSA_HPG + hp + 1] for hp in range(NSA_HPG)], axis=1)

        o_s = acc_ref[:dk, :] / acc_ref[dk:dk + 1, :]
        out = gate(0) * oc_ref[0] + gate(1) * o_s + gate(2) * ow_ref[0]
        o_ref[...] = jnp.concatenate(_per_head(out, tq), axis=0).T.astype(o_ref.dtype)

    ahead = qi - 2 * ki

    @pl.when(ahead >= 2)
    def _():
        update([(0, False), (1, False)])

    @pl.when(ahead == 1)
    def _():
        update([(0, False), (1, True)])
        finish()

    @pl.when(ahead == 0)
    def _():
        update([(0, True)])
        finish()


def _slc_attn(qL, kr, vT, selT, ocL, owL, gT, tq):
    g = NSA_KV_GROUPS
    s = qL.shape[2] // NSA_HPG
    tk = 2 * tq
    pairs = [(qi, ki) for qi in range(s // tq) for ki in range(qi // 2 + 1)]
    qi_tab = jnp.asarray(np.array([p[0] for p in pairs], np.int32))
    ki_tab = jnp.asarray(np.array([p[1] for p in pairs], np.int32))
    q_spec = pl.BlockSpec((1, NSA_HEAD_DIM, NSA_HPG * tq), lambda gi, st, qt, kt: (gi, 0, qt[st]))
    grid_spec = pltpu.PrefetchScalarGridSpec(
        num_scalar_prefetch=2,
        grid=(g, len(pairs)),
        in_specs=[q_spec,
                  pl.BlockSpec((tk, 256), lambda gi, st, qt, kt: (kt[st], gi)),
                  pl.BlockSpec((NSA_HEAD_DIM, tk), lambda gi, st, qt, kt: (2 * gi, kt[st])),
                  pl.BlockSpec((1, SLC_LANES, tq), lambda gi, st, qt, kt: (gi, 0, qt[st])),
                  q_spec, q_spec,
                  pl.BlockSpec((GATE_ROWS, tq), lambda gi, st, qt, kt: (gi, qt[st]))],
        out_specs=pl.BlockSpec((tq, NSA_HPG * NSA_HEAD_DIM), lambda gi, st, qt, kt: (qt[st], gi)),
        scratch_shapes=[pltpu.VMEM((SLC_LANES + 2 * NSA_HEAD_DIM, NSA_HPG * tq), BF16),
                        pltpu.VMEM((1, NSA_HPG * tq), F32),
                        pltpu.VMEM((NSA_HEAD_DIM + ONES_ROWS, NSA_HPG * tq), F32)])
    return pl.pallas_call(
        functools.partial(_slc_attn_kernel, tq=tq),
        grid_spec=grid_spec,
        out_shape=jax.ShapeDtypeStruct((s, NSA_WIDTH), BF16),
        compiler_params=_params(("parallel", "arbitrary")),
        name="slc_attn",
    )(qi_tab, ki_tab, qL, kr, vT, selT, ocL, owL, gT)


def _res_ln_epilogue(acc, x_ref, g_ref, b_ref, xo_ref, xb_ref):
    v = DN_ALPHA * x_ref[...] + acc
    mu = jnp.mean(v, axis=-1, keepdims=True)
    d = v - mu
    var = jnp.mean(d * d, axis=-1, keepdims=True)
    y = d * lax.rsqrt(var + LN_EPS) * g_ref[...] + b_ref[...]
    xo_ref[...] = y
    xb_ref[...] = y.astype(BF16)


def _mix_out_kernel(a1_ref, a2_ref, w_ref, x_ref, g_ref, b_ref, xo_ref, xb_ref):
    k1 = a1_ref.shape[1]
    y = _dot(a1_ref[...], w_ref[:k1, :]) + _dot(a2_ref[...], w_ref[k1:, :])
    _res_ln_epilogue(y, x_ref, g_ref, b_ref, xo_ref, xb_ref)


def _mix_out(a1, a2, w, l, x, gamma, beta, tm):
    s, k1 = a1.shape
    n = w.shape[2]
    row = pl.BlockSpec((tm, n), lambda i: (i, 0))
    vec = pl.BlockSpec((1, n), lambda i: (0, 0))
    return pl.pallas_call(
        _mix_out_kernel,
        grid=(s // tm,),
        in_specs=[pl.BlockSpec((tm, k1), lambda i: (i, 0)),
                  pl.BlockSpec((tm, k1), lambda i: (i, 0)),
                  pl.BlockSpec((None,) + w.shape[1:], lambda i: (l, 0, 0)),
                  row, vec, vec],
        out_specs=[row, row],
        out_shape=[jax.ShapeDtypeStruct((s, n), F32), jax.ShapeDtypeStruct((s, n), BF16)],
        compiler_params=_params(("parallel",)),
        name="mix_out",
    )(a1, a2, w, x, gamma, beta)


def _ffn_down_kernel(a_ref, w_ref, x_ref, g_ref, b_ref, xo_ref, xb_ref, acc_ref):
    k = pl.program_id(1)

    @pl.when(k == 0)
    def _():
        acc_ref[...] = _dot(a_ref[...], w_ref[...])

    @pl.when(k > 0)
    def _():
        acc_ref[...] += _dot(a_ref[...], w_ref[...])

    @pl.when(k == pl.num_programs(1) - 1)
    def _():
        _res_ln_epilogue(acc_ref[...], x_ref, g_ref, b_ref, xo_ref, xb_ref)


def _ffn_down(a, w, l, x, gamma, beta, tm, tk):
    s, kk = a.shape
    n = w.shape[2]
    row = pl.BlockSpec((tm, n), lambda i, k: (i, 0))
    vec = pl.BlockSpec((1, n), lambda i, k: (0, 0))
    res = pl.BlockSpec((tm, n), lambda i, k: (i, 0), pipeline_mode=pl.Buffered(1))
    return pl.pallas_call(
        _ffn_down_kernel,
        grid=(s // tm, kk // tk),
        in_specs=[pl.BlockSpec((tm, tk), lambda i, k: (i, k)),
                  pl.BlockSpec((None, tk, n), lambda i, k: (l, k, 0)),
                  res, vec, vec],
        out_specs=[row, row],
        out_shape=[jax.ShapeDtypeStruct((s, n), F32), jax.ShapeDtypeStruct((s, n), BF16)],
        scratch_shapes=[pltpu.VMEM((tm, n), F32)],
        compiler_params=pltpu.CompilerParams(dimension_semantics=("parallel", "arbitrary"),
                                             vmem_limit_bytes=FFN_DOWN_VMEM_LIMIT),
        name="ffn_down",
    )(a, w, x, gamma, beta)


HALO = 16


def _ffn_up_kernel(a_ref, ap_ref, wg_ref, wu_ref, cwg_ref, cwu_ref, cbg_ref, cbu_ref, o_ref, wgb_ref, wub_ref):
    i = pl.program_id(1)

    @pl.when(i == 0)
    def _():
        wgb_ref[...] = wg_ref[...].astype(BF16)
        wub_ref[...] = wu_ref[...].astype(BF16)

    ap = ap_ref[...]
    ap = jnp.where(i > 0, ap, jnp.zeros_like(ap))
    a = jnp.concatenate([ap, a_ref[...]], axis=0)

    def branch(w_ref, cw_ref, cb_ref):
        h = _dot(a, w_ref[...])
        cw = cw_ref[...]
        return (cw[2:3] * h[HALO:] + cw[1:2] * h[HALO - 1:-1] + cw[0:1] * h[HALO - 2:-2]) + cb_ref[...]

    gate = branch(wgb_ref, cwg_ref, cbg_ref)
    up = branch(wub_ref, cwu_ref, cbu_ref)
    o_ref[...] = (_silu(gate) * up).astype(o_ref.dtype)


def _ffn_up(a, w, l, conv_w, conv_b, tm, tn):
    s, d = a.shape
    nj = D_FF // tn
    return pl.pallas_call(
        _ffn_up_kernel,
        grid=(nj, s // tm),
        in_specs=[pl.BlockSpec((tm, d), lambda j, i: (i, 0)),
                  pl.BlockSpec((HALO, d), lambda j, i: (jnp.maximum(i * (tm // HALO) - 1, 0), 0)),
                  pl.BlockSpec((None, d, tn), lambda j, i: (l, 0, j)),
                  pl.BlockSpec((None, d, tn), lambda j, i: (l, 0, j + nj)),
                  pl.BlockSpec((3, tn), lambda j, i: (0, j)),
                  pl.BlockSpec((3, tn), lambda j, i: (0, j + nj)),
                  pl.BlockSpec((1, tn), lambda j, i: (0, j)),
                  pl.BlockSpec((1, tn), lambda j, i: (0, j + nj))],
        out_specs=pl.BlockSpec((tm, tn), lambda j, i: (i, j)),
        out_shape=jax.ShapeDtypeStruct((s, D_FF), BF16),
        scratch_shapes=[pltpu.VMEM((d, tn), BF16), pltpu.VMEM((d, tn), BF16)],
        compiler_params=_params(("parallel", "arbitrary")),
        name="ffn_up",
    )(a, a, w, w, conv_w, conv_w, conv_b, conv_b)


def _rope_tables(s):
    half = NSA_HEAD_DIM // 2
    inv = ROPE_THETA ** (-jnp.arange(half, dtype=F32) / half)
    ang = jnp.arange(s).astype(F32)[:, None] * inv[None, :]
    cos, sin = jnp.cos(ang), jnp.sin(ang)
    cos64 = jnp.concatenate([cos, cos], -1)
    sin64 = jnp.concatenate([-sin, sin], -1)
    ones, zeros = jnp.ones_like(cos64), jnp.zeros_like(sin64)
    cos_tabs = jnp.stack([jnp.concatenate([cos64, ones], -1), jnp.concatenate([cos64, cos64], -1)])
    sin_tabs = jnp.stack([jnp.concatenate([sin64, zeros], -1), jnp.concatenate([sin64, sin64], -1)])
    q_scale = NSA_HEAD_DIM ** -0.5 * LOG2E
    return cos_tabs, sin_tabs, cos.T * q_scale, sin.T * q_scale


def _nsa_weight_layout(w_nsa):
    nl, d, _ = w_nsa.shape
    g, dk = NSA_KV_GROUPS, NSA_HEAD_DIM
    q = w_nsa[:, :, :NSA_WIDTH]
    kv = w_nsa[:, :, NSA_WIDTH:NSA_WIDTH + 6 * NSA_KV_WIDTH].reshape(nl, d, 6, g, dk)
    kc, vc, ks, vs, kw, vw = (kv[:, :, i] for i in range(6))
    w_rows = jnp.concatenate([jnp.stack([kc, vc], 3).reshape(nl, d, 2 * g * dk),
                              jnp.stack([ks, kw], 3).reshape(nl, d, 2 * g * dk)], axis=2)
    w_v = jnp.stack([vs, vw], 3).reshape(nl, d, 2 * g * dk)
    gt = w_nsa[:, :, NSA_WIDTH + 6 * NSA_KV_WIDTH:].reshape(nl, d, g, NSA_HPG, 3).transpose(0, 1, 2, 4, 3)
    gt = jnp.pad(gt.reshape(nl, d, g, 3 * NSA_HPG), ((0, 0), (0, 0), (0, 0), (0, GATE_ROWS - 3 * NSA_HPG)))
    w_vg = jnp.concatenate([w_v, gt.reshape(nl, d, g * GATE_ROWS)], axis=2)
    return w_rows.astype(BF16), q.transpose(0, 2, 1).astype(BF16), w_vg.transpose(0, 2, 1).astype(BF16)


def _block_diag(a, b):
    top = jnp.concatenate([a, jnp.zeros(a.shape[:-1] + b.shape[-1:], a.dtype)], axis=-1)
    bot = jnp.concatenate([jnp.zeros(b.shape[:-1] + a.shape[-1:], b.dtype), b], axis=-1)
    return jnp.concatenate([top, bot], axis=-2)


def _compress_weight_layout(w1_k, w1_v, w2_k, w2_v):
    nl = w1_k.shape[0]
    per_token = lambda w: w.reshape(nl, CMP_BLOCK, NSA_HEAD_DIM, CMP_HIDDEN)
    return (_block_diag(per_token(w1_k), per_token(w1_v)).astype(BF16),
            _block_diag(w2_k, w2_v).astype(BF16))


def kernel(x, w_in, w_out, hgrn_lb_logits, hgrn_norm_w, cmp_pe_k, cmp_pe_v, cmp_w1_k, cmp_w2_k,
           cmp_w1_v, cmp_w2_v, ln1_g, ln1_b, w_up, conv_w, conv_b, w_down, ln2_g, ln2_b):
    s = x.shape[1]
    g = NSA_KV_GROUPS
    p_lb = jax.nn.softmax(hgrn_lb_logits.astype(F32), axis=0)
    lower_bounds = jnp.cumsum(p_lb, axis=0) - p_lb[0:1]
    cos_tabs, sin_tabs, cosT, sinT = _rope_tables(s)
    hw = 4 * HGRN_WIDTH
    w_out_b, w_down_b = w_out.astype(BF16), w_down.astype(BF16)
    w_rows, w_qT, w_vgT = _nsa_weight_layout(w_in[:, :, hw:])
    cmp_w1, cmp_w2 = _compress_weight_layout(cmp_w1_k, cmp_w1_v, cmp_w2_k, cmp_w2_v)
    xf = x[0]
    xb = xf.astype(BF16)
    for l in range(DEPTH):
        hh = _proj(xb, w_in, l, hw, F32, 1024, 512)
        kvc = _proj_rope(xb, w_rows, l, 0, cos_tabs, sin_tabs, 0, F32, 1024, 512)
        kr = _proj_rope(xb, w_rows, l, 1, cos_tabs, sin_tabs, 1, BF16, 1024, 512, block_one_hot=True)
        qL = _projT_rope(w_qT, l, xb, cosT, sinT, NSA_TQ)
        vT, gT = _projT_vg(w_vgT, l, xb, 2 * g * NSA_HEAD_DIM, 1024)
        o_h = _hgrn(hh, lower_bounds[l][None], hgrn_norm_w[l][None], 512, 8)

        pe = jnp.concatenate([cmp_pe_k[l], cmp_pe_v[l]], axis=-1)[:, None, :]
        kv_cmp, kv_cmpT = _compress(kvc, pe, cmp_w1, cmp_w2, l)
        ocL, selT = _cmp_select(qL, kv_cmp, kv_cmpT, NSA_TQ)
        owL = _win_attn(qL, kr, vT, NSA_TQ)
        o_n = _slc_attn(qL, kr, vT, selT, ocL, owL, gT, NSA_TQ)

        xf, xb = _mix_out(o_h, o_n, w_out_b, l, xf, ln1_g[l][None], ln1_b[l][None], 512)
        u = _ffn_up(xb, w_up, l, conv_w[l], conv_b[l][None], 1024, 512)
        xf, xb = _ffn_down(u, w_down_b, l, xf, ln2_g[l][None], ln2_b[l][None], 1024, 512)
    return xf[None]
```

```python
import functools

import numpy as np
import jax
import jax.numpy as jnp
from jax import lax
from jax.experimental import pallas as pl
from jax.experimental.pallas import tpu as pltpu

F32 = jnp.float32
BF16 = jnp.bfloat16

D_MODEL = 2048
DEPTH = 4
HGRN_WIDTH = 1024
HGRN_HEADS = 8
HEAD_LANES = 128
NSA_WIDTH = 1024
NSA_HEAD_DIM = 64
NSA_HEADS = 16
NSA_KV_GROUPS = 4
NSA_HPG = 4
NSA_KV_WIDTH = NSA_KV_GROUPS * NSA_HEAD_DIM
CMP_BLOCK = 32
CMP_STRIDE = 16
CMP_HIDDEN = 256
SLC_BLOCK = 64
SLC_TOPN = 16
SLC_LANES = 128
WINDOW = 512
D_FF = 5632
ROPE_THETA = 10000.0
LN_EPS = 1e-5
RMS_EPS = 1e-6
F_MIN = 1e-30
DN_ALPHA = (2 * DEPTH) ** 0.25
NEG_INF = -1e30
FORCE_SCORE = 1e9

GATE_ROWS = 16
LOG2E = 1.4426950408889634

NSA_TQ = 512
HGRN_CHUNK = 64
HGRN_SUB = 16
VMEM_LIMIT = 48 * 1024 * 1024
FFN_DOWN_VMEM_LIMIT = 56 * 1024 * 1024


def _params(sem):
    return pltpu.CompilerParams(dimension_semantics=sem, vmem_limit_bytes=VMEM_LIMIT)


def _dot(a, b):
    return jnp.dot(a, b, preferred_element_type=F32)


def _dot_nt(a, b):
    return lax.dot_general(a, b, (((1,), (1,)), ((), ())), preferred_element_type=F32)


def _dot_tn(a, b):
    return lax.dot_general(a, b, (((0,), (0,)), ((), ())), preferred_element_type=F32)


def _split3(x):
    hi = x.astype(BF16)
    r = x - hi.astype(F32)
    mid = r.astype(BF16)
    lo = (r - mid.astype(F32)).astype(BF16)
    return hi, mid, lo


def _sigmoid_pair(z):
    e = jnp.exp(-jnp.abs(z))
    r = 1.0 / (1.0 + e)
    er = e * r
    pos = z >= 0
    return jnp.where(pos, r, er), jnp.where(pos, er, r)


def _silu(x):
    return x * _sigmoid_pair(x)[0]


def _proj_kernel(x_ref, w_ref, o_ref):
    o_ref[...] = _dot(x_ref[...], w_ref[...]).astype(o_ref.dtype)


def _proj(x, w, l, n, out_dtype, tm, tn):
    m, k = x.shape
    return pl.pallas_call(
        _proj_kernel,
        grid=(m // tm, n // tn),
        in_specs=[pl.BlockSpec((tm, k), lambda i, j: (i, 0)),
                  pl.BlockSpec((None, k, tn), lambda i, j: (l, 0, j))],
        out_specs=pl.BlockSpec((tm, tn), lambda i, j: (i, j)),
        out_shape=jax.ShapeDtypeStruct((m, n), out_dtype),
        compiler_params=_params(("parallel", "arbitrary")),
        name="proj",
    )(x, w)


def _proj_rope_kernel(x_ref, w_ref, cos_ref, sin_ref, o_ref, *, tn, block_one_hot):
    h = _dot(x_ref[...], w_ref[...])
    cos = cos_ref[0]
    sin = sin_ref[0]
    tm = cos.shape[0]
    lane = lax.broadcasted_iota(jnp.int32, cos.shape, 1)
    first_half = (lane % NSA_HEAD_DIM) < (NSA_HEAD_DIM // 2)
    if block_one_hot:
        pos = pl.program_id(0) * tm + lax.broadcasted_iota(jnp.int32, cos.shape, 0)
        one_hot = (lane == pos // SLC_BLOCK).astype(o_ref.dtype)
    for c in range(tn // 128):
        hc = h[:, c * 128:(c + 1) * 128]
        rot = jnp.where(first_half, pltpu.roll(hc, 96, axis=1), pltpu.roll(hc, 32, axis=1))
        roped = (hc * cos + rot * sin).astype(o_ref.dtype)
        if block_one_hot:
            o_ref[:, 2 * c * 128:(2 * c + 1) * 128] = one_hot
            o_ref[:, (2 * c + 1) * 128:(2 * c + 2) * 128] = roped
        else:
            o_ref[:, c * 128:(c + 1) * 128] = roped


def _proj_rope(x, w, l, col0, cos_tabs, sin_tabs, tab, out_dtype, tm, tn, block_one_hot=False):
    m, k = x.shape
    n_out = 2 * tn if block_one_hot else tn
    tab_spec = pl.BlockSpec((1, tm, 128), lambda i: (tab, i, 0))
    return pl.pallas_call(
        functools.partial(_proj_rope_kernel, tn=tn, block_one_hot=block_one_hot),
        grid=(m // tm,),
        in_specs=[pl.BlockSpec((tm, k), lambda i: (i, 0)),
                  pl.BlockSpec((None, k, tn), lambda i: (l, 0, col0)),
                  tab_spec, tab_spec],
        out_specs=pl.BlockSpec((tm, n_out), lambda i: (i, 0)),
        out_shape=jax.ShapeDtypeStruct((m, n_out), out_dtype),
        compiler_params=_params(("parallel",)),
        name="proj_rope",
    )(x, w, cos_tabs, sin_tabs)


def _hgrn_chunk(q, z, v, g, lb, nw, st, tri, sub_row):
    C, c = HGRN_CHUNK, HGRN_SUB
    one_minus_lb = 1.0 - lb
    sig, sig_neg = _sigmoid_pair(z)
    f = lb + one_minus_lb * sig
    lf = jnp.log2(jnp.maximum(f, F_MIN))
    k = one_minus_lb * sig_neg
    qs = _silu(q)
    lf_hi, lf_mid, lf_lo = _split3(lf)
    b = _dot(tri, lf_hi) + _dot(tri, lf_mid) + _dot(tri, lf_lo)
    o_inter = _dot_nt((qs * jnp.exp2(b)).astype(BF16), st.astype(BF16))
    v16 = v.astype(BF16)
    outs = []
    for i in range(C // c):
        lo = i * c
        b_i = b[lo:lo + c]
        qs_i = qs[lo:lo + c]
        k_i = k[lo:lo + c]
        v_i = v[lo:lo + c]
        o_i = o_inter[lo:lo + c]
        if i > 0:
            beta = b[lo - 1:lo]
            qt = (qs_i * jnp.exp2(b_i - beta)).astype(BF16)
            kt = (k[:lo] * jnp.exp2(beta - b[:lo])).astype(BF16)
            att = _dot_nt(qt, kt)
            o_i = o_i + _dot(att.astype(BF16), v16[:lo])
        for s in range(c):
            d = jnp.exp2(b_i - b_i[s:s + 1])
            w = jnp.sum(qs_i * d * k_i[s:s + 1], axis=-1, keepdims=True)
            w = jnp.where(sub_row >= s, w, 0.0)
            o_i = o_i + w * v_i[s:s + 1]
        outs.append(o_i)
    o = jnp.concatenate(outs, axis=0)
    b_last = b[C - 1:C]
    kd = (k * jnp.exp2(b_last - b)).astype(BF16)
    st_new = jnp.exp2(b_last) * st + _dot_tn(v16, kd)
    o = o * lax.rsqrt(jnp.mean(o * o, axis=-1, keepdims=True) + RMS_EPS)
    return o * nw * _silu(g), st_new


def _hgrn_kernel(q_ref, z_ref, v_ref, g_ref, lb_ref, nw_ref, o_ref, st_ref, *, tb, hb):
    C, c = HGRN_CHUNK, HGRN_SUB

    @pl.when(pl.program_id(1) == 0)
    def _():
        st_ref[...] = jnp.zeros_like(st_ref)

    tri = (lax.broadcasted_iota(jnp.int32, (C, C), 1)
           <= lax.broadcasted_iota(jnp.int32, (C, C), 0)).astype(BF16)
    sub_row = lax.broadcasted_iota(jnp.int32, (c, 1), 0)

    def chunk(ci, carry):
        r0 = pl.multiple_of(ci * C, C)
        for h in range(hb):
            lanes = slice(h * HEAD_LANES, (h + 1) * HEAD_LANES)
            out, st_new = _hgrn_chunk(q_ref[pl.ds(r0, C), lanes], z_ref[pl.ds(r0, C), lanes],
                                      v_ref[pl.ds(r0, C), lanes], g_ref[pl.ds(r0, C), lanes],
                                      lb_ref[:, lanes], nw_ref[:, lanes], st_ref[h], tri, sub_row)
            st_ref[h] = st_new
            o_ref[pl.ds(r0, C), lanes] = out.astype(o_ref.dtype)
        return carry

    lax.fori_loop(0, tb // C, chunk, 0)


def _hgrn(hh, lb, nw, tb, hb):
    s = hh.shape[0]
    nhb = HGRN_HEADS // hb
    wide = hb * HEAD_LANES

    def col(off):
        return pl.BlockSpec((tb, wide), lambda h, t: (t, off * nhb + h))

    vec = pl.BlockSpec((1, wide), lambda h, t: (0, h))
    return pl.pallas_call(
        functools.partial(_hgrn_kernel, tb=tb, hb=hb),
        grid=(nhb, s // tb),
        in_specs=[col(0), col(1), col(2), col(3), vec, vec],
        out_specs=pl.BlockSpec((tb, wide), lambda h, t: (t, h)),
        out_shape=jax.ShapeDtypeStruct((s, HGRN_WIDTH), BF16),
        scratch_shapes=[pltpu.VMEM((hb, HEAD_LANES, HEAD_LANES), F32)],
        compiler_params=_params(("parallel", "arbitrary")),
        name="hgrn2",
    )(hh, hh, hh, hh, lb, nw)


def _projT_rope_kernel(w_ref, x_ref, cos_ref, sin_ref, o_ref, *, tq):
    h = _dot_nt(w_ref[...], x_ref[...])
    cos = cos_ref[...]
    sin = sin_ref[...]
    half = NSA_HEAD_DIM // 2
    for hp in range(NSA_HPG):
        r = hp * NSA_HEAD_DIM
        x1 = h[r:r + half]
        x2 = h[r + half:r + 2 * half]
        o_ref[0, :half, hp * tq:(hp + 1) * tq] = (x1 * cos - x2 * sin).astype(o_ref.dtype)
        o_ref[0, half:, hp * tq:(hp + 1) * tq] = (x2 * cos + x1 * sin).astype(o_ref.dtype)


def _projT_rope(wT, l, x, cosT, sinT, tq):
    _, n, d = wT.shape
    s = x.shape[0]
    half = NSA_HEAD_DIM // 2
    gw = NSA_HPG * NSA_HEAD_DIM
    return pl.pallas_call(
        functools.partial(_projT_rope_kernel, tq=tq),
        grid=(s // tq, n // gw),
        in_specs=[pl.BlockSpec((None, gw, d), lambda i, j: (l, j, 0)),
                  pl.BlockSpec((tq, d), lambda i, j: (i, 0)),
                  pl.BlockSpec((half, tq), lambda i, j: (0, i)),
                  pl.BlockSpec((half, tq), lambda i, j: (0, i))],
        out_specs=pl.BlockSpec((1, NSA_HEAD_DIM, NSA_HPG * tq), lambda i, j: (j, 0, i)),
        out_shape=jax.ShapeDtypeStruct((n // gw, NSA_HEAD_DIM, NSA_HPG * s), BF16),
        compiler_params=_params(("parallel", "arbitrary")),
        name="projT_rope",
    )(wT, x, cosT, sinT)


def _projT_vg_kernel(w_ref, x_ref, v_ref, g_ref):
    h = _dot_nt(w_ref[...], x_ref[...])
    nv = v_ref.shape[0]
    v_ref[...] = h[:nv].astype(v_ref.dtype)
    g_ref[...] = h[nv:]


def _projT_vg(wT, l, x, nv, ts):
    _, n, d = wT.shape
    s = x.shape[0]
    return pl.pallas_call(
        _projT_vg_kernel,
        grid=(s // ts,),
        in_specs=[pl.BlockSpec((None, n, d), lambda i: (l, 0, 0)),
                  pl.BlockSpec((ts, d), lambda i: (i, 0))],
        out_specs=[pl.BlockSpec((nv, ts), lambda i: (0, i)),
                   pl.BlockSpec((n - nv, ts), lambda i: (0, i))],
        out_shape=[jax.ShapeDtypeStruct((nv, s), BF16), jax.ShapeDtypeStruct((n - nv, s), F32)],
        compiler_params=_params(("parallel",)),
        name="projT_vg",
    )(wT, x)


def _compress_kernel(x_ref, pe_ref, w1_ref, w2_ref, kv_ref, kvT_ref):
    n_blk = x_ref.shape[0] // CMP_STRIDE
    width = 2 * CMP_HIDDEN
    top = jnp.zeros((n_blk, width), F32)
    bot = jnp.zeros((n_blk, width), F32)
    for j in range(CMP_STRIDE):
        rows = x_ref[pl.ds(j, n_blk, stride=CMP_STRIDE), :]
        top = top + _dot((rows + pe_ref[j]).astype(BF16), w1_ref[0, j])
        bot = bot + _dot((rows + pe_ref[CMP_STRIDE + j]).astype(BF16), w1_ref[0, CMP_STRIDE + j])
    hid = top + pltpu.roll(bot, n_blk - 1, axis=0)
    out = _dot(_silu(hid).astype(BF16), w2_ref[0])
    kv_ref[0] = out.astype(kv_ref.dtype)
    kvT_ref[0] = out.T.astype(kvT_ref.dtype)


def _compress(x, pe, w1, w2, l):
    s = x.shape[0]
    g = NSA_KV_GROUPS
    n_blk = s // CMP_STRIDE
    return pl.pallas_call(
        _compress_kernel,
        grid=(g,),
        in_specs=[pl.BlockSpec((s, 128), lambda gi: (0, gi)),
                  pl.BlockSpec(pe.shape, lambda gi: (0, 0, 0)),
                  pl.BlockSpec((1,) + w1.shape[1:], lambda gi: (l, 0, 0, 0)),
                  pl.BlockSpec((1,) + w2.shape[1:], lambda gi: (l, 0, 0))],
        out_specs=[pl.BlockSpec((1, n_blk, 128), lambda gi: (gi, 0, 0)),
                   pl.BlockSpec((1, 128, n_blk), lambda gi: (gi, 0, 0))],
        out_shape=[jax.ShapeDtypeStruct((g, n_blk, 128), BF16),
                   jax.ShapeDtypeStruct((g, 128, n_blk), BF16)],
        compiler_params=_params(("parallel",)),
        name="compress",
    )(x, pe, w1, w2)


ONES_ROWS = 16


def _with_ones_rows(vT):
    return jnp.concatenate([vT, jnp.ones((ONES_ROWS, vT.shape[1]), vT.dtype)], axis=0)


def _per_head(x, tq):
    return [x[:, hp * tq:(hp + 1) * tq] for hp in range(NSA_HPG)]


CMP_ROW_STEP = 128
PICKED = -3e38


def _cmp_select_kernel(q_ref, kc_ref, vcT_ref, oc_ref, selT_ref, imp_ref, *, tq):
    n_blk = kc_ref.shape[1]
    qi = pl.program_id(1)
    t = qi * tq + lax.broadcasted_iota(jnp.int32, (1, tq), 1)
    any_visible = (t >= CMP_BLOCK - 1).astype(F32)

    def attend(rows):
        kc = kc_ref[0, :rows, :NSA_HEAD_DIM]
        vcT = vcT_ref[0, :, :rows]
        n_idx = lax.broadcasted_iota(jnp.int32, (rows, 1), 0)
        bias = jnp.where((n_idx * CMP_STRIDE + (CMP_BLOCK - 1)) <= t, 0.0, NEG_INF)
        s = _dot(kc, q_ref[0])
        s = jnp.concatenate([x + bias for x in _per_head(s, tq)], axis=1)
        e = jnp.exp2(s - jnp.max(s, axis=0, keepdims=True))
        scale = jnp.concatenate([any_visible] * NSA_HPG, axis=1) / jnp.sum(e, axis=0, keepdims=True)
        p = e * scale
        oc_ref[0] = _dot(vcT, p.astype(BF16))
        p_sum = functools.reduce(lambda a, b: a + b, _per_head(p, tq))
        sj = lax.broadcasted_iota(jnp.int32, (SLC_LANES, rows), 0) * SLC_BLOCK
        ci = lax.broadcasted_iota(jnp.int32, (SLC_LANES, rows), 1) * CMP_STRIDE
        overlap = ((ci < sj + SLC_BLOCK) & (ci + CMP_BLOCK > sj)).astype(BF16)
        p_hi, p_mid, p_lo = _split3(p_sum)
        imp_ref[...] = _dot(overlap, p_hi) + _dot(overlap, p_mid) + _dot(overlap, p_lo)

    n_visible = (qi + 1) * (tq // CMP_STRIDE)
    n_steps = n_blk // CMP_ROW_STEP
    for b in range(1, n_steps + 1):
        below = n_visible <= b * CMP_ROW_STEP
        above = n_visible > (b - 1) * CMP_ROW_STEP
        pl.when(above & below if b < n_steps else above)(functools.partial(attend, b * CMP_ROW_STEP))

    blk = lax.broadcasted_iota(jnp.int32, (SLC_LANES, 1), 0)
    cur = t // SLC_BLOCK
    forced = (blk == 0) | (blk == cur) | (blk == cur - 1)
    causal = blk * SLC_BLOCK <= t
    score = jnp.where(forced, FORCE_SCORE, jnp.where(causal, imp_ref[...], -1.0))
    blk_f = blk.astype(F32)
    work = score
    for _ in range(SLC_TOPN):
        mx = jnp.max(work, axis=0, keepdims=True)
        first = jnp.min(jnp.where(work == mx, blk_f, float(SLC_LANES)), axis=0, keepdims=True)
        work = jnp.where(blk_f == first, PICKED, work)
    selT_ref[0] = jnp.where((work == PICKED) & (score >= 0.0), 1.0, 0.0).astype(selT_ref.dtype)


def _cmp_select(qL, k_cmp, v_cmpT, tq):
    g = NSA_KV_GROUPS
    s = qL.shape[2] // NSA_HPG
    n_blk = k_cmp.shape[1]
    q_spec = pl.BlockSpec((1, NSA_HEAD_DIM, NSA_HPG * tq), lambda gi, qi: (gi, 0, qi))
    return pl.pallas_call(
        functools.partial(_cmp_select_kernel, tq=tq),
        grid=(g, s // tq),
        in_specs=[q_spec,
                  pl.BlockSpec((1, n_blk, 2 * NSA_HEAD_DIM), lambda gi, qi: (gi, 0, 0)),
                  pl.BlockSpec((1, NSA_HEAD_DIM, n_blk), lambda gi, qi: (gi, 1, 0))],
        out_specs=[q_spec,
                   pl.BlockSpec((1, SLC_LANES, tq), lambda gi, qi: (gi, 0, qi))],
        out_shape=[jax.ShapeDtypeStruct(qL.shape, F32),
                   jax.ShapeDtypeStruct((g, SLC_LANES, s), BF16)],
        scratch_shapes=[pltpu.VMEM((SLC_LANES, tq), F32)],
        compiler_params=_params(("parallel", "arbitrary")),
        name="cmp_select",
    )(qL, k_cmp, v_cmpT)


def _win_attn_kernel(q_ref, k0_ref, k1_ref, v0_ref, v1_ref, ow_ref, *, tq):
    dk = NSA_HEAD_DIM
    qi = pl.program_id(1)
    q = q_ref[0]
    t = qi * tq + lax.broadcasted_iota(jnp.int32, (1, tq), 1)
    s = []
    for c, k_ref in enumerate((k0_ref, k1_ref)):
        j = (qi - 1 + c) * tq + lax.broadcasted_iota(jnp.int32, (tq, 1), 0)
        bias = jnp.where((j >= 0) & (j <= t) & (t - j < WINDOW), 0.0, NEG_INF)
        sc = _dot(k_ref[:, dk:], q)
        s.append(jnp.concatenate([x + bias for x in _per_head(sc, tq)], axis=1))
    m = jnp.maximum(jnp.max(s[0], axis=0, keepdims=True), jnp.max(s[1], axis=0, keepdims=True))
    o = (_dot(_with_ones_rows(v0_ref[...]), jnp.exp2(s[0] - m).astype(BF16))
         + _dot(_with_ones_rows(v1_ref[...]), jnp.exp2(s[1] - m).astype(BF16)))
    ow_ref[0] = o[:dk] / o[dk:dk + 1]


def _win_attn(qL, kr, vT, tq):
    g = NSA_KV_GROUPS
    s = qL.shape[2] // NSA_HPG
    assert WINDOW == tq
    q_spec = pl.BlockSpec((1, NSA_HEAD_DIM, NSA_HPG * tq), lambda gi, qi: (gi, 0, qi))

    def k_spec(c):
        return pl.BlockSpec((tq, 128), lambda gi, qi: (jnp.maximum(qi - 1 + c, 0), 2 * gi + 1))

    def v_spec(c):
        return pl.BlockSpec((NSA_HEAD_DIM, tq), lambda gi, qi: (2 * gi + 1, jnp.maximum(qi - 1 + c, 0)))

    return pl.pallas_call(
        functools.partial(_win_attn_kernel, tq=tq),
        grid=(g, s // tq),
        in_specs=[q_spec, k_spec(0), k_spec(1), v_spec(0), v_spec(1)],
        out_specs=q_spec,
        out_shape=jax.ShapeDtypeStruct(qL.shape, F32),
        compiler_params=_params(("parallel", "arbitrary")),
        name="win_attn",
    )(qL, kr, kr, vT, vT)


def _slc_attn_kernel(qi_tab, ki_tab, q_ref, k_ref, vT_ref, selT_ref, oc_ref, ow_ref, gT_ref,
                     o_ref, rhs_ref, m_ref, acc_ref, *, tq):
    dk = NSA_HEAD_DIM
    step = pl.program_id(1)
    qi = qi_tab[step]
    ki = ki_tab[step]

    @pl.when(ki == 0)
    def _():
        m_ref[...] = jnp.full_like(m_ref, NEG_INF)
        acc_ref[...] = jnp.zeros_like(acc_ref)
        not_selected = ((selT_ref[0].astype(F32) - 1.0) * (-NEG_INF)).astype(BF16)
        rhs_ref[:SLC_LANES, :] = jnp.concatenate([not_selected] * NSA_HPG, axis=1)
        rhs_ref[SLC_LANES:SLC_LANES + dk, :] = q_ref[0]
        rhs_ref[SLC_LANES + dk:, :] = jnp.zeros((dk, NSA_HPG * tq), BF16)

    def update(sub_tiles):
        chains = [(sub, hp, causal) for sub, causal in sub_tiles for hp in range(NSA_HPG)]
        v_aug = {sub: _with_ones_rows(vT_ref[:, sub * tq:(sub + 1) * tq]) for sub, _ in sub_tiles}
        scores = lambda sub, hp: _dot(k_ref[sub * tq:(sub + 1) * tq, :], rhs_ref[:, hp * tq:(hp + 1) * tq])
        s_next = scores(*chains[0][:2])
        for n, (sub, hp, causal) in enumerate(chains):
            lanes = slice(hp * tq, (hp + 1) * tq)
            s = s_next
            if n + 1 < len(chains):
                s_next = scores(*chains[n + 1][:2])
            if causal:
                s = jnp.where(lax.broadcasted_iota(jnp.int32, (tq, 1), 0)
                              <= lax.broadcasted_iota(jnp.int32, (1, tq), 1), s, NEG_INF)
            m_old = m_ref[:, lanes]
            m_new = jnp.maximum(m_old, jnp.max(s, axis=0, keepdims=True))
            p = jnp.exp2(s - m_new).astype(BF16)
            acc_ref[:, lanes] = jnp.exp2(m_old - m_new) * acc_ref[:, lanes] + _dot(v_aug[sub], p)
            m_ref[:, lanes] = m_new

    def finish():
        sig = _sigmoid_pair(gT_ref[...])[0]

        def gate(br):
            return jnp.concatenate([sig[br * NSA_HPG + hp:br * NSA_HPG + hp + 1] for hp in range(NSA_HPG)], axis=1)

        o_s = acc_ref[:dk, :] / acc_ref[dk:dk + 1, :]
        out = gate(0) * oc_ref[0] + gate(1) * o_s + gate(2) * ow_ref[0]
        o_ref[...] = jnp.concatenate(_per_head(out, tq), axis=0).T.astype(o_ref.dtype)

    ahead = qi - 2 * ki

    @pl.when(ahead >= 2)
    def _():
        update([(0, False), (1, False)])

    @pl.when(ahead == 1)
    def _():
        update([(0, False), (1, True)])
        finish()

    @pl.when(ahead == 0)
    def _():
        update([(0, True)])
        finish()


def _slc_attn(qL, kr, vT, selT, ocL, owL, gT, tq):
    g = NSA_KV_GROUPS
    s = qL.shape[2] // NSA_HPG
    tk = 2 * tq
    pairs = [(qi, ki) for qi in range(s // tq) for ki in range(qi // 2 + 1)]
    qi_tab = jnp.asarray(np.array([p[0] for p in pairs], np.int32))
    ki_tab = jnp.asarray(np.array([p[1] for p in pairs], np.int32))
    q_spec = pl.BlockSpec((1, NSA_HEAD_DIM, NSA_HPG * tq), lambda gi, st, qt, kt: (gi, 0, qt[st]))
    grid_spec = pltpu.PrefetchScalarGridSpec(
        num_scalar_prefetch=2,
        grid=(g, len(pairs)),
        in_specs=[q_spec,
                  pl.BlockSpec((tk, 256), lambda gi, st, qt, kt: (kt[st], gi)),
                  pl.BlockSpec((NSA_HEAD_DIM, tk), lambda gi, st, qt, kt: (2 * gi, kt[st])),
                  pl.BlockSpec((1, SLC_LANES, tq), lambda gi, st, qt, kt: (gi, 0, qt[st])),
                  q_spec, q_spec,
                  pl.BlockSpec((GATE_ROWS, tq), lambda gi, st, qt, kt: (gi, qt[st]))],
        out_specs=pl.BlockSpec((tq, NSA_HPG * NSA_HEAD_DIM), lambda gi, st, qt, kt: (qt[st], gi)),
        scratch_shapes=[pltpu.VMEM((SLC_LANES + 2 * NSA_HEAD_DIM, NSA_HPG * tq), BF16),
                        pltpu.VMEM((1, NSA_HPG * tq), F32),
                        pltpu.VMEM((NSA_HEAD_DIM + ONES_ROWS, NSA_HPG * tq), F32)])
    return pl.pallas_call(
        functools.partial(_slc_attn_kernel, tq=tq),
        grid_spec=grid_spec,
        out_shape=jax.ShapeDtypeStruct((s, NSA_WIDTH), BF16),
        compiler_params=_params(("parallel", "arbitrary")),
        name="slc_attn",
    )(qi_tab, ki_tab, qL, kr, vT, selT, ocL, owL, gT)


def _res_ln_epilogue(acc, x_ref, g_ref, b_ref, xo_ref, xb_ref):
    v = DN_ALPHA * x_ref[...] + acc
    mu = jnp.mean(v, axis=-1, keepdims=True)
    d = v - mu
    var = jnp.mean(d * d, axis=-1, keepdims=True)
    y = d * lax.rsqrt(var + LN_EPS) * g_ref[...] + b_ref[...]
    xo_ref[...] = y
    xb_ref[...] = y.astype(BF16)


def _mix_out_kernel(a1_ref, a2_ref, w_ref, x_ref, g_ref, b_ref, xo_ref, xb_ref):
    k1 = a1_ref.shape[1]
    y = _dot(a1_ref[...], w_ref[:k1, :]) + _dot(a2_ref[...], w_ref[k1:, :])
    _res_ln_epilogue(y, x_ref, g_ref, b_ref, xo_ref, xb_ref)


def _mix_out(a1, a2, w, l, x, gamma, beta, tm):
    s, k1 = a1.shape
    n = w.shape[2]
    row = pl.BlockSpec((tm, n), lambda i: (i, 0))
    vec = pl.BlockSpec((1, n), lambda i: (0, 0))
    return pl.pallas_call(
        _mix_out_kernel,
        grid=(s // tm,),
        in_specs=[pl.BlockSpec((tm, k1), lambda i: (i, 0)),
                  pl.BlockSpec((tm, k1), lambda i: (i, 0)),
                  pl.BlockSpec((None,) + w.shape[1:], lambda i: (l, 0, 0)),
                  row, vec, vec],
        out_specs=[row, row],
        out_shape=[jax.ShapeDtypeStruct((s, n), F32), jax.ShapeDtypeStruct((s, n), BF16)],
        compiler_params=_params(("parallel",)),
        name="mix_out",
    )(a1, a2, w, x, gamma, beta)


def _ffn_down_kernel(a_ref, w_ref, x_ref, g_ref, b_ref, xo_ref, xb_ref, acc_ref):
    k = pl.program_id(1)

    @pl.when(k == 0)
    def _():
        acc_ref[...] = _dot(a_ref[...], w_ref[...])

    @pl.when(k > 0)
    def _():
        acc_ref[...] += _dot(a_ref[...], w_ref[...])

    @pl.when(k == pl.num_programs(1) - 1)
    def _():
        _res_ln_epilogue(acc_ref[...], x_ref, g_ref, b_ref, xo_ref, xb_ref)


def _ffn_down(a, w, l, x, gamma, beta, tm, tk):
    s, kk = a.shape
    n = w.shape[2]
    row = pl.BlockSpec((tm, n), lambda i, k: (i, 0))
    vec = pl.BlockSpec((1, n), lambda i, k: (0, 0))
    res = pl.BlockSpec((tm, n), lambda i, k: (i, 0), pipeline_mode=pl.Buffered(1))
    return pl.pallas_call(
        _ffn_down_kernel,
        grid=(s // tm, kk // tk),
        in_specs=[pl.BlockSpec((tm, tk), lambda i, k: (i, k)),
                  pl.BlockSpec((None, tk, n), lambda i, k: (l, k, 0)),
                  res, vec, vec],
        out_specs=[row, row],
        out_shape=[jax.ShapeDtypeStruct((s, n), F32), jax.ShapeDtypeStruct((s, n), BF16)],
        scratch_shapes=[pltpu.VMEM((tm, n), F32)],
        compiler_params=pltpu.CompilerParams(dimension_semantics=("parallel", "arbitrary"),
                                             vmem_limit_bytes=FFN_DOWN_VMEM_LIMIT),
        name="ffn_down",
    )(a, w, x, gamma, beta)


HALO = 16


def _ffn_up_kernel(a_ref, ap_ref, wg_ref, wu_ref, cwg_ref, cwu_ref, cbg_ref, cbu_ref, o_ref, wgb_ref, wub_ref):
    i = pl.program_id(1)

    @pl.when(i == 0)
    def _():
        wgb_ref[...] = wg_ref[...].astype(BF16)
        wub_ref[...] = wu_ref[...].astype(BF16)

    ap = ap_ref[...]
    ap = jnp.where(i > 0, ap, jnp.zeros_like(ap))
    a = jnp.concatenate([ap, a_ref[...]], axis=0)

    def branch(w_ref, cw_ref, cb_ref):
        h = _dot(a, w_ref[...])
        cw = cw_ref[...]
        return (cw[2:3] * h[HALO:] + cw[1:2] * h[HALO - 1:-1] + cw[0:1] * h[HALO - 2:-2]) + cb_ref[...]

    gate = branch(wgb_ref, cwg_ref, cbg_ref)
    up = branch(wub_ref, cwu_ref, cbu_ref)
    o_ref[...] = (_silu(gate) * up).astype(o_ref.dtype)


def _ffn_up(a, w, l, conv_w, conv_b, tm, tn):
    s, d = a.shape
    nj = D_FF // tn
    return pl.pallas_call(
        _ffn_up_kernel,
        grid=(nj, s // tm),
        in_specs=[pl.BlockSpec((tm, d), lambda j, i: (i, 0)),
                  pl.BlockSpec((HALO, d), lambda j, i: (jnp.maximum(i * (tm // HALO) - 1, 0), 0)),
                  pl.BlockSpec((None, d, tn), lambda j, i: (l, 0, j)),
                  pl.BlockSpec((None, d, tn), lambda j, i: (l, 0, j + nj)),
                  pl.BlockSpec((3, tn), lambda j, i: (0, j)),
                  pl.BlockSpec((3, tn), lambda j, i: (0, j + nj)),
                  pl.BlockSpec((1, tn), lambda j, i: (0, j)),
                  pl.BlockSpec((1, tn), lambda j, i: (0, j + nj))],
        out_specs=pl.BlockSpec((tm, tn), lambda j, i: (i, j)),
        out_shape=jax.ShapeDtypeStruct((s, D_FF), BF16),
        scratch_shapes=[pltpu.VMEM((d, tn), BF16), pltpu.VMEM((d, tn), BF16)],
        compiler_params=_params(("parallel", "arbitrary")),
        name="ffn_up",
    )(a, a, w, w, conv_w, conv_w, conv_b, conv_b)


def _rope_tables(s):
    half = NSA_HEAD_DIM // 2
    inv = ROPE_THETA ** (-jnp.arange(half, dtype=F32) / half)
    ang = jnp.arange(s).astype(F32)[:, None] * inv[None, :]
    cos, sin = jnp.cos(ang), jnp.sin(ang)
    cos64 = jnp.concatenate([cos, cos], -1)
    sin64 = jnp.concatenate([-sin, sin], -1)
    ones, zeros = jnp.ones_like(cos64), jnp.zeros_like(sin64)
    cos_tabs = jnp.stack([jnp.concatenate([cos64, ones], -1), jnp.concatenate([cos64, cos64], -1)])
    sin_tabs = jnp.stack([jnp.concatenate([sin64, zeros], -1), jnp.concatenate([sin64, sin64], -1)])
    q_scale = NSA_HEAD_DIM ** -0.5 * LOG2E
    return cos_tabs, sin_tabs, cos.T * q_scale, sin.T * q_scale


def _nsa_weight_layout(w_nsa):
    nl, d, _ = w_nsa.shape
    g, dk = NSA_KV_GROUPS, NSA_HEAD_DIM
    q = w_nsa[:, :, :NSA_WIDTH]
    kv = w_nsa[:, :, NSA_WIDTH:NSA_WIDTH + 6 * NSA_KV_WIDTH].reshape(nl, d, 6, g, dk)
    kc, vc, ks, vs, kw, vw = (kv[:, :, i] for i in range(6))
    w_rows = jnp.concatenate([jnp.stack([kc, vc], 3).reshape(nl, d, 2 * g * dk),
                              jnp.stack([ks, kw], 3).reshape(nl, d, 2 * g * dk)], axis=2)
    w_v = jnp.stack([vs, vw], 3).reshape(nl, d, 2 * g * dk)
    gt = w_nsa[:, :, NSA_WIDTH + 6 * NSA_KV_WIDTH:].reshape(nl, d, g, NSA_HPG, 3).transpose(0, 1, 2, 4, 3)
    gt = jnp.pad(gt.reshape(nl, d, g, 3 * NSA_HPG), ((0, 0), (0, 0), (0, 0), (0, GATE_ROWS - 3 * NSA_HPG)))
    w_vg = jnp.concatenate([w_v, gt.reshape(nl, d, g * GATE_ROWS)], axis=2)
    return w_rows.astype(BF16), q.transpose(0, 2, 1).astype(BF16), w_vg.transpose(0, 2, 1).astype(BF16)


def _block_diag(a, b):
    top = jnp.concatenate([a, jnp.zeros(a.shape[:-1] + b.shape[-1:], a.dtype)], axis=-1)
    bot = jnp.concatenate([jnp.zeros(b.shape[:-1] + a.shape[-1:], b.dtype), b], axis=-1)
    return jnp.concatenate([top, bot], axis=-2)


def _compress_weight_layout(w1_k, w1_v, w2_k, w2_v):
    nl = w1_k.shape[0]
    per_token = lambda w: w.reshape(nl, CMP_BLOCK, NSA_HEAD_DIM, CMP_HIDDEN)
    return (_block_diag(per_token(w1_k), per_token(w1_v)).astype(BF16),
            _block_diag(w2_k, w2_v).astype(BF16))


def kernel(x, w_in, w_out, hgrn_lb_logits, hgrn_norm_w, cmp_pe_k, cmp_pe_v, cmp_w1_k, cmp_w2_k,
           cmp_w1_v, cmp_w2_v, ln1_g, ln1_b, w_up, conv_w, conv_b, w_down, ln2_g, ln2_b):
    s = x.shape[1]
    g = NSA_KV_GROUPS
    p_lb = jax.nn.softmax(hgrn_lb_logits.astype(F32), axis=0)
    lower_bounds = jnp.cumsum(p_lb, axis=0) - p_lb[0:1]
    cos_tabs, sin_tabs, cosT, sinT = _rope_tables(s)
    hw = 4 * HGRN_WIDTH
    w_hgrn_b, w_out_b, w_down_b = (w.astype(BF16) for w in (w_in[:, :, :hw], w_out, w_down))
    w_rows, w_qT, w_vgT = _nsa_weight_layout(w_in[:, :, hw:])
    cmp_w1, cmp_w2 = _compress_weight_layout(cmp_w1_k, cmp_w1_v, cmp_w2_k, cmp_w2_v)
    xf = x[0]
    xb = xf.astype(BF16)
    for l in range(DEPTH):
        hh = _proj(xb, w_hgrn_b, l, hw, F32, 1024, 512)
        kvc = _proj_rope(xb, w_rows, l, 0, cos_tabs, sin_tabs, 0, F32, 1024, 512)
        kr = _proj_rope(xb, w_rows, l, 1, cos_tabs, sin_tabs, 1, BF16, 1024, 512, block_one_hot=True)
        qL = _projT_rope(w_qT, l, xb, cosT, sinT, NSA_TQ)
        vT, gT = _projT_vg(w_vgT, l, xb, 2 * g * NSA_HEAD_DIM, 1024)
        o_h = _hgrn(hh, lower_bounds[l][None], hgrn_norm_w[l][None], 512, 8)

        pe = jnp.concatenate([cmp_pe_k[l], cmp_pe_v[l]], axis=-1)[:, None, :]
        kv_cmp, kv_cmpT = _compress(kvc, pe, cmp_w1, cmp_w2, l)
        ocL, selT = _cmp_select(qL, kv_cmp, kv_cmpT, NSA_TQ)
        owL = _win_attn(qL, kr, vT, NSA_TQ)
        o_n = _slc_attn(qL, kr, vT, selT, ocL, owL, gT, NSA_TQ)

        xf, xb = _mix_out(o_h, o_n, w_out_b, l, xf, ln1_g[l][None], ln1_b[l][None], 512)
        u = _ffn_up(xb, w_up, l, conv_w[l], conv_b[l][None], 1024, 512)
        xf, xb = _ffn_down(u, w_down_b, l, xf, ln2_g[l][None], ln2_b[l][None], 1024, 512)
    return xf[None]
```

```python
import functools

import numpy as np
import jax
import jax.numpy as jnp
from jax import lax
from jax.experimental import pallas as pl
from jax.experimental.pallas import tpu as pltpu

F32 = jnp.float32
BF16 = jnp.bfloat16

D_MODEL = 2048
DEPTH = 4
HGRN_WIDTH = 1024
HGRN_HEADS = 8
HEAD_LANES = 128
NSA_WIDTH = 1024
NSA_HEAD_DIM = 64
NSA_HEADS = 16
NSA_KV_GROUPS = 4
NSA_HPG = 4
NSA_KV_WIDTH = NSA_KV_GROUPS * NSA_HEAD_DIM
CMP_BLOCK = 32
CMP_STRIDE = 16
CMP_HIDDEN = 256
SLC_BLOCK = 64
SLC_TOPN = 16
SLC_LANES = 128
WINDOW = 512
D_FF = 5632
ROPE_THETA = 10000.0
LN_EPS = 1e-5
RMS_EPS = 1e-6
F_MIN = 1e-30
DN_ALPHA = (2 * DEPTH) ** 0.25
NEG_INF = -1e30
FORCE_SCORE = 1e9

GATE_ROWS = 16
LOG2E = 1.4426950408889634

NSA_TQ = 512
HGRN_CHUNK = 64
HGRN_SUB = 16
VMEM_LIMIT = 48 * 1024 * 1024
FFN_DOWN_VMEM_LIMIT = 56 * 1024 * 1024


def _params(sem):
    return pltpu.CompilerParams(dimension_semantics=sem, vmem_limit_bytes=VMEM_LIMIT)


def _dot(a, b):
    return jnp.dot(a, b, preferred_element_type=F32)


def _dot_nt(a, b):
    return lax.dot_general(a, b, (((1,), (1,)), ((), ())), preferred_element_type=F32)


def _dot_tn(a, b):
    return lax.dot_general(a, b, (((0,), (0,)), ((), ())), preferred_element_type=F32)


def _split3(x):
    hi = x.astype(BF16)
    r = x - hi.astype(F32)
    mid = r.astype(BF16)
    lo = (r - mid.astype(F32)).astype(BF16)
    return hi, mid, lo


def _sigmoid_pair(z):
    e = jnp.exp(-jnp.abs(z))
    r = 1.0 / (1.0 + e)
    er = e * r
    pos = z >= 0
    return jnp.where(pos, r, er), jnp.where(pos, er, r)


def _silu(x):
    return x * _sigmoid_pair(x)[0]


def _proj_kernel(x_ref, w_ref, o_ref):
    o_ref[...] = _dot(x_ref[...], w_ref[...]).astype(o_ref.dtype)


def _proj(x, w, l, n, out_dtype, tm, tn):
    m, k = x.shape
    return pl.pallas_call(
        _proj_kernel,
        grid=(m // tm, n // tn),
        in_specs=[pl.BlockSpec((tm, k), lambda i, j: (i, 0)),
                  pl.BlockSpec((None, k, tn), lambda i, j: (l, 0, j))],
        out_specs=pl.BlockSpec((tm, tn), lambda i, j: (i, j)),
        out_shape=jax.ShapeDtypeStruct((m, n), out_dtype),
        compiler_params=_params(("parallel", "arbitrary")),
        name="proj",
    )(x, w)


def _proj_rope_kernel(x_ref, w_ref, cos_ref, sin_ref, o_ref, *, tn, block_one_hot):
    h = _dot(x_ref[...], w_ref[...])
    cos = cos_ref[0]
    sin = sin_ref[0]
    tm = cos.shape[0]
    lane = lax.broadcasted_iota(jnp.int32, cos.shape, 1)
    first_half = (lane % NSA_HEAD_DIM) < (NSA_HEAD_DIM // 2)
    if block_one_hot:
        pos = pl.program_id(0) * tm + lax.broadcasted_iota(jnp.int32, cos.shape, 0)
        one_hot = (lane == pos // SLC_BLOCK).astype(o_ref.dtype)
    for c in range(tn // 128):
        hc = h[:, c * 128:(c + 1) * 128]
        rot = jnp.where(first_half, pltpu.roll(hc, 96, axis=1), pltpu.roll(hc, 32, axis=1))
        roped = (hc * cos + rot * sin).astype(o_ref.dtype)
        if block_one_hot:
            o_ref[:, 2 * c * 128:(2 * c + 1) * 128] = one_hot
            o_ref[:, (2 * c + 1) * 128:(2 * c + 2) * 128] = roped
        else:
            o_ref[:, c * 128:(c + 1) * 128] = roped


def _proj_rope(x, w, l, col0, cos_tabs, sin_tabs, tab, out_dtype, tm, tn, block_one_hot=False):
    m, k = x.shape
    n_out = 2 * tn if block_one_hot else tn
    tab_spec = pl.BlockSpec((1, tm, 128), lambda i: (tab, i, 0))
    return pl.pallas_call(
        functools.partial(_proj_rope_kernel, tn=tn, block_one_hot=block_one_hot),
        grid=(m // tm,),
        in_specs=[pl.BlockSpec((tm, k), lambda i: (i, 0)),
                  pl.BlockSpec((None, k, tn), lambda i: (l, 0, col0)),
                  tab_spec, tab_spec],
        out_specs=pl.BlockSpec((tm, n_out), lambda i: (i, 0)),
        out_shape=jax.ShapeDtypeStruct((m, n_out), out_dtype),
        compiler_params=_params(("parallel",)),
        name="proj_rope",
    )(x, w, cos_tabs, sin_tabs)


def _hgrn_chunk(q, z, v, g, lb, nw, st, tri, sub_row):
    C, c = HGRN_CHUNK, HGRN_SUB
    one_minus_lb = 1.0 - lb
    sig, sig_neg = _sigmoid_pair(z)
    f = lb + one_minus_lb * sig
    lf = jnp.log2(jnp.maximum(f, F_MIN))
    k = one_minus_lb * sig_neg
    qs = _silu(q)
    lf_hi, lf_mid, lf_lo = _split3(lf)
    b = _dot(tri, lf_hi) + _dot(tri, lf_mid) + _dot(tri, lf_lo)
    o_inter = _dot_nt((qs * jnp.exp2(b)).astype(BF16), st.astype(BF16))
    v16 = v.astype(BF16)
    outs = []
    for i in range(C // c):
        lo = i * c
        b_i = b[lo:lo + c]
        qs_i = qs[lo:lo + c]
        k_i = k[lo:lo + c]
        v_i = v[lo:lo + c]
        o_i = o_inter[lo:lo + c]
        if i > 0:
            beta = b[lo - 1:lo]
            qt = (qs_i * jnp.exp2(b_i - beta)).astype(BF16)
            kt = (k[:lo] * jnp.exp2(beta - b[:lo])).astype(BF16)
            att = _dot_nt(qt, kt)
            o_i = o_i + _dot(att.astype(BF16), v16[:lo])
        for s in range(c):
            d = jnp.exp2(b_i - b_i[s:s + 1])
            w = jnp.sum(qs_i * d * k_i[s:s + 1], axis=-1, keepdims=True)
            w = jnp.where(sub_row >= s, w, 0.0)
            o_i = o_i + w * v_i[s:s + 1]
        outs.append(o_i)
    o = jnp.concatenate(outs, axis=0)
    b_last = b[C - 1:C]
    kd = (k * jnp.exp2(b_last - b)).astype(BF16)
    st_new = jnp.exp2(b_last) * st + _dot_tn(v16, kd)
    o = o * lax.rsqrt(jnp.mean(o * o, axis=-1, keepdims=True) + RMS_EPS)
    return o * nw * _silu(g), st_new


def _hgrn_kernel(q_ref, z_ref, v_ref, g_ref, lb_ref, nw_ref, o_ref, st_ref, *, tb, hb):
    C, c = HGRN_CHUNK, HGRN_SUB

    @pl.when(pl.program_id(1) == 0)
    def _():
        st_ref[...] = jnp.zeros_like(st_ref)

    tri = (lax.broadcasted_iota(jnp.int32, (C, C), 1)
           <= lax.broadcasted_iota(jnp.int32, (C, C), 0)).astype(BF16)
    sub_row = lax.broadcasted_iota(jnp.int32, (c, 1), 0)

    def chunk(ci, carry):
        r0 = pl.multiple_of(ci * C, C)
        for h in range(hb):
            lanes = slice(h * HEAD_LANES, (h + 1) * HEAD_LANES)
            out, st_new = _hgrn_chunk(q_ref[pl.ds(r0, C), lanes], z_ref[pl.ds(r0, C), lanes],
                                      v_ref[pl.ds(r0, C), lanes], g_ref[pl.ds(r0, C), lanes],
                                      lb_ref[:, lanes], nw_ref[:, lanes], st_ref[h], tri, sub_row)
            st_ref[h] = st_new
            o_ref[pl.ds(r0, C), lanes] = out.astype(o_ref.dtype)
        return carry

    lax.fori_loop(0, tb // C, chunk, 0)


def _hgrn(hh, lb, nw, tb, hb):
    s = hh.shape[0]
    nhb = HGRN_HEADS // hb
    wide = hb * HEAD_LANES

    def col(off):
        return pl.BlockSpec((tb, wide), lambda h, t: (t, off * nhb + h))

    vec = pl.BlockSpec((1, wide), lambda h, t: (0, h))
    return pl.pallas_call(
        functools.partial(_hgrn_kernel, tb=tb, hb=hb),
        grid=(nhb, s // tb),
        in_specs=[col(0), col(1), col(2), col(3), vec, vec],
        out_specs=pl.BlockSpec((tb, wide), lambda h, t: (t, h)),
        out_shape=jax.ShapeDtypeStruct((s, HGRN_WIDTH), BF16),
        scratch_shapes=[pltpu.VMEM((hb, HEAD_LANES, HEAD_LANES), F32)],
        compiler_params=_params(("parallel", "arbitrary")),
        name="hgrn2",
    )(hh, hh, hh, hh, lb, nw)


def _projT_rope_kernel(w_ref, x_ref, cos_ref, sin_ref, o_ref, *, tq):
    h = _dot_nt(w_ref[...], x_ref[...])
    cos = cos_ref[...]
    sin = sin_ref[...]
    half = NSA_HEAD_DIM // 2
    for hp in range(NSA_HPG):
        r = hp * NSA_HEAD_DIM
        x1 = h[r:r + half]
        x2 = h[r + half:r + 2 * half]
        o_ref[0, :half, hp * tq:(hp + 1) * tq] = (x1 * cos - x2 * sin).astype(o_ref.dtype)
        o_ref[0, half:, hp * tq:(hp + 1) * tq] = (x2 * cos + x1 * sin).astype(o_ref.dtype)


def _projT_rope(wT, l, x, cosT, sinT, tq):
    _, n, d = wT.shape
    s = x.shape[0]
    half = NSA_HEAD_DIM // 2
    gw = NSA_HPG * NSA_HEAD_DIM
    return pl.pallas_call(
        functools.partial(_projT_rope_kernel, tq=tq),
        grid=(s // tq, n // gw),
        in_specs=[pl.BlockSpec((None, gw, d), lambda i, j: (l, j, 0)),
                  pl.BlockSpec((tq, d), lambda i, j: (i, 0)),
                  pl.BlockSpec((half, tq), lambda i, j: (0, i)),
                  pl.BlockSpec((half, tq), lambda i, j: (0, i))],
        out_specs=pl.BlockSpec((1, NSA_HEAD_DIM, NSA_HPG * tq), lambda i, j: (j, 0, i)),
        out_shape=jax.ShapeDtypeStruct((n // gw, NSA_HEAD_DIM, NSA_HPG * s), BF16),
        compiler_params=_params(("parallel", "arbitrary")),
        name="projT_rope",
    )(wT, x, cosT, sinT)


def _projT_vg_kernel(w_ref, x_ref, v_ref, g_ref):
    h = _dot_nt(w_ref[...], x_ref[...])
    nv = v_ref.shape[0]
    v_ref[...] = h[:nv].astype(v_ref.dtype)
    g_ref[...] = h[nv:]


def _projT_vg(wT, l, x, nv, ts):
    _, n, d = wT.shape
    s = x.shape[0]
    return pl.pallas_call(
        _projT_vg_kernel,
        grid=(s // ts,),
        in_specs=[pl.BlockSpec((None, n, d), lambda i: (l, 0, 0)),
                  pl.BlockSpec((ts, d), lambda i: (i, 0))],
        out_specs=[pl.BlockSpec((nv, ts), lambda i: (0, i)),
                   pl.BlockSpec((n - nv, ts), lambda i: (0, i))],
        out_shape=[jax.ShapeDtypeStruct((nv, s), BF16), jax.ShapeDtypeStruct((n - nv, s), F32)],
        compiler_params=_params(("parallel",)),
        name="projT_vg",
    )(wT, x)


def _compress_kernel(x_ref, pe_ref, w1_ref, w2_ref, kv_ref, kvT_ref):
    n_blk = x_ref.shape[0] // CMP_STRIDE
    width = 2 * CMP_HIDDEN
    top = jnp.zeros((n_blk, width), F32)
    bot = jnp.zeros((n_blk, width), F32)
    for j in range(CMP_STRIDE):
        rows = x_ref[pl.ds(j, n_blk, stride=CMP_STRIDE), :]
        top = top + _dot((rows + pe_ref[j]).astype(BF16), w1_ref[0, j])
        bot = bot + _dot((rows + pe_ref[CMP_STRIDE + j]).astype(BF16), w1_ref[0, CMP_STRIDE + j])
    hid = top + pltpu.roll(bot, n_blk - 1, axis=0)
    out = _dot(_silu(hid).astype(BF16), w2_ref[0])
    kv_ref[0] = out.astype(kv_ref.dtype)
    kvT_ref[0] = out.T.astype(kvT_ref.dtype)


def _compress(x, pe, w1, w2, l):
    s = x.shape[0]
    g = NSA_KV_GROUPS
    n_blk = s // CMP_STRIDE
    return pl.pallas_call(
        _compress_kernel,
        grid=(g,),
        in_specs=[pl.BlockSpec((s, 128), lambda gi: (0, gi)),
                  pl.BlockSpec(pe.shape, lambda gi: (0, 0, 0)),
                  pl.BlockSpec((1,) + w1.shape[1:], lambda gi: (l, 0, 0, 0)),
                  pl.BlockSpec((1,) + w2.shape[1:], lambda gi: (l, 0, 0))],
        out_specs=[pl.BlockSpec((1, n_blk, 128), lambda gi: (gi, 0, 0)),
                   pl.BlockSpec((1, 128, n_blk), lambda gi: (gi, 0, 0))],
        out_shape=[jax.ShapeDtypeStruct((g, n_blk, 128), BF16),
                   jax.ShapeDtypeStruct((g, 128, n_blk), BF16)],
        compiler_params=_params(("parallel",)),
        name="compress",
    )(x, pe, w1, w2)


ONES_ROWS = 16


def _with_ones_rows(vT):
    return jnp.concatenate([vT, jnp.ones((ONES_ROWS, vT.shape[1]), vT.dtype)], axis=0)


def _per_head(x, tq):
    return [x[:, hp * tq:(hp + 1) * tq] for hp in range(NSA_HPG)]


CMP_ROW_STEP = 128
PICKED = -3e38


def _cmp_select_kernel(q_ref, kc_ref, vcT_ref, oc_ref, selT_ref, imp_ref, *, tq):
    n_blk = kc_ref.shape[1]
    qi = pl.program_id(1)
    t = qi * tq + lax.broadcasted_iota(jnp.int32, (1, tq), 1)
    any_visible = (t >= CMP_BLOCK - 1).astype(F32)

    def attend(rows):
        kc = kc_ref[0, :rows, :NSA_HEAD_DIM]
        vcT = vcT_ref[0, :, :rows]
        n_idx = lax.broadcasted_iota(jnp.int32, (rows, 1), 0)
        bias = jnp.where((n_idx * CMP_STRIDE + (CMP_BLOCK - 1)) <= t, 0.0, NEG_INF)
        s = _dot(kc, q_ref[0])
        s = jnp.concatenate([x + bias for x in _per_head(s, tq)], axis=1)
        e = jnp.exp2(s - jnp.max(s, axis=0, keepdims=True))
        scale = jnp.concatenate([any_visible] * NSA_HPG, axis=1) / jnp.sum(e, axis=0, keepdims=True)
        p = e * scale
        oc_ref[0] = _dot(vcT, p.astype(BF16))
        p_sum = functools.reduce(lambda a, b: a + b, _per_head(p, tq))
        sj = lax.broadcasted_iota(jnp.int32, (SLC_LANES, rows), 0) * SLC_BLOCK
        ci = lax.broadcasted_iota(jnp.int32, (SLC_LANES, rows), 1) * CMP_STRIDE
        overlap = ((ci < sj + SLC_BLOCK) & (ci + CMP_BLOCK > sj)).astype(BF16)
        p_hi, p_mid, p_lo = _split3(p_sum)
        imp_ref[...] = _dot(overlap, p_hi) + _dot(overlap, p_mid) + _dot(overlap, p_lo)

    n_visible = (qi + 1) * (tq // CMP_STRIDE)
    n_steps = n_blk // CMP_ROW_STEP
    for b in range(1, n_steps + 1):
        below = n_visible <= b * CMP_ROW_STEP
        above = n_visible > (b - 1) * CMP_ROW_STEP
        pl.when(above & below if b < n_steps else above)(functools.partial(attend, b * CMP_ROW_STEP))

    blk = lax.broadcasted_iota(jnp.int32, (SLC_LANES, 1), 0)
    cur = t // SLC_BLOCK
    forced = (blk == 0) | (blk == cur) | (blk == cur - 1)
    causal = blk * SLC_BLOCK <= t
    score = jnp.where(forced, FORCE_SCORE, jnp.where(causal, imp_ref[...], -1.0))
    blk_f = blk.astype(F32)
    work = score
    for _ in range(SLC_TOPN):
        mx = jnp.max(work, axis=0, keepdims=True)
        first = jnp.min(jnp.where(work == mx, blk_f, float(SLC_LANES)), axis=0, keepdims=True)
        work = jnp.where(blk_f == first, PICKED, work)
    selT_ref[0] = jnp.where((work == PICKED) & (score >= 0.0), 1.0, 0.0).astype(selT_ref.dtype)


def _cmp_select(qL, k_cmp, v_cmpT, tq):
    g = NSA_KV_GROUPS
    s = qL.shape[2] // NSA_HPG
    n_blk = k_cmp.shape[1]
    q_spec = pl.BlockSpec((1, NSA_HEAD_DIM, NSA_HPG * tq), lambda gi, qi: (gi, 0, qi))
    return pl.pallas_call(
        functools.partial(_cmp_select_kernel, tq=tq),
        grid=(g, s // tq),
        in_specs=[q_spec,
                  pl.BlockSpec((1, n_blk, 2 * NSA_HEAD_DIM), lambda gi, qi: (gi, 0, 0)),
                  pl.BlockSpec((1, NSA_HEAD_DIM, n_blk), lambda gi, qi: (gi, 1, 0))],
        out_specs=[q_spec,
                   pl.BlockSpec((1, SLC_LANES, tq), lambda gi, qi: (gi, 0, qi))],
        out_shape=[jax.ShapeDtypeStruct(qL.shape, F32),
                   jax.ShapeDtypeStruct((g, SLC_LANES, s), BF16)],
        scratch_shapes=[pltpu.VMEM((SLC_LANES, tq), F32)],
        compiler_params=_params(("parallel", "arbitrary")),
        name="cmp_select",
    )(qL, k_cmp, v_cmpT)


def _win_attn_kernel(q_ref, k0_ref, k1_ref, v0_ref, v1_ref, ow_ref, *, tq):
    dk = NSA_HEAD_DIM
    qi = pl.program_id(1)
    from_prev = lax.broadcasted_iota(jnp.int32, (tq, 1), 0) > lax.broadcasted_iota(jnp.int32, (1, tq), 1)
    k_prev, k_cur = k0_ref[:, dk:], k1_ref[:, dk:]
    v_prev, v_cur = _with_ones_rows(v0_ref[...]), _with_ones_rows(v1_ref[...])

    def attend(has_prev):
        def scores(hp):
            qh = q_ref[0, :, hp * tq:(hp + 1) * tq]
            return jnp.where(from_prev, _dot(k_prev, qh) if has_prev else NEG_INF, _dot(k_cur, qh))

        s_next = scores(0)
        for hp in range(NSA_HPG):
            lanes = slice(hp * tq, (hp + 1) * tq)
            s = s_next
            if hp + 1 < NSA_HPG:
                s_next = scores(hp + 1)
            p = jnp.exp2(s - jnp.max(s, axis=0, keepdims=True)).astype(BF16)
            zero = jnp.zeros_like(p)
            o = _dot(v_cur, jnp.where(from_prev, zero, p))
            if has_prev:
                o = o + _dot(v_prev, jnp.where(from_prev, p, zero))
            ow_ref[0, :, lanes] = o[:dk] / o[dk:dk + 1]

    @pl.when(qi > 0)
    def _():
        attend(True)

    @pl.when(qi == 0)
    def _():
        attend(False)


def _win_attn(qL, kr, vT, tq):
    g = NSA_KV_GROUPS
    s = qL.shape[2] // NSA_HPG
    assert WINDOW == tq
    q_spec = pl.BlockSpec((1, NSA_HEAD_DIM, NSA_HPG * tq), lambda gi, qi: (gi, 0, qi))

    def k_spec(c):
        return pl.BlockSpec((tq, 128), lambda gi, qi: (jnp.maximum(qi - 1 + c, 0), 2 * gi + 1))

    def v_spec(c):
        return pl.BlockSpec((NSA_HEAD_DIM, tq), lambda gi, qi: (2 * gi + 1, jnp.maximum(qi - 1 + c, 0)))

    return pl.pallas_call(
        functools.partial(_win_attn_kernel, tq=tq),
        grid=(g, s // tq),
        in_specs=[q_spec, k_spec(0), k_spec(1), v_spec(0), v_spec(1)],
        out_specs=q_spec,
        out_shape=jax.ShapeDtypeStruct(qL.shape, F32),
        compiler_params=_params(("parallel", "arbitrary")),
        name="win_attn",
    )(qL, kr, kr, vT, vT)


def _slc_attn_kernel(qi_tab, ki_tab, q_ref, k_ref, vT_ref, selT_ref, oc_ref, ow_ref, gT_ref,
                     o_ref, rhs_ref, m_ref, acc_ref, *, tq):
    dk = NSA_HEAD_DIM
    step = pl.program_id(1)
    qi = qi_tab[step]
    ki = ki_tab[step]

    @pl.when(ki == 0)
    def _():
        m_ref[...] = jnp.full_like(m_ref, NEG_INF)
        acc_ref[...] = jnp.zeros_like(acc_ref)
        not_selected = ((selT_ref[0].astype(F32) - 1.0) * (-NEG_INF)).astype(BF16)
        rhs_ref[:SLC_LANES, :] = jnp.concatenate([not_selected] * NSA_HPG, axis=1)
        rhs_ref[SLC_LANES:SLC_LANES + dk, :] = q_ref[0]
        rhs_ref[SLC_LANES + dk:, :] = jnp.zeros((dk, NSA_HPG * tq), BF16)

    def update(sub_tiles):
        chains = [(sub, hp, causal) for sub, causal in sub_tiles for hp in range(NSA_HPG)]
        v_aug = {sub: _with_ones_rows(vT_ref[:, sub * tq:(sub + 1) * tq]) for sub, _ in sub_tiles}
        scores = lambda sub, hp: _dot(k_ref[sub * tq:(sub + 1) * tq, :], rhs_ref[:, hp * tq:(hp + 1) * tq])
        s_next = scores(*chains[0][:2])
        for n, (sub, hp, causal) in enumerate(chains):
            lanes = slice(hp * tq, (hp + 1) * tq)
            s = s_next
            if n + 1 < len(chains):
                s_next = scores(*chains[n + 1][:2])
            if causal:
                s = jnp.where(lax.broadcasted_iota(jnp.int32, (tq, 1), 0)
                              <= lax.broadcasted_iota(jnp.int32, (1, tq), 1), s, NEG_INF)
            m_old = m_ref[:, lanes]
            m_new = jnp.maximum(m_old, jnp.max(s, axis=0, keepdims=True))
            p = jnp.exp2(s - m_new).astype(BF16)
            acc_ref[:, lanes] = jnp.exp2(m_old - m_new) * acc_ref[:, lanes] + _dot(v_aug[sub], p)
            m_ref[:, lanes] = m_new

    def finish():
        sig = _sigmoid_pair(gT_ref[...])[0]

        def gate(br):
            return jnp.concatenate([sig[br * NSA_HPG + hp:br * NSA_HPG + hp + 1] for hp in range(NSA_HPG)], axis=1)

        o_s = acc_ref[:dk, :] / acc_ref[dk:dk + 1, :]
        out = gate(0) * oc_ref[0] + gate(1) * o_s + gate(2) * ow_ref[0]
        o_ref[...] = jnp.concatenate(_per_head(out, tq), axis=0).T.astype(o_ref.dtype)

    ahead = qi - 2 * ki

    @pl.when(ahead >= 2)
    def _():
        update([(0, False), (1, False)])

    @pl.when(ahead == 1)
    def _():
        update([(0, False), (1, True)])
        finish()

    @pl.when(ahead == 0)
    def _():
        update([(0, True)])
        finish()


def _slc_attn(qL, kr, vT, selT, ocL, owL, gT, tq):
    g = NSA_KV_GROUPS
    s = qL.shape[2] // NSA_HPG
    tk = 2 * tq
    pairs = [(qi, ki) for qi in range(s // tq) for ki in range(qi // 2 + 1)]
    qi_tab = jnp.asarray(np.array([p[0] for p in pairs], np.int32))
    ki_tab = jnp.asarray(np.array([p[1] for p in pairs], np.int32))
    q_spec = pl.BlockSpec((1, NSA_HEAD_DIM, NSA_HPG * tq), lambda gi, st, qt, kt: (gi, 0, qt[st]))
    grid_spec = pltpu.PrefetchScalarGridSpec(
        num_scalar_prefetch=2,
        grid=(g, len(pairs)),
        in_specs=[q_spec,
                  pl.BlockSpec((tk, 256), lambda gi, st, qt, kt: (kt[st], gi)),
                  pl.BlockSpec((NSA_HEAD_DIM, tk), lambda gi, st, qt, kt: (2 * gi, kt[st])),
                  pl.BlockSpec((1, SLC_LANES, tq), lambda gi, st, qt, kt: (gi, 0, qt[st])),
                  q_spec, q_spec,
                  pl.BlockSpec((GATE_ROWS, tq), lambda gi, st, qt, kt: (gi, qt[st]))],
        out_specs=pl.BlockSpec((tq, NSA_HPG * NSA_HEAD_DIM), lambda gi, st, qt, kt: (qt[st], gi)),
        scratch_shapes=[pltpu.VMEM((SLC_LANES + 2 * NSA_HEAD_DIM, NSA_HPG * tq), BF16),
                        pltpu.VMEM((1, NSA_HPG * tq), F32),
                        pltpu.VMEM((NSA_HEAD_DIM + ONES_ROWS, NSA_HPG * tq), F32)])
    return pl.pallas_call(
        functools.partial(_slc_attn_kernel, tq=tq),
        grid_spec=grid_spec,
        out_shape=jax.ShapeDtypeStruct((s, NSA_WIDTH), BF16),
        compiler_params=_params(("parallel", "arbitrary")),
        name="slc_attn",
    )(qi_tab, ki_tab, qL, kr, vT, selT, ocL, owL, gT)


def _res_ln_epilogue(acc, x_ref, g_ref, b_ref, xo_ref, xb_ref):
    v = DN_ALPHA * x_ref[...] + acc
    mu = jnp.mean(v, axis=-1, keepdims=True)
    d = v - mu
    var = jnp.mean(d * d, axis=-1, keepdims=True)
    y = d * lax.rsqrt(var + LN_EPS) * g_ref[...] + b_ref[...]
    xo_ref[...] = y
    xb_ref[...] = y.astype(BF16)


def _mix_out_kernel(a1_ref, a2_ref, w_ref, x_ref, g_ref, b_ref, xo_ref, xb_ref):
    k1 = a1_ref.shape[1]
    y = _dot(a1_ref[...], w_ref[:k1, :]) + _dot(a2_ref[...], w_ref[k1:, :])
    _res_ln_epilogue(y, x_ref, g_ref, b_ref, xo_ref, xb_ref)


def _mix_out(a1, a2, w, l, x, gamma, beta, tm):
    s, k1 = a1.shape
    n = w.shape[2]
    row = pl.BlockSpec((tm, n), lambda i: (i, 0))
    vec = pl.BlockSpec((1, n), lambda i: (0, 0))
    return pl.pallas_call(
        _mix_out_kernel,
        grid=(s // tm,),
        in_specs=[pl.BlockSpec((tm, k1), lambda i: (i, 0)),
                  pl.BlockSpec((tm, k1), lambda i: (i, 0)),
                  pl.BlockSpec((None,) + w.shape[1:], lambda i: (l, 0, 0)),
                  row, vec, vec],
        out_specs=[row, row],
        out_shape=[jax.ShapeDtypeStruct((s, n), F32), jax.ShapeDtypeStruct((s, n), BF16)],
        compiler_params=_params(("parallel",)),
        name="mix_out",
    )(a1, a2, w, x, gamma, beta)


def _ffn_down_kernel(a_ref, w_ref, x_ref, g_ref, b_ref, xo_ref, xb_ref, acc_ref):
    k = pl.program_id(1)

    @pl.when(k == 0)
    def _():
        acc_ref[...] = _dot(a_ref[...], w_ref[...])

    @pl.when(k > 0)
    def _():
        acc_ref[...] += _dot(a_ref[...], w_ref[...])

    @pl.when(k == pl.num_programs(1) - 1)
    def _():
        _res_ln_epilogue(acc_ref[...], x_ref, g_ref, b_ref, xo_ref, xb_ref)


def _ffn_down(a, w, l, x, gamma, beta, tm, tk):
    s, kk = a.shape
    n = w.shape[2]
    row = pl.BlockSpec((tm, n), lambda i, k: (i, 0))
    vec = pl.BlockSpec((1, n), lambda i, k: (0, 0))
    res = pl.BlockSpec((tm, n), lambda i, k: (i, 0), pipeline_mode=pl.Buffered(1))
    return pl.pallas_call(
        _ffn_down_kernel,
        grid=(s // tm, kk // tk),
        in_specs=[pl.BlockSpec((tm, tk), lambda i, k: (i, k)),
                  pl.BlockSpec((None, tk, n), lambda i, k: (l, k, 0)),
                  res, vec, vec],
        out_specs=[row, row],
        out_shape=[jax.ShapeDtypeStruct((s, n), F32), jax.ShapeDtypeStruct((s, n), BF16)],
        scratch_shapes=[pltpu.VMEM((tm, n), F32)],
        compiler_params=pltpu.CompilerParams(dimension_semantics=("parallel", "arbitrary"),
                                             vmem_limit_bytes=FFN_DOWN_VMEM_LIMIT),
        name="ffn_down",
    )(a, w, x, gamma, beta)


HALO = 16


def _ffn_up_kernel(a_ref, ap_ref, wg_ref, wu_ref, cwg_ref, cwu_ref, cbg_ref, cbu_ref, o_ref, wgb_ref, wub_ref):
    i = pl.program_id(1)

    @pl.when(i == 0)
    def _():
        wgb_ref[...] = wg_ref[...].astype(BF16)
        wub_ref[...] = wu_ref[...].astype(BF16)

    ap = ap_ref[...]
    ap = jnp.where(i > 0, ap, jnp.zeros_like(ap))
    a = jnp.concatenate([ap, a_ref[...]], axis=0)

    def branch(w_ref, cw_ref, cb_ref):
        h = _dot(a, w_ref[...])
        cw = cw_ref[...]
        return (cw[2:3] * h[HALO:] + cw[1:2] * h[HALO - 1:-1] + cw[0:1] * h[HALO - 2:-2]) + cb_ref[...]

    gate = branch(wgb_ref, cwg_ref, cbg_ref)
    up = branch(wub_ref, cwu_ref, cbu_ref)
    o_ref[...] = (_silu(gate) * up).astype(o_ref.dtype)


def _ffn_up(a, w, l, conv_w, conv_b, tm, tn):
    s, d = a.shape
    nj = D_FF // tn
    return pl.pallas_call(
        _ffn_up_kernel,
        grid=(nj, s // tm),
        in_specs=[pl.BlockSpec((tm, d), lambda j, i: (i, 0)),
                  pl.BlockSpec((HALO, d), lambda j, i: (jnp.maximum(i * (tm // HALO) - 1, 0), 0)),
                  pl.BlockSpec((None, d, tn), lambda j, i: (l, 0, j)),
                  pl.BlockSpec((None, d, tn), lambda j, i: (l, 0, j + nj)),
                  pl.BlockSpec((3, tn), lambda j, i: (0, j)),
                  pl.BlockSpec((3, tn), lambda j, i: (0, j + nj)),
                  pl.BlockSpec((1, tn), lambda j, i: (0, j)),
                  pl.BlockSpec((1, tn), lambda j, i: (0, j + nj))],
        out_specs=pl.BlockSpec((tm, tn), lambda j, i: (i, j)),
        out_shape=jax.ShapeDtypeStruct((s, D_FF), BF16),
        scratch_shapes=[pltpu.VMEM((d, tn), BF16), pltpu.VMEM((d, tn), BF16)],
        compiler_params=_params(("parallel", "arbitrary")),
        name="ffn_up",
    )(a, a, w, w, conv_w, conv_w, conv_b, conv_b)


def _rope_tables(s):
    half = NSA_HEAD_DIM // 2
    inv = ROPE_THETA ** (-jnp.arange(half, dtype=F32) / half)
    ang = jnp.arange(s).astype(F32)[:, None] * inv[None, :]
    cos, sin = jnp.cos(ang), jnp.sin(ang)
    cos64 = jnp.concatenate([cos, cos], -1)
    sin64 = jnp.concatenate([-sin, sin], -1)
    ones, zeros = jnp.ones_like(cos64), jnp.zeros_like(sin64)
    cos_tabs = jnp.stack([jnp.concatenate([cos64, ones], -1), jnp.concatenate([cos64, cos64], -1)])
    sin_tabs = jnp.stack([jnp.concatenate([sin64, zeros], -1), jnp.concatenate([sin64, sin64], -1)])
    q_scale = NSA_HEAD_DIM ** -0.5 * LOG2E
    return cos_tabs, sin_tabs, cos.T * q_scale, sin.T * q_scale


def _nsa_weight_layout(w_nsa):
    nl, d, _ = w_nsa.shape
    g, dk = NSA_KV_GROUPS, NSA_HEAD_DIM
    q = w_nsa[:, :, :NSA_WIDTH]
    kv = w_nsa[:, :, NSA_WIDTH:NSA_WIDTH + 6 * NSA_KV_WIDTH].reshape(nl, d, 6, g, dk)
    kc, vc, ks, vs, kw, vw = (kv[:, :, i] for i in range(6))
    w_rows = jnp.concatenate([jnp.stack([kc, vc], 3).reshape(nl, d, 2 * g * dk),
                              jnp.stack([ks, kw], 3).reshape(nl, d, 2 * g * dk)], axis=2)
    w_v = jnp.stack([vs, vw], 3).reshape(nl, d, 2 * g * dk)
    gt = w_nsa[:, :, NSA_WIDTH + 6 * NSA_KV_WIDTH:].reshape(nl, d, g, NSA_HPG, 3).transpose(0, 1, 2, 4, 3)
    gt = jnp.pad(gt.reshape(nl, d, g, 3 * NSA_HPG), ((0, 0), (0, 0), (0, 0), (0, GATE_ROWS - 3 * NSA_HPG)))
    w_vg = jnp.concatenate([w_v, gt.reshape(nl, d, g * GATE_ROWS)], axis=2)
    return w_rows.astype(BF16), q.transpose(0, 2, 1).astype(BF16), w_vg.transpose(0, 2, 1).astype(BF16)


def _block_diag(a, b):
    top = jnp.concatenate([a, jnp.zeros(a.shape[:-1] + b.shape[-1:], a.dtype)], axis=-1)
    bot = jnp.concatenate([jnp.zeros(b.shape[:-1] + a.shape[-1:], b.dtype), b], axis=-1)
    return jnp.concatenate([top, bot], axis=-2)


def _compress_weight_layout(w1_k, w1_v, w2_k, w2_v):
    nl = w1_k.shape[0]
    per_token = lambda w: w.reshape(nl, CMP_BLOCK, NSA_HEAD_DIM, CMP_HIDDEN)
    return (_block_diag(per_token(w1_k), per_token(w1_v)).astype(BF16),
            _block_diag(w2_k, w2_v).astype(BF16))


def kernel(x, w_in, w_out, hgrn_lb_logits, hgrn_norm_w, cmp_pe_k, cmp_pe_v, cmp_w1_k, cmp_w2_k,
           cmp_w1_v, cmp_w2_v, ln1_g, ln1_b, w_up, conv_w, conv_b, w_down, ln2_g, ln2_b):
    s = x.shape[1]
    g = NSA_KV_GROUPS
    p_lb = jax.nn.softmax(hgrn_lb_logits.astype(F32), axis=0)
    lower_bounds = jnp.cumsum(p_lb, axis=0) - p_lb[0:1]
    cos_tabs, sin_tabs, cosT, sinT = _rope_tables(s)
    hw = 4 * HGRN_WIDTH
    w_hgrn_b, w_out_b, w_down_b = (w.astype(BF16) for w in (w_in[:, :, :hw], w_out, w_down))
    w_rows, w_qT, w_vgT = _nsa_weight_layout(w_in[:, :, hw:])
    cmp_w1, cmp_w2 = _compress_weight_layout(cmp_w1_k, cmp_w1_v, cmp_w2_k, cmp_w2_v)
    xf = x[0]
    xb = xf.astype(BF16)
    for l in range(DEPTH):
        hh = _proj(xb, w_hgrn_b, l, hw, F32, 1024, 512)
        kvc = _proj_rope(xb, w_rows, l, 0, cos_tabs, sin_tabs, 0, F32, 1024, 512)
        kr = _proj_rope(xb, w_rows, l, 1, cos_tabs, sin_tabs, 1, BF16, 1024, 512, block_one_hot=True)
        qL = _projT_rope(w_qT, l, xb, cosT, sinT, NSA_TQ)
        vT, gT = _projT_vg(w_vgT, l, xb, 2 * g * NSA_HEAD_DIM, 1024)
        o_h = _hgrn(hh, lower_bounds[l][None], hgrn_norm_w[l][None], 512, 8)

        pe = jnp.concatenate([cmp_pe_k[l], cmp_pe_v[l]], axis=-1)[:, None, :]
        kv_cmp, kv_cmpT = _compress(kvc, pe, cmp_w1, cmp_w2, l)
        ocL, selT = _cmp_select(qL, kv_cmp, kv_cmpT, NSA_TQ)
        owL = _win_attn(qL, kr, vT, NSA_TQ)
        o_n = _slc_attn(qL, kr, vT, selT, ocL, owL, gT, NSA_TQ)

        xf, xb = _mix_out(o_h, o_n, w_out_b, l, xf, ln1_g[l][None], ln1_b[l][None], 512)
        u = _ffn_up(xb, w_up, l, conv_w[l], conv_b[l][None], 1024, 512)
        xf, xb = _ffn_down(u, w_down_b, l, xf, ln2_g[l][None], ln2_b[l][None], 1024, 512)
    return xf[None]
```

```python
import functools

import numpy as np
import jax
import jax.numpy as jnp
from jax import lax
from jax.experimental import pallas as pl
from jax.experimental.pallas import tpu as pltpu

F32 = jnp.float32
BF16 = jnp.bfloat16

D_MODEL = 2048
DEPTH = 4
HGRN_WIDTH = 1024
HGRN_HEADS = 8
HEAD_LANES = 128
NSA_WIDTH = 1024
NSA_HEAD_DIM = 64
NSA_HEADS = 16
NSA_KV_GROUPS = 4
NSA_HPG = 4
NSA_KV_WIDTH = NSA_KV_GROUPS * NSA_HEAD_DIM
CMP_BLOCK = 32
CMP_STRIDE = 16
CMP_HIDDEN = 256
SLC_BLOCK = 64
SLC_TOPN = 16
SLC_LANES = 128
WINDOW = 512
D_FF = 5632
ROPE_THETA = 10000.0
LN_EPS = 1e-5
RMS_EPS = 1e-6
F_MIN = 1e-30
DN_ALPHA = (2 * DEPTH) ** 0.25
NEG_INF = -1e30
FORCE_SCORE = 1e9

GATE_ROWS = 16
LOG2E = 1.4426950408889634

NSA_TQ = 512
SLC_AHEAD = 3
HGRN_CHUNK = 64
HGRN_SUB = 16
VMEM_LIMIT = 48 * 1024 * 1024
FFN_DOWN_VMEM_LIMIT = 56 * 1024 * 1024


def _params(sem):
    return pltpu.CompilerParams(dimension_semantics=sem, vmem_limit_bytes=VMEM_LIMIT)


def _dot(a, b):
    return jnp.dot(a, b, preferred_element_type=F32)


def _dot_nt(a, b):
    return lax.dot_general(a, b, (((1,), (1,)), ((), ())), preferred_element_type=F32)


def _dot_tn(a, b):
    return lax.dot_general(a, b, (((0,), (0,)), ((), ())), preferred_element_type=F32)


def _split3(x):
    hi = x.astype(BF16)
    r = x - hi.astype(F32)
    mid = r.astype(BF16)
    lo = (r - mid.astype(F32)).astype(BF16)
    return hi, mid, lo


def _sigmoid_pair(z):
    e = jnp.exp(-jnp.abs(z))
    r = 1.0 / (1.0 + e)
    er = e * r
    pos = z >= 0
    return jnp.where(pos, r, er), jnp.where(pos, er, r)


def _silu(x):
    return x * _sigmoid_pair(x)[0]


def _proj_kernel(x_ref, w_ref, o_ref):
    o_ref[...] = _dot(x_ref[...], w_ref[...]).astype(o_ref.dtype)


def _proj(x, w, l, n, out_dtype, tm, tn):
    m, k = x.shape
    return pl.pallas_call(
        _proj_kernel,
        grid=(m // tm, n // tn),
        in_specs=[pl.BlockSpec((tm, k), lambda i, j: (i, 0)),
                  pl.BlockSpec((None, k, tn), lambda i, j: (l, 0, j))],
        out_specs=pl.BlockSpec((tm, tn), lambda i, j: (i, j)),
        out_shape=jax.ShapeDtypeStruct((m, n), out_dtype),
        compiler_params=_params(("parallel", "arbitrary")),
        name="proj",
    )(x, w)


def _proj_rope_kernel(x_ref, w_ref, cos_ref, sin_ref, o_ref, *, tn, block_one_hot):
    h = _dot(x_ref[...], w_ref[...])
    cos = cos_ref[0]
    sin = sin_ref[0]
    tm = cos.shape[0]
    lane = lax.broadcasted_iota(jnp.int32, cos.shape, 1)
    first_half = (lane % NSA_HEAD_DIM) < (NSA_HEAD_DIM // 2)
    if block_one_hot:
        pos = pl.program_id(0) * tm + lax.broadcasted_iota(jnp.int32, cos.shape, 0)
        one_hot = (lane == pos // SLC_BLOCK).astype(o_ref.dtype)
    for c in range(tn // 128):
        hc = h[:, c * 128:(c + 1) * 128]
        rot = jnp.where(first_half, pltpu.roll(hc, 96, axis=1), pltpu.roll(hc, 32, axis=1))
        roped = (hc * cos + rot * sin).astype(o_ref.dtype)
        if block_one_hot:
            o_ref[:, 2 * c * 128:(2 * c + 1) * 128] = one_hot
            o_ref[:, (2 * c + 1) * 128:(2 * c + 2) * 128] = roped
        else:
            o_ref[:, c * 128:(c + 1) * 128] = roped


def _proj_rope(x, w, l, col0, cos_tabs, sin_tabs, tab, out_dtype, tm, tn, block_one_hot=False):
    m, k = x.shape
    n_out = 2 * tn if block_one_hot else tn
    tab_spec = pl.BlockSpec((1, tm, 128), lambda i: (tab, i, 0))
    return pl.pallas_call(
        functools.partial(_proj_rope_kernel, tn=tn, block_one_hot=block_one_hot),
        grid=(m // tm,),
        in_specs=[pl.BlockSpec((tm, k), lambda i: (i, 0)),
                  pl.BlockSpec((None, k, tn), lambda i: (l, 0, col0)),
                  tab_spec, tab_spec],
        out_specs=pl.BlockSpec((tm, n_out), lambda i: (i, 0)),
        out_shape=jax.ShapeDtypeStruct((m, n_out), out_dtype),
        compiler_params=_params(("parallel",)),
        name="proj_rope",
    )(x, w, cos_tabs, sin_tabs)


def _hgrn_chunk(q, z, v, g, lb, nw, st, tri, sub_row):
    C, c = HGRN_CHUNK, HGRN_SUB
    one_minus_lb = 1.0 - lb
    sig, sig_neg = _sigmoid_pair(z)
    f = lb + one_minus_lb * sig
    lf = jnp.log2(jnp.maximum(f, F_MIN))
    k = one_minus_lb * sig_neg
    qs = _silu(q)
    lf_hi, lf_mid, lf_lo = _split3(lf)
    b = _dot(tri, lf_hi) + _dot(tri, lf_mid) + _dot(tri, lf_lo)
    o_inter = _dot_nt((qs * jnp.exp2(b)).astype(BF16), st.astype(BF16))
    v16 = v.astype(BF16)
    outs = []
    for i in range(C // c):
        lo = i * c
        b_i = b[lo:lo + c]
        qs_i = qs[lo:lo + c]
        k_i = k[lo:lo + c]
        v_i = v[lo:lo + c]
        o_i = o_inter[lo:lo + c]
        if i > 0:
            beta = b[lo - 1:lo]
            qt = (qs_i * jnp.exp2(b_i - beta)).astype(BF16)
            kt = (k[:lo] * jnp.exp2(beta - b[:lo])).astype(BF16)
            att = _dot_nt(qt, kt)
            o_i = o_i + _dot(att.astype(BF16), v16[:lo])
        for s in range(c):
            d = jnp.exp2(b_i - b_i[s:s + 1])
            w = jnp.sum(qs_i * d * k_i[s:s + 1], axis=-1, keepdims=True)
            w = jnp.where(sub_row >= s, w, 0.0)
            o_i = o_i + w * v_i[s:s + 1]
        outs.append(o_i)
    o = jnp.concatenate(outs, axis=0)
    b_last = b[C - 1:C]
    kd = (k * jnp.exp2(b_last - b)).astype(BF16)
    st_new = jnp.exp2(b_last) * st + _dot_tn(v16, kd)
    o = o * lax.rsqrt(jnp.mean(o * o, axis=-1, keepdims=True) + RMS_EPS)
    return o * nw * _silu(g), st_new


def _hgrn_kernel(q_ref, z_ref, v_ref, g_ref, lb_ref, nw_ref, o_ref, st_ref, *, tb, hb):
    C, c = HGRN_CHUNK, HGRN_SUB

    @pl.when(pl.program_id(1) == 0)
    def _():
        st_ref[...] = jnp.zeros_like(st_ref)

    tri = (lax.broadcasted_iota(jnp.int32, (C, C), 1)
           <= lax.broadcasted_iota(jnp.int32, (C, C), 0)).astype(BF16)
    sub_row = lax.broadcasted_iota(jnp.int32, (c, 1), 0)

    def chunk(ci, carry):
        r0 = pl.multiple_of(ci * C, C)
        for h in range(hb):
            lanes = slice(h * HEAD_LANES, (h + 1) * HEAD_LANES)
            out, st_new = _hgrn_chunk(q_ref[pl.ds(r0, C), lanes], z_ref[pl.ds(r0, C), lanes],
                                      v_ref[pl.ds(r0, C), lanes], g_ref[pl.ds(r0, C), lanes],
                                      lb_ref[:, lanes], nw_ref[:, lanes], st_ref[h], tri, sub_row)
            st_ref[h] = st_new
            o_ref[pl.ds(r0, C), lanes] = out.astype(o_ref.dtype)
        return carry

    lax.fori_loop(0, tb // C, chunk, 0)


def _hgrn(hh, lb, nw, tb, hb):
    s = hh.shape[0]
    nhb = HGRN_HEADS // hb
    wide = hb * HEAD_LANES

    def col(off):
        return pl.BlockSpec((tb, wide), lambda h, t: (t, off * nhb + h))

    vec = pl.BlockSpec((1, wide), lambda h, t: (0, h))
    return pl.pallas_call(
        functools.partial(_hgrn_kernel, tb=tb, hb=hb),
        grid=(nhb, s // tb),
        in_specs=[col(0), col(1), col(2), col(3), vec, vec],
        out_specs=pl.BlockSpec((tb, wide), lambda h, t: (t, h)),
        out_shape=jax.ShapeDtypeStruct((s, HGRN_WIDTH), BF16),
        scratch_shapes=[pltpu.VMEM((hb, HEAD_LANES, HEAD_LANES), F32)],
        compiler_params=_params(("parallel", "arbitrary")),
        name="hgrn2",
    )(hh, hh, hh, hh, lb, nw)


def _projT_rope_kernel(w_ref, x_ref, cos_ref, sin_ref, o_ref, *, tq):
    h = _dot_nt(w_ref[...], x_ref[...])
    cos = cos_ref[...]
    sin = sin_ref[...]
    half = NSA_HEAD_DIM // 2
    for hp in range(NSA_HPG):
        r = hp * NSA_HEAD_DIM
        x1 = h[r:r + half]
        x2 = h[r + half:r + 2 * half]
        o_ref[0, :half, hp * tq:(hp + 1) * tq] = (x1 * cos - x2 * sin).astype(o_ref.dtype)
        o_ref[0, half:, hp * tq:(hp + 1) * tq] = (x2 * cos + x1 * sin).astype(o_ref.dtype)


def _projT_rope(wT, l, x, cosT, sinT, tq):
    _, n, d = wT.shape
    s = x.shape[0]
    half = NSA_HEAD_DIM // 2
    gw = NSA_HPG * NSA_HEAD_DIM
    return pl.pallas_call(
        functools.partial(_projT_rope_kernel, tq=tq),
        grid=(s // tq, n // gw),
        in_specs=[pl.BlockSpec((None, gw, d), lambda i, j: (l, j, 0)),
                  pl.BlockSpec((tq, d), lambda i, j: (i, 0)),
                  pl.BlockSpec((half, tq), lambda i, j: (0, i)),
                  pl.BlockSpec((half, tq), lambda i, j: (0, i))],
        out_specs=pl.BlockSpec((1, NSA_HEAD_DIM, NSA_HPG * tq), lambda i, j: (j, 0, i)),
        out_shape=jax.ShapeDtypeStruct((n // gw, NSA_HEAD_DIM, NSA_HPG * s), BF16),
        compiler_params=_params(("parallel", "arbitrary")),
        name="projT_rope",
    )(wT, x, cosT, sinT)


def _projT_vg_kernel(w_ref, x_ref, v_ref, g_ref):
    h = _dot_nt(w_ref[...], x_ref[...])
    nv = v_ref.shape[0]
    v_ref[...] = h[:nv].astype(v_ref.dtype)
    g_ref[...] = h[nv:]


def _projT_vg(wT, l, x, nv, ts):
    _, n, d = wT.shape
    s = x.shape[0]
    return pl.pallas_call(
        _projT_vg_kernel,
        grid=(s // ts,),
        in_specs=[pl.BlockSpec((None, n, d), lambda i: (l, 0, 0)),
                  pl.BlockSpec((ts, d), lambda i: (i, 0))],
        out_specs=[pl.BlockSpec((nv, ts), lambda i: (0, i)),
                   pl.BlockSpec((n - nv, ts), lambda i: (0, i))],
        out_shape=[jax.ShapeDtypeStruct((nv, s), BF16), jax.ShapeDtypeStruct((n - nv, s), F32)],
        compiler_params=_params(("parallel",)),
        name="projT_vg",
    )(wT, x)


def _compress_kernel(x_ref, pe_ref, w1_ref, w2_ref, kv_ref, kvT_ref):
    n_blk = x_ref.shape[0] // CMP_STRIDE
    width = 2 * CMP_HIDDEN
    top = jnp.zeros((n_blk, width), F32)
    bot = jnp.zeros((n_blk, width), F32)
    for j in range(CMP_STRIDE):
        rows = x_ref[pl.ds(j, n_blk, stride=CMP_STRIDE), :]
        top = top + _dot((rows + pe_ref[j]).astype(BF16), w1_ref[0, j])
        bot = bot + _dot((rows + pe_ref[CMP_STRIDE + j]).astype(BF16), w1_ref[0, CMP_STRIDE + j])
    hid = top + pltpu.roll(bot, n_blk - 1, axis=0)
    out = _dot(_silu(hid).astype(BF16), w2_ref[0])
    kv_ref[0] = out.astype(kv_ref.dtype)
    kvT_ref[0] = out.T.astype(kvT_ref.dtype)


def _compress(x, pe, w1, w2, l):
    s = x.shape[0]
    g = NSA_KV_GROUPS
    n_blk = s // CMP_STRIDE
    return pl.pallas_call(
        _compress_kernel,
        grid=(g,),
        in_specs=[pl.BlockSpec((s, 128), lambda gi: (0, gi)),
                  pl.BlockSpec(pe.shape, lambda gi: (0, 0, 0)),
                  pl.BlockSpec((1,) + w1.shape[1:], lambda gi: (l, 0, 0, 0)),
                  pl.BlockSpec((1,) + w2.shape[1:], lambda gi: (l, 0, 0))],
        out_specs=[pl.BlockSpec((1, n_blk, 128), lambda gi: (gi, 0, 0)),
                   pl.BlockSpec((1, 128, n_blk), lambda gi: (gi, 0, 0))],
        out_shape=[jax.ShapeDtypeStruct((g, n_blk, 128), BF16),
                   jax.ShapeDtypeStruct((g, 128, n_blk), BF16)],
        compiler_params=_params(("parallel",)),
        name="compress",
    )(x, pe, w1, w2)


ONES_ROWS = 16


def _with_ones_rows(vT):
    return jnp.concatenate([vT, jnp.ones((ONES_ROWS, vT.shape[1]), vT.dtype)], axis=0)


def _per_head(x, tq):
    return [x[:, hp * tq:(hp + 1) * tq] for hp in range(NSA_HPG)]


CMP_ROW_STEP = 128
PICKED = -3e38


def _cmp_select_kernel(q_ref, kc_ref, vcT_ref, oc_ref, selT_ref, imp_ref, *, tq):
    n_blk = kc_ref.shape[1]
    qi = pl.program_id(1)
    t = qi * tq + lax.broadcasted_iota(jnp.int32, (1, tq), 1)
    any_visible = (t >= CMP_BLOCK - 1).astype(F32)

    def attend(rows):
        kc = kc_ref[0, :rows, :NSA_HEAD_DIM]
        vcT = vcT_ref[0, :, :rows]
        n_idx = lax.broadcasted_iota(jnp.int32, (rows, 1), 0)
        bias = jnp.where((n_idx * CMP_STRIDE + (CMP_BLOCK - 1)) <= t, 0.0, NEG_INF)
        s = _dot(kc, q_ref[0])
        s = jnp.concatenate([x + bias for x in _per_head(s, tq)], axis=1)
        e = jnp.exp2(s - jnp.max(s, axis=0, keepdims=True))
        scale = jnp.concatenate([any_visible] * NSA_HPG, axis=1) / jnp.sum(e, axis=0, keepdims=True)
        p = e * scale
        oc_ref[0] = _dot(vcT, p.astype(BF16))
        p_sum = functools.reduce(lambda a, b: a + b, _per_head(p, tq))
        sj = lax.broadcasted_iota(jnp.int32, (SLC_LANES, rows), 0) * SLC_BLOCK
        ci = lax.broadcasted_iota(jnp.int32, (SLC_LANES, rows), 1) * CMP_STRIDE
        overlap = ((ci < sj + SLC_BLOCK) & (ci + CMP_BLOCK > sj)).astype(BF16)
        p_hi, p_mid, p_lo = _split3(p_sum)
        imp_ref[...] = _dot(overlap, p_hi) + _dot(overlap, p_mid) + _dot(overlap, p_lo)

    n_visible = (qi + 1) * (tq // CMP_STRIDE)
    n_steps = n_blk // CMP_ROW_STEP
    for b in range(1, n_steps + 1):
        below = n_visible <= b * CMP_ROW_STEP
        above = n_visible > (b - 1) * CMP_ROW_STEP
        pl.when(above & below if b < n_steps else above)(functools.partial(attend, b * CMP_ROW_STEP))

    blk = lax.broadcasted_iota(jnp.int32, (SLC_LANES, 1), 0)
    cur = t // SLC_BLOCK
    forced = (blk == 0) | (blk == cur) | (blk == cur - 1)
    causal = blk * SLC_BLOCK <= t
    score = jnp.where(forced, FORCE_SCORE, jnp.where(causal, imp_ref[...], -1.0))
    blk_f = blk.astype(F32)
    work = score
    for _ in range(SLC_TOPN):
        mx = jnp.max(work, axis=0, keepdims=True)
        first = jnp.min(jnp.where(work == mx, blk_f, float(SLC_LANES)), axis=0, keepdims=True)
        work = jnp.where(blk_f == first, PICKED, work)
    selT_ref[0] = jnp.where((work == PICKED) & (score >= 0.0), 1.0, 0.0).astype(selT_ref.dtype)


def _cmp_select(qL, k_cmp, v_cmpT, tq):
    g = NSA_KV_GROUPS
    s = qL.shape[2] // NSA_HPG
    n_blk = k_cmp.shape[1]
    q_spec = pl.BlockSpec((1, NSA_HEAD_DIM, NSA_HPG * tq), lambda gi, qi: (gi, 0, qi))
    return pl.pallas_call(
        functools.partial(_cmp_select_kernel, tq=tq),
        grid=(g, s // tq),
        in_specs=[q_spec,
                  pl.BlockSpec((1, n_blk, 2 * NSA_HEAD_DIM), lambda gi, qi: (gi, 0, 0)),
                  pl.BlockSpec((1, NSA_HEAD_DIM, n_blk), lambda gi, qi: (gi, 1, 0))],
        out_specs=[q_spec,
                   pl.BlockSpec((1, SLC_LANES, tq), lambda gi, qi: (gi, 0, qi))],
        out_shape=[jax.ShapeDtypeStruct(qL.shape, F32),
                   jax.ShapeDtypeStruct((g, SLC_LANES, s), BF16)],
        scratch_shapes=[pltpu.VMEM((SLC_LANES, tq), F32)],
        compiler_params=_params(("parallel", "arbitrary")),
        name="cmp_select",
    )(qL, k_cmp, v_cmpT)


def _win_attn_kernel(q_ref, k0_ref, k1_ref, v0_ref, v1_ref, ow_ref, *, tq):
    dk = NSA_HEAD_DIM
    qi = pl.program_id(1)
    from_prev = lax.broadcasted_iota(jnp.int32, (tq, 1), 0) > lax.broadcasted_iota(jnp.int32, (1, tq), 1)
    k_prev, k_cur = k0_ref[:, dk:], k1_ref[:, dk:]
    v_prev, v_cur = _with_ones_rows(v0_ref[...]), _with_ones_rows(v1_ref[...])

    def attend(has_prev):
        def scores(hp):
            qh = q_ref[0, :, hp * tq:(hp + 1) * tq]
            return jnp.where(from_prev, _dot(k_prev, qh) if has_prev else NEG_INF, _dot(k_cur, qh))

        s_next = scores(0)
        for hp in range(NSA_HPG):
            lanes = slice(hp * tq, (hp + 1) * tq)
            s = s_next
            if hp + 1 < NSA_HPG:
                s_next = scores(hp + 1)
            p = jnp.exp2(s - jnp.max(s, axis=0, keepdims=True)).astype(BF16)
            zero = jnp.zeros_like(p)
            o = _dot(v_cur, jnp.where(from_prev, zero, p))
            if has_prev:
                o = o + _dot(v_prev, jnp.where(from_prev, p, zero))
            ow_ref[0, :, lanes] = o[:dk] / o[dk:dk + 1]

    @pl.when(qi > 0)
    def _():
        attend(True)

    @pl.when(qi == 0)
    def _():
        attend(False)


def _win_attn(qL, kr, vT, tq):
    g = NSA_KV_GROUPS
    s = qL.shape[2] // NSA_HPG
    assert WINDOW == tq
    q_spec = pl.BlockSpec((1, NSA_HEAD_DIM, NSA_HPG * tq), lambda gi, qi: (gi, 0, qi))

    def k_spec(c):
        return pl.BlockSpec((tq, 128), lambda gi, qi: (jnp.maximum(qi - 1 + c, 0), 2 * gi + 1))

    def v_spec(c):
        return pl.BlockSpec((NSA_HEAD_DIM, tq), lambda gi, qi: (2 * gi + 1, jnp.maximum(qi - 1 + c, 0)))

    return pl.pallas_call(
        functools.partial(_win_attn_kernel, tq=tq),
        grid=(g, s // tq),
        in_specs=[q_spec, k_spec(0), k_spec(1), v_spec(0), v_spec(1)],
        out_specs=q_spec,
        out_shape=jax.ShapeDtypeStruct(qL.shape, F32),
        compiler_params=_params(("parallel", "arbitrary")),
        name="win_attn",
    )(qL, kr, kr, vT, vT)


def _slc_attn_kernel(qi_tab, ki_tab, q_ref, k_ref, vT_ref, selT_ref, oc_ref, ow_ref, gT_ref,
                     o_ref, rhs_ref, m_ref, acc_ref, *, tq):
    dk = NSA_HEAD_DIM
    step = pl.program_id(1)
    qi = qi_tab[step]
    ki = ki_tab[step]

    @pl.when(ki == 0)
    def _():
        m_ref[...] = jnp.full_like(m_ref, NEG_INF)
        acc_ref[...] = jnp.zeros_like(acc_ref)
        not_selected = ((selT_ref[0].astype(F32) - 1.0) * (-NEG_INF)).astype(BF16)
        rhs_ref[:SLC_LANES, :] = jnp.concatenate([not_selected] * NSA_HPG, axis=1)
        rhs_ref[SLC_LANES:SLC_LANES + dk, :] = q_ref[0]
        rhs_ref[SLC_LANES + dk:, :] = jnp.zeros((dk, NSA_HPG * tq), BF16)

    def update(sub_tiles):
        chains = [(sub, hp, causal) for sub, causal in sub_tiles for hp in range(NSA_HPG)]
        v_aug = {sub: _with_ones_rows(vT_ref[:, sub * tq:(sub + 1) * tq]) for sub, _ in sub_tiles}
        scores = lambda sub, hp: _dot(k_ref[sub * tq:(sub + 1) * tq, :], rhs_ref[:, hp * tq:(hp + 1) * tq])
        pending = [scores(*ch[:2]) for ch in chains[:SLC_AHEAD]]
        for n, (sub, hp, causal) in enumerate(chains):
            lanes = slice(hp * tq, (hp + 1) * tq)
            s = pending.pop(0)
            if n + SLC_AHEAD < len(chains):
                pending.append(scores(*chains[n + SLC_AHEAD][:2]))
            if causal:
                s = jnp.where(lax.broadcasted_iota(jnp.int32, (tq, 1), 0)
                              <= lax.broadcasted_iota(jnp.int32, (1, tq), 1), s, NEG_INF)
            m_old = m_ref[:, lanes]
            m_new = jnp.maximum(m_old, jnp.max(s, axis=0, keepdims=True))
            p = jnp.exp2(s - m_new).astype(BF16)
            acc_ref[:, lanes] = jnp.exp2(m_old - m_new) * acc_ref[:, lanes] + _dot(v_aug[sub], p)
            m_ref[:, lanes] = m_new

    def finish():
        sig = _sigmoid_pair(gT_ref[...])[0]

        def gate(br):
            return jnp.concatenate([sig[br * NSA_HPG + hp:br * NSA_HPG + hp + 1] for hp in range(NSA_HPG)], axis=1)

        o_s = acc_ref[:dk, :] / acc_ref[dk:dk + 1, :]
        out = gate(0) * oc_ref[0] + gate(1) * o_s + gate(2) * ow_ref[0]
        o_ref[...] = jnp.concatenate(_per_head(out, tq), axis=0).T.astype(o_ref.dtype)

    ahead = qi - 2 * ki

    @pl.when(ahead >= 2)
    def _():
        update([(0, False), (1, False)])

    @pl.when(ahead == 1)
    def _():
        update([(0, False), (1, True)])
        finish()

    @pl.when(ahead == 0)
    def _():
        update([(0, True)])
        finish()


def _slc_attn(qL, kr, vT, selT, ocL, owL, gT, tq):
    g = NSA_KV_GROUPS
    s = qL.shape[2] // NSA_HPG
    tk = 2 * tq
    pairs = [(qi, ki) for qi in range(s // tq) for ki in range(qi // 2 + 1)]
    qi_tab = jnp.asarray(np.array([p[0] for p in pairs], np.int32))
    ki_tab = jnp.asarray(np.array([p[1] for p in pairs], np.int32))
    q_spec = pl.BlockSpec((1, NSA_HEAD_DIM, NSA_HPG * tq), lambda gi, st, qt, kt: (gi, 0, qt[st]))
    grid_spec = pltpu.PrefetchScalarGridSpec(
        num_scalar_prefetch=2,
        grid=(g, len(pairs)),
        in_specs=[q_spec,
                  pl.BlockSpec((tk, 256), lambda gi, st, qt, kt: (kt[st], gi)),
                  pl.BlockSpec((NSA_HEAD_DIM, tk), lambda gi, st, qt, kt: (2 * gi, kt[st])),
                  pl.BlockSpec((1, SLC_LANES, tq), lambda gi, st, qt, kt: (gi, 0, qt[st])),
                  q_spec, q_spec,
                  pl.BlockSpec((GATE_ROWS, tq), lambda gi, st, qt, kt: (gi, qt[st]))],
        out_specs=pl.BlockSpec((tq, NSA_HPG * NSA_HEAD_DIM), lambda gi, st, qt, kt: (qt[st], gi)),
        scratch_shapes=[pltpu.VMEM((SLC_LANES + 2 * NSA_HEAD_DIM, NSA_HPG * tq), BF16),
                        pltpu.VMEM((1, NSA_HPG * tq), F32),
                        pltpu.VMEM((NSA_HEAD_DIM + ONES_ROWS, NSA_HPG * tq), F32)])
    return pl.pallas_call(
        functools.partial(_slc_attn_kernel, tq=tq),
        grid_spec=grid_spec,
        out_shape=jax.ShapeDtypeStruct((s, NSA_WIDTH), BF16),
        compiler_params=_params(("parallel", "arbitrary")),
        name="slc_attn",
    )(qi_tab, ki_tab, qL, kr, vT, selT, ocL, owL, gT)


def _res_ln_epilogue(acc, x_ref, g_ref, b_ref, xo_ref, xb_ref):
    v = DN_ALPHA * x_ref[...] + acc
    mu = jnp.mean(v, axis=-1, keepdims=True)
    d = v - mu
    var = jnp.mean(d * d, axis=-1, keepdims=True)
    y = d * lax.rsqrt(var + LN_EPS) * g_ref[...] + b_ref[...]
    xo_ref[...] = y
    xb_ref[...] = y.astype(BF16)


def _mix_out_kernel(a1_ref, a2_ref, w_ref, x_ref, g_ref, b_ref, xo_ref, xb_ref):
    k1 = a1_ref.shape[1]
    y = _dot(a1_ref[...], w_ref[:k1, :]) + _dot(a2_ref[...], w_ref[k1:, :])
    _res_ln_epilogue(y, x_ref, g_ref, b_ref, xo_ref, xb_ref)


def _mix_out(a1, a2, w, l, x, gamma, beta, tm):
    s, k1 = a1.shape
    n = w.shape[2]
    row = pl.BlockSpec((tm, n), lambda i: (i, 0))
    vec = pl.BlockSpec((1, n), lambda i: (0, 0))
    return pl.pallas_call(
        _mix_out_kernel,
        grid=(s // tm,),
        in_specs=[pl.BlockSpec((tm, k1), lambda i: (i, 0)),
                  pl.BlockSpec((tm, k1), lambda i: (i, 0)),
                  pl.BlockSpec((None,) + w.shape[1:], lambda i: (l, 0, 0)),
                  row, vec, vec],
        out_specs=[row, row],
        out_shape=[jax.ShapeDtypeStruct((s, n), F32), jax.ShapeDtypeStruct((s, n), BF16)],
        compiler_params=_params(("parallel",)),
        name="mix_out",
    )(a1, a2, w, x, gamma, beta)


def _ffn_down_kernel(a_ref, w_ref, x_ref, g_ref, b_ref, xo_ref, xb_ref, acc_ref):
    k = pl.program_id(1)

    @pl.when(k == 0)
    def _():
        acc_ref[...] = _dot(a_ref[...], w_ref[...])

    @pl.when(k > 0)
    def _():
        acc_ref[...] += _dot(a_ref[...], w_ref[...])

    @pl.when(k == pl.num_programs(1) - 1)
    def _():
        _res_ln_epilogue(acc_ref[...], x_ref, g_ref, b_ref, xo_ref, xb_ref)


def _ffn_down(a, w, l, x, gamma, beta, tm, tk):
    s, kk = a.shape
    n = w.shape[2]
    row = pl.BlockSpec((tm, n), lambda i, k: (i, 0))
    vec = pl.BlockSpec((1, n), lambda i, k: (0, 0))
    res = pl.BlockSpec((tm, n), lambda i, k: (i, 0), pipeline_mode=pl.Buffered(1))
    return pl.pallas_call(
        _ffn_down_kernel,
        grid=(s // tm, kk // tk),
        in_specs=[pl.BlockSpec((tm, tk), lambda i, k: (i, k)),
                  pl.BlockSpec((None, tk, n), lambda i, k: (l, k, 0)),
                  res, vec, vec],
        out_specs=[row, row],
        out_shape=[jax.ShapeDtypeStruct((s, n), F32), jax.ShapeDtypeStruct((s, n), BF16)],
        scratch_shapes=[pltpu.VMEM((tm, n), F32)],
        compiler_params=pltpu.CompilerParams(dimension_semantics=("parallel", "arbitrary"),
                                             vmem_limit_bytes=FFN_DOWN_VMEM_LIMIT),
        name="ffn_down",
    )(a, w, x, gamma, beta)


HALO = 16


def _ffn_up_kernel(a_ref, ap_ref, wg_ref, wu_ref, cwg_ref, cwu_ref, cbg_ref, cbu_ref, o_ref, wgb_ref, wub_ref):
    i = pl.program_id(1)

    @pl.when(i == 0)
    def _():
        wgb_ref[...] = wg_ref[...].astype(BF16)
        wub_ref[...] = wu_ref[...].astype(BF16)

    ap = ap_ref[...]
    ap = jnp.where(i > 0, ap, jnp.zeros_like(ap))
    a = jnp.concatenate([ap, a_ref[...]], axis=0)

    def branch(w_ref, cw_ref, cb_ref):
        h = _dot(a, w_ref[...])
        cw = cw_ref[...]
        return (cw[2:3] * h[HALO:] + cw[1:2] * h[HALO - 1:-1] + cw[0:1] * h[HALO - 2:-2]) + cb_ref[...]

    gate = branch(wgb_ref, cwg_ref, cbg_ref)
    up = branch(wub_ref, cwu_ref, cbu_ref)
    o_ref[...] = (_silu(gate) * up).astype(o_ref.dtype)


def _ffn_up(a, w, l, conv_w, conv_b, tm, tn):
    s, d = a.shape
    nj = D_FF // tn
    return pl.pallas_call(
        _ffn_up_kernel,
        grid=(nj, s // tm),
        in_specs=[pl.BlockSpec((tm, d), lambda j, i: (i, 0)),
                  pl.BlockSpec((HALO, d), lambda j, i: (jnp.maximum(i * (tm // HALO) - 1, 0), 0)),
                  pl.BlockSpec((None, d, tn), lambda j, i: (l, 0, j)),
                  pl.BlockSpec((None, d, tn), lambda j, i: (l, 0, j + nj)),
                  pl.BlockSpec((3, tn), lambda j, i: (0, j)),
                  pl.BlockSpec((3, tn), lambda j, i: (0, j + nj)),
                  pl.BlockSpec((1, tn), lambda j, i: (0, j)),
                  pl.BlockSpec((1, tn), lambda j, i: (0, j + nj))],
        out_specs=pl.BlockSpec((tm, tn), lambda j, i: (i, j)),
        out_shape=jax.ShapeDtypeStruct((s, D_FF), BF16),
        scratch_shapes=[pltpu.VMEM((d, tn), BF16), pltpu.VMEM((d, tn), BF16)],
        compiler_params=_params(("parallel", "arbitrary")),
        name="ffn_up",
    )(a, a, w, w, conv_w, conv_w, conv_b, conv_b)


def _rope_tables(s):
    half = NSA_HEAD_DIM // 2
    inv = ROPE_THETA ** (-jnp.arange(half, dtype=F32) / half)
    ang = jnp.arange(s).astype(F32)[:, None] * inv[None, :]
    cos, sin = jnp.cos(ang), jnp.sin(ang)
    cos64 = jnp.concatenate([cos, cos], -1)
    sin64 = jnp.concatenate([-sin, sin], -1)
    ones, zeros = jnp.ones_like(cos64), jnp.zeros_like(sin64)
    cos_tabs = jnp.stack([jnp.concatenate([cos64, ones], -1), jnp.concatenate([cos64, cos64], -1)])
    sin_tabs = jnp.stack([jnp.concatenate([sin64, zeros], -1), jnp.concatenate([sin64, sin64], -1)])
    q_scale = NSA_HEAD_DIM ** -0.5 * LOG2E
    return cos_tabs, sin_tabs, cos.T * q_scale, sin.T * q_scale


def _nsa_weight_layout(w_nsa):
    nl, d, _ = w_nsa.shape
    g, dk = NSA_KV_GROUPS, NSA_HEAD_DIM
    q = w_nsa[:, :, :NSA_WIDTH]
    kv = w_nsa[:, :, NSA_WIDTH:NSA_WIDTH + 6 * NSA_KV_WIDTH].reshape(nl, d, 6, g, dk)
    kc, vc, ks, vs, kw, vw = (kv[:, :, i] for i in range(6))
    w_rows = jnp.concatenate([jnp.stack([kc, vc], 3).reshape(nl, d, 2 * g * dk),
                              jnp.stack([ks, kw], 3).reshape(nl, d, 2 * g * dk)], axis=2)
    w_v = jnp.stack([vs, vw], 3).reshape(nl, d, 2 * g * dk)
    gt = w_nsa[:, :, NSA_WIDTH + 6 * NSA_KV_WIDTH:].reshape(nl, d, g, NSA_HPG, 3).transpose(0, 1, 2, 4, 3)
    gt = jnp.pad(gt.reshape(nl, d, g, 3 * NSA_HPG), ((0, 0), (0, 0), (0, 0), (0, GATE_ROWS - 3 * NSA_HPG)))
    w_vg = jnp.concatenate([w_v, gt.reshape(nl, d, g * GATE_ROWS)], axis=2)
    return w_rows.astype(BF16), q.transpose(0, 2, 1).astype(BF16), w_vg.transpose(0, 2, 1).astype(BF16)


def _block_diag(a, b):
    top = jnp.concatenate([a, jnp.zeros(a.shape[:-1] + b.shape[-1:], a.dtype)], axis=-1)
    bot = jnp.concatenate([jnp.zeros(b.shape[:-1] + a.shape[-1:], b.dtype), b], axis=-1)
    return jnp.concatenate([top, bot], axis=-2)


def _compress_weight_layout(w1_k, w1_v, w2_k, w2_v):
    nl = w1_k.shape[0]
    per_token = lambda w: w.reshape(nl, CMP_BLOCK, NSA_HEAD_DIM, CMP_HIDDEN)
    return (_block_diag(per_token(w1_k), per_token(w1_v)).astype(BF16),
            _block_diag(w2_k, w2_v).astype(BF16))


def kernel(x, w_in, w_out, hgrn_lb_logits, hgrn_norm_w, cmp_pe_k, cmp_pe_v, cmp_w1_k, cmp_w2_k,
           cmp_w1_v, cmp_w2_v, ln1_g, ln1_b, w_up, conv_w, conv_b, w_down, ln2_g, ln2_b):
    s = x.shape[1]
    g = NSA_KV_GROUPS
    p_lb = jax.nn.softmax(hgrn_lb_logits.astype(F32), axis=0)
    lower_bounds = jnp.cumsum(p_lb, axis=0) - p_lb[0:1]
    cos_tabs, sin_tabs, cosT, sinT = _rope_tables(s)
    hw = 4 * HGRN_WIDTH
    w_hgrn_b, w_out_b, w_down_b = (w.astype(BF16) for w in (w_in[:, :, :hw], w_out, w_down))
    w_rows, w_qT, w_vgT = _nsa_weight_layout(w_in[:, :, hw:])
    cmp_w1, cmp_w2 = _compress_weight_layout(cmp_w1_k, cmp_w1_v, cmp_w2_k, cmp_w2_v)
    xf = x[0]
    xb = xf.astype(BF16)
    for l in range(DEPTH):
        hh = _proj(xb, w_hgrn_b, l, hw, F32, 1024, 512)
        kvc = _proj_rope(xb, w_rows, l, 0, cos_tabs, sin_tabs, 0, F32, 1024, 512)
        kr = _proj_rope(xb, w_rows, l, 1, cos_tabs, sin_tabs, 1, BF16, 1024, 512, block_one_hot=True)
        qL = _projT_rope(w_qT, l, xb, cosT, sinT, NSA_TQ)
        vT, gT = _projT_vg(w_vgT, l, xb, 2 * g * NSA_HEAD_DIM, 1024)
        o_h = _hgrn(hh, lower_bounds[l][None], hgrn_norm_w[l][None], 512, 8)

        pe = jnp.concatenate([cmp_pe_k[l], cmp_pe_v[l]], axis=-1)[:, None, :]
        kv_cmp, kv_cmpT = _compress(kvc, pe, cmp_w1, cmp_w2, l)
        ocL, selT = _cmp_select(qL, kv_cmp, kv_cmpT, NSA_TQ)
        owL = _win_attn(qL, kr, vT, NSA_TQ)
        o_n = _slc_attn(qL, kr, vT, selT, ocL, owL, gT, NSA_TQ)

        xf, xb = _mix_out(o_h, o_n, w_out_b, l, xf, ln1_g[l][None], ln1_b[l][None], 512)
        u = _ffn_up(xb, w_up, l, conv_w[l], conv_b[l][None], 1024, 512)
        xf, xb = _ffn_down(u, w_down_b, l, xf, ln2_g[l][None], ln2_b[l][None], 1024, 512)
    return xf[None]
```
